```python
import jax, jax.numpy as jnp
from jax import lax
import numpy as np

D_MODEL = 1024
BATCH = 8
SEQ = 4096
DEPTH = 1

MEM_LEN = 256
RW_HEADS = 8
RW_HEAD_DIM = 64
RW_WIDTH = RW_HEADS * RW_HEAD_DIM
DECAY_LORA = 64
AAA_LORA = 64
GATE_LORA = 128
RW_COLS = 3 * RW_WIDTH + DECAY_LORA + AAA_LORA + GATE_LORA
GN_EPS = 64e-5
ML_HEADS = 4
ML_HEAD_DIM = 128
ML_WIDTH = ML_HEADS * ML_HEAD_DIM
ML_CONV = 4
ML_CHUNK = 64
ML_COLS = 4 * ML_WIDTH + 2 * ML_HEADS
ML_NORM_EPS = 1e-6
IN_COLS = RW_COLS + ML_COLS + 2 * D_MODEL
D_FF = 2816
FFN_HALF = 0.5
XA_HEADS = 4
XA_HEAD_DIM = D_MODEL // XA_HEADS
NORM_EPS = 1e-5

kernel_name = "hybrid_rwkv7_mlstm_macaron_memxattn"


def _split(z, sizes):
    idx = np.cumsum(sizes)[:-1].tolist()
    return jnp.split(z, idx, axis=-1)


def _shift(u, j):
    if j == 0:
        return u
    return jnp.pad(u[:, :-j], ((0, 0), (j, 0), (0, 0)))


def _rmsnorm(x, g):
    xf = x.astype(jnp.float32)
    r = xf * lax.rsqrt(jnp.mean(xf * xf, axis=-1, keepdims=True) + NORM_EPS)
    return (r * g.astype(jnp.float32)).astype(x.dtype)


def _swiglu(h, w_gate, w_up, w_down):
    return (jax.nn.silu(h @ w_gate) * (h @ w_up)) @ w_down


def _causal_conv(u, w, b):
    out = b
    for j in range(ML_CONV):
        out = out + w[j] * _shift(u, j)
    return out


def _rwkv7_scan(r, decay, k, v, kk, kka):
    bsz, _, nh, n = r.shape

    def step(state, xs):
        r_t, w_t, k_t, v_t, kk_t, b_t = xs
        sa = jnp.einsum('bhvk,bhk->bhv', state, -kk_t)
        state = (state * w_t[:, :, None, :] + sa[..., None] * b_t[:, :, None, :]
                 + v_t[..., None] * k_t[:, :, None, :])
        return state, jnp.einsum('bhvk,bhk->bhv', state, r_t)

    xs = tuple(jnp.moveaxis(t, 1, 0) for t in (r, decay, k, v, kk, kka))
    init = jnp.zeros((bsz, nh, n, n), jnp.float32)
    _, y = lax.scan(step, init, xs)
    return jnp.moveaxis(y, 0, 1)


def _mlstm_chunkwise(q, k, v, i_pre, f_pre):
    bsz, seq, nh, dk = q.shape
    dv = v.shape[-1]
    nc, cl = seq // ML_CHUNK, ML_CHUNK

    def chunks(t):
        t = t.astype(jnp.float32).reshape((bsz, nc, cl, nh) + t.shape[3:])
        return jnp.moveaxis(t, 3, 1)

    q = chunks(q) * (dk ** -0.5)
    k, v = chunks(k), chunks(v)
    ig = chunks(i_pre)
    b = jnp.cumsum(jax.nn.log_sigmoid(chunks(f_pre)), axis=-1)
    b_end = b[..., -1]

    log_w_end = b_end[..., None] - b + ig
    m_chunk = jnp.max(log_w_end, axis=-1)
    w_end = jnp.exp(log_w_end - m_chunk[..., None])
    c_chunk = jnp.einsum('bhcl,bhclv,bhclk->bhcvk', w_end, v, k)
    n_chunk = jnp.einsum('bhcl,bhclk->bhck', w_end, k)

    def step(carry, xs):
        c, n, m = carry
        be, mc, cc, nck = xs
        m_new = jnp.maximum(be + m, mc)
        dec = jnp.exp(be + m - m_new)
        sc = jnp.exp(mc - m_new)
        c_new = dec[..., None, None] * c + sc[..., None, None] * cc
        n_new = dec[..., None] * n + sc[..., None] * nck
        return (c_new, n_new, m_new), (c, n, m)

    init = (jnp.zeros((bsz, nh, dv, dk), jnp.float32),
            jnp.zeros((bsz, nh, dk), jnp.float32),
            jnp.zeros((bsz, nh), jnp.float32))
    xs = tuple(jnp.moveaxis(t, 2, 0) for t in (b_end, m_chunk, c_chunk, n_chunk))
    _, (c_prev, n_prev, m_prev) = lax.scan(step, init, xs)
    c_prev = jnp.moveaxis(c_prev, 0, 2)
    n_prev = jnp.moveaxis(n_prev, 0, 2)
    m_prev = jnp.moveaxis(m_prev, 0, 2)

    causal = jnp.tril(jnp.ones((cl, cl), dtype=bool))
    log_d = jnp.where(causal, b[..., :, None] - b[..., None, :] + ig[..., None, :], -jnp.inf)
    log_inter = b + m_prev[..., None]
    m_row = jnp.maximum(log_inter, jnp.max(log_d, axis=-1))
    p = jnp.exp(log_d - m_row[..., None]) * jnp.einsum('bhcsk,bhcrk->bhcsr', q, k)
    w_inter = jnp.exp(log_inter - m_row)
    num = (jnp.einsum('bhcsr,bhcrv->bhcsv', p, v)
           + w_inter[..., None] * jnp.einsum('bhcvk,bhcsk->bhcsv', c_prev, q))
    den = jnp.sum(p, axis=-1) + w_inter * jnp.einsum('bhck,bhcsk->bhcs', n_prev, q)
    h = num / jnp.maximum(jnp.abs(den), jnp.exp(-m_row))[..., None]
    return jnp.moveaxis(h, 1, 3).reshape(bsz, seq, nh, dv)


def _token_mix(h, w_in, rw_mu, rw_w0, rw_w2, rw_a0, rw_a2, rw_g2, rw_k_k, rw_k_a, rw_r_k,
               rw_gn_w, rw_gn_b, ml_conv_w, ml_conv_b, ml_i_b, ml_f_b, ml_norm,
               w_up_rw, w_up_ml, w_out):
    bsz, seq, _ = h.shape
    f32 = jnp.float32
    z = h @ w_in
    z_rw, z_ml, z_gate = _split(z, [RW_COLS, ML_COLS, 2 * D_MODEL])

    z_rw = z_rw + (_shift(z_rw, 1) - z_rw) * rw_mu
    r, k, v, w_d, a_d, g_d = _split(z_rw, [RW_WIDTH] * 3 + [DECAY_LORA, AAA_LORA, GATE_LORA])
    w_raw = (rw_w0 + jnp.tanh(w_d) @ rw_w2).astype(f32)
    decay = jnp.exp(-jnp.exp(-jax.nn.softplus(-w_raw) - 0.5))
    a = jax.nn.sigmoid((rw_a0 + a_d @ rw_a2).astype(f32))
    g = (jax.nn.sigmoid(g_d) @ rw_g2).astype(f32)

    def heads(t):
        return t.astype(f32).reshape(bsz, seq, RW_HEADS, RW_HEAD_DIM)

    kf = k.astype(f32)
    kk = heads(kf * rw_k_k)
    kk = kk / jnp.maximum(jnp.linalg.norm(kk, axis=-1, keepdims=True), 1e-12)
    k_mod = heads(kf * (1.0 + (a - 1.0) * rw_k_a))
    r_h, v_h, a_h, w_h = heads(r), heads(v), heads(a), heads(decay)
    y = _rwkv7_scan(r_h, w_h, k_mod, v_h, kk, kk * a_h)
    mu = jnp.mean(y, axis=-1, keepdims=True)
    var = jnp.mean(jnp.square(y - mu), axis=-1, keepdims=True)
    gn_w = rw_gn_w.astype(f32).reshape(RW_HEADS, RW_HEAD_DIM)
    gn_b = rw_gn_b.astype(f32).reshape(RW_HEADS, RW_HEAD_DIM)
    y = (y - mu) * lax.rsqrt(var + GN_EPS) * gn_w + gn_b
    y = y + jnp.sum(r_h * k_mod * rw_r_k, axis=-1, keepdims=True) * v_h
    y_rw = y.reshape(bsz, seq, RW_WIDTH) * g

    q2, k2, v2, o2, i_pre, f_pre = _split(z_ml, [ML_WIDTH] * 4 + [ML_HEADS] * 2)
    qk = jax.nn.silu(_causal_conv(jnp.concatenate([q2, k2], axis=-1), ml_conv_w, ml_conv_b))
    q2, k2 = _split(qk, [ML_WIDTH, ML_WIDTH])

    def mh(t):
        return t.reshape(bsz, seq, ML_HEADS, ML_HEAD_DIM)

    hm = _mlstm_chunkwise(mh(q2), mh(k2), mh(v2), i_pre + ml_i_b, f_pre + ml_f_b)
    mu = jnp.mean(hm, axis=-1, keepdims=True)
    var = jnp.mean(jnp.square(hm - mu), axis=-1, keepdims=True)
    hm = ((hm - mu) * lax.rsqrt(var + ML_NORM_EPS)).reshape(bsz, seq, ML_WIDTH)
    y_ml = hm * ml_norm.astype(f32) * jax.nn.sigmoid(o2.astype(f32))

    g_a, g_b = _split(z_gate, [D_MODEL, D_MODEL])
    dt = h.dtype
    merged = (jax.nn.sigmoid(g_a) * (y_rw.astype(dt) @ w_up_rw)
              + jax.nn.sigmoid(g_b) * (y_ml.astype(dt) @ w_up_ml))
    return merged @ w_out


def _memory_attention(hq, hm, wq, wkv, wo):
    bsz, seq, _ = hq.shape
    mlen = hm.shape[1]
    q = (hq @ wq).reshape(bsz, seq, XA_HEADS, XA_HEAD_DIM)
    k, v = _split(hm @ wkv, [D_MODEL, D_MODEL])
    k = k.reshape(bsz, mlen, XA_HEADS, XA_HEAD_DIM)
    v = v.reshape(bsz, mlen, XA_HEADS, XA_HEAD_DIM)
    s = jnp.einsum('bshd,bmhd->bhsm', q, k).astype(jnp.float32) * (XA_HEAD_DIM ** -0.5)
    p = jax.nn.softmax(s, axis=-1).astype(hq.dtype)
    o = jnp.einsum('bhsm,bmhd->bshd', p, v).reshape(bsz, seq, D_MODEL)
    return o @ wo


def setup_inputs(seed: int = 0) -> dict:
    key = jax.random.key(seed)
    ks = iter(jax.random.split(key, 64))
    d = D_MODEL

    def nrm(shape, scale):
        return jax.random.normal(next(ks), (DEPTH,) + shape, jnp.float32) * scale

    def gain(shape):
        return 1.0 + 0.05 * jax.random.normal(next(ks), (DEPTH,) + shape, jnp.float32)

    def unif(shape, lo, hi):
        return jax.random.uniform(next(ks), (DEPTH,) + shape, jnp.float32, lo, hi)

    return {
        "x": jax.random.normal(next(ks), (BATCH, SEQ, d), jnp.float32),
        "mem": jax.random.normal(next(ks), (BATCH, MEM_LEN, d), jnp.float32),
        "ffn1_norm": gain((d,)),
        "ffn1_w_gate": nrm((d, D_FF), d ** -0.5),
        "ffn1_w_up": nrm((d, D_FF), d ** -0.5),
        "ffn1_w_down": nrm((D_FF, d), D_FF ** -0.5),
        "mix_norm": gain((d,)),
        "w_in": nrm((d, IN_COLS), d ** -0.5),
        "rw_mu": unif((RW_COLS,), 0.0, 1.0),
        "rw_w0": unif((RW_WIDTH,), -4.0, 1.0),
        "rw_w2": nrm((DECAY_LORA, RW_WIDTH), 0.5 * DECAY_LORA ** -0.5),
        "rw_a0": nrm((RW_WIDTH,), 0.1),
        "rw_a2": nrm((AAA_LORA, RW_WIDTH), AAA_LORA ** -0.5),
        "rw_g2": nrm((GATE_LORA, RW_WIDTH), GATE_LORA ** -0.5),
        "rw_k_k": 0.85 + nrm((RW_WIDTH,), 0.05),
        "rw_k_a": gain((RW_WIDTH,)),
        "rw_r_k": nrm((RW_HEADS, RW_HEAD_DIM), 0.1),
        "rw_gn_w": gain((RW_WIDTH,)),
        "rw_gn_b": nrm((RW_WIDTH,), 0.01),
        "ml_conv_w": nrm((ML_CONV, 2 * ML_WIDTH), ML_CONV ** -0.5),
        "ml_conv_b": nrm((2 * ML_WIDTH,), 0.01),
        "ml_i_b": unif((ML_HEADS,), -2.0, 0.0),
        "ml_f_b": unif((ML_HEADS,), 3.0, 6.0),
        "ml_norm": gain((ML_WIDTH,)),
        "w_up_rw": nrm((RW_WIDTH, d), RW_WIDTH ** -0.5),
        "w_up_ml": nrm((ML_WIDTH, d), ML_WIDTH ** -0.5),
        "w_out": nrm((d, d), d ** -0.5),
        "xa_norm": gain((d,)),
        "mem_norm": gain((d,)),
        "xa_wq": nrm((d, d), d ** -0.5),
        "xa_wkv": nrm((d, 2 * d), d ** -0.5),
        "xa_wo": nrm((d, d), d ** -0.5),
        "ffn2_norm": gain((d,)),
        "ffn2_w_gate": nrm((d, D_FF), d ** -0.5),
        "ffn2_w_up": nrm((d, D_FF), d ** -0.5),
        "ffn2_w_down": nrm((D_FF, d), D_FF ** -0.5),
        "final_norm": 1.0 + 0.05 * jax.random.normal(next(ks), (d,), jnp.float32),
    }


def reference(x, mem, ffn1_norm, ffn1_w_gate, ffn1_w_up, ffn1_w_down, mix_norm, w_in,
              rw_mu, rw_w0, rw_w2, rw_a0, rw_a2, rw_g2, rw_k_k, rw_k_a, rw_r_k, rw_gn_w, rw_gn_b,
              ml_conv_w, ml_conv_b, ml_i_b, ml_f_b, ml_norm, w_up_rw, w_up_ml, w_out,
              xa_norm, mem_norm, xa_wq, xa_wkv, xa_wo,
              ffn2_norm, ffn2_w_gate, ffn2_w_up, ffn2_w_down, final_norm):
    for l in range(DEPTH):
        x = x + FFN_HALF * _swiglu(_rmsnorm(x, ffn1_norm[l]), ffn1_w_gate[l], ffn1_w_up[l], ffn1_w_down[l])
        x = x + _token_mix(_rmsnorm(x, mix_norm[l]), w_in[l], rw_mu[l], rw_w0[l], rw_w2[l], rw_a0[l],
                           rw_a2[l], rw_g2[l], rw_k_k[l], rw_k_a[l], rw_r_k[l], rw_gn_w[l], rw_gn_b[l],
                           ml_conv_w[l], ml_conv_b[l], ml_i_b[l], ml_f_b[l], ml_norm[l],
                           w_up_rw[l], w_up_ml[l], w_out[l])
        x = x + _memory_attention(_rmsnorm(x, xa_norm[l]), _rmsnorm(mem, mem_norm[l]),
                                  xa_wq[l], xa_wkv[l], xa_wo[l])
        x = x + FFN_HALF * _swiglu(_rmsnorm(x, ffn2_norm[l]), ffn2_w_gate[l], ffn2_w_up[l], ffn2_w_down[l])
    return _rmsnorm(x, final_norm)
```

```python
import functools
import math

import jax
import jax.numpy as jnp
import numpy as np
from jax import lax
from jax.experimental import pallas as pl
from jax.experimental.pallas import tpu as pltpu

F32 = jnp.float32
BF16 = jnp.bfloat16

D_MODEL = 1024
D_FF = 2816
FFN_HALF = 0.5
NORM_EPS = 1e-5
MEM_LEN = 256

RW_HEADS = 8
RW_HEAD_DIM = 64
RW_WIDTH = 512
DECAY_LORA = 64
AAA_LORA = 64
GATE_LORA = 128
RW_COLS = 3 * RW_WIDTH + DECAY_LORA + AAA_LORA + GATE_LORA
GN_EPS = 64e-5
RW_PAIRS = RW_HEADS // 2
PAIR = 2 * RW_HEAD_DIM

ML_HEADS = 4
ML_HEAD_DIM = 128
ML_WIDTH = 512
ML_CONV = 4
ML_NORM_EPS = 1e-6
ML_QKVO = 4 * ML_WIDTH
IF_PAD = 128

XA_HEADS = 4
XA_HEAD_DIM = 256

CHUNK = 64
LANES = 128
SUBLANES = 8
NEG_BIG = -1e30

VMEM_LIMIT = 56 * 1024 * 1024


def _bdot(a, b):
    return jnp.dot(a.astype(BF16), b.astype(BF16), preferred_element_type=F32)


def _bdot_nt(a, b):
    return lax.dot_general(a.astype(BF16), b.astype(BF16), (((1,), (1,)), ((), ())),
                           preferred_element_type=F32)


def _bdot_tn(a, b):
    return lax.dot_general(a.astype(BF16), b.astype(BF16), (((0,), (0,)), ((), ())),
                           preferred_element_type=F32)


def _split3(x):
    hi = x.astype(BF16)
    r1 = x - hi.astype(F32)
    mid = r1.astype(BF16)
    lo = (r1 - mid.astype(F32)).astype(BF16)
    return hi, mid, lo


def _dot_exact_lhs(c_bf16, x):
    hi, mid, lo = _split3(x)
    f = lambda p: jnp.dot(c_bf16, p, preferred_element_type=F32)
    return f(hi) + f(mid) + f(lo)


def _dot_exact_rhs(x, c_bf16):
    hi, mid, lo = _split3(x)
    f = lambda p: jnp.dot(p, c_bf16, preferred_element_type=F32)
    return f(hi) + f(mid) + f(lo)


def _dot_f32(a, b):
    a_hi = a.astype(BF16)
    a_lo = (a - a_hi.astype(F32)).astype(BF16)
    b_hi = b.astype(BF16)
    b_lo = (b - b_hi.astype(F32)).astype(BF16)
    f = lambda p, q: jnp.dot(p, q, preferred_element_type=F32)
    return f(a_hi, b_hi) + f(a_hi, b_lo) + f(a_lo, b_hi)


def _sigmoid(x):
    return 1.0 / (1.0 + jnp.exp(-x))


def _silu(x):
    return x * _sigmoid(x)


def _rmsnorm(x, g):
    ms = jnp.mean(x * x, axis=-1, keepdims=True)
    return x * lax.rsqrt(ms + NORM_EPS) * g


def _shift_rows(u, carry8, j):
    rolled = pltpu.roll(u, j, axis=0)
    rolled_c = pltpu.roll(carry8, j, axis=0)
    row = lax.broadcasted_iota(jnp.int32, (SUBLANES, u.shape[1]), 0)
    top = jnp.where(row < j, rolled_c, rolled[:SUBLANES])
    return jnp.concatenate([top, rolled[SUBLANES:]], axis=0)


def _const_spec(shape):
    nd = len(shape)
    return pl.BlockSpec(shape, lambda *_: (0,) * nd, pipeline_mode=pl.Buffered(1))


def _params(sem):
    return pltpu.CompilerParams(dimension_semantics=sem, vmem_limit_bytes=VMEM_LIMIT)


FF_CHUNK = 256
FFN_TM = 512


def _ffn_kernel(x_ref, g_ref, wg_ref, wu_ref, wd_ref, fg_ref, o_ref, *, final):
    x = x_ref[...]
    h = _rmsnorm(x, g_ref[...]).astype(BF16)
    acc = jnp.zeros(x.shape, F32)
    for c in range(D_FF // FF_CHUNK):
        sl = slice(c * FF_CHUNK, (c + 1) * FF_CHUNK)
        gate = jnp.dot(h, wg_ref[:, sl], preferred_element_type=F32)
        up = jnp.dot(h, wu_ref[:, sl], preferred_element_type=F32)
        act = (_silu(gate) * up).astype(BF16)
        acc = acc + jnp.dot(act, wd_ref[sl, :], preferred_element_type=F32)
    y = x + FFN_HALF * acc
    if final:
        y = _rmsnorm(y, fg_ref[...])
    o_ref[...] = y


def _ffn_call(x, g, wg, wu, wd, fg, final):
    t, d = x.shape
    tm = min(FFN_TM, t)
    return pl.pallas_call(
        functools.partial(_ffn_kernel, final=final),
        grid=(t // tm,),
        in_specs=[pl.BlockSpec((tm, d), lambda i: (i, 0)),
                  _const_spec((1, d)), _const_spec((d, D_FF)), _const_spec((d, D_FF)),
                  _const_spec((D_FF, d)), _const_spec((1, d))],
        out_specs=pl.BlockSpec((tm, d), lambda i: (i, 0)),
        out_shape=jax.ShapeDtypeStruct((t, d), F32),
        compiler_params=_params(("parallel",)),
        name="ffn_final" if final else "ffn",
    )(x, g, wg, wu, wd, fg)


INPROJ_TM = 256


def _inproj_kernel(x_ref, g_ref, wrw_ref, wml_ref, wif_ref, wgt_ref, mu_ref,
                   zrw_ref, zml_ref, zif_ref, gate_ref, carry_ref):
    @pl.when(pl.program_id(1) == 0)
    def _():
        carry_ref[...] = jnp.zeros(carry_ref.shape, F32)

    h = _rmsnorm(x_ref[...], g_ref[...]).astype(BF16)
    z = jnp.dot(h, wrw_ref[...], preferred_element_type=F32)
    prev = _shift_rows(z, carry_ref[...], 1)
    carry_ref[...] = z[z.shape[0] - SUBLANES:]
    zrw_ref[...] = z + (prev - z) * mu_ref[...]
    zml_ref[...] = jnp.dot(h, wml_ref[...], preferred_element_type=F32)
    zif_ref[...] = jnp.dot(h, wif_ref[...], preferred_element_type=F32)
    gate_ref[...] = _sigmoid(jnp.dot(h, wgt_ref[...], preferred_element_type=F32))


def _inproj_call(x, g, wrw, wml, wif, wgt, mu, bsz, seq):
    t, d = x.shape
    tm = min(INPROJ_TM, seq)
    nt = seq // tm
    row = lambda b, i: (b * nt + i, 0)
    return pl.pallas_call(
        _inproj_kernel,
        grid=(bsz, nt),
        in_specs=[pl.BlockSpec((tm, d), row),
                  _const_spec((1, d)), _const_spec((d, RW_COLS)), _const_spec((d, ML_QKVO)),
                  _const_spec((d, IF_PAD)), _const_spec((d, 2 * d)), _const_spec((1, RW_COLS))],
        out_specs=[pl.BlockSpec((tm, RW_COLS), row), pl.BlockSpec((tm, ML_QKVO), row),
                   pl.BlockSpec((tm, IF_PAD), row), pl.BlockSpec((tm, 2 * d), row)],
        out_shape=[jax.ShapeDtypeStruct((t, RW_COLS), F32), jax.ShapeDtypeStruct((t, ML_QKVO), F32),
                   jax.ShapeDtypeStruct((t, IF_PAD), F32), jax.ShapeDtypeStruct((t, 2 * d), F32)],
        scratch_shapes=[pltpu.VMEM((SUBLANES, RW_COLS), F32)],
        compiler_params=_params(("parallel", "arbitrary")),
        name="inproj",
    )(x, g, wrw, wml, wif, wgt, mu)


RW1_CHUNKS = 4
INV_LEVELS = 6


def _rw_masks():
    n = 2 * CHUNK
    t = np.arange(n)[:, None]
    s = np.arange(n)[None, :]
    same = (t // CHUNK) == (s // CHUNK)
    strict = same & (s < t)
    incl = same & (s <= t)
    levels = []
    for k in range(INV_LEVELS):
        size = 1 << k
        levels.append(((t // (2 * size)) == (s // (2 * size))) & ((t // size) % 2 == 1) & ((s // size) % 2 == 0))
    m = np.stack([strict, incl, np.eye(n, dtype=bool)] + levels).astype(np.float32)
    tri = (np.arange(CHUNK)[:, None] >= np.arange(CHUNK)[None, :]).astype(np.float32)
    return m, tri


def _group_ones(width, group):
    i = np.arange(width)
    return (i[:, None] // group == i[None, :] // group).astype(np.float32)


def _rw1_kernel(z_ref, w2_ref, a2_ref, g2_ref, vec_ref, gsum_ref, tri_ref, mask_ref,
                g_ref, yi_ref, bonus_ref, gate_ref, m_ref, c_ref):
    z = z_ref[...]
    rows = z.shape[0]
    r = z[:, 0:RW_WIDTH]
    k = z[:, RW_WIDTH:2 * RW_WIDTH]
    v = z[:, 2 * RW_WIDTH:3 * RW_WIDTH]
    lora_in = z[:, 3 * RW_WIDTH:3 * RW_WIDTH + DECAY_LORA + AAA_LORA]
    g_d = z[:, 3 * RW_WIDTH + DECAY_LORA + AAA_LORA:]
    w0 = vec_ref[0:1, :]
    a0 = vec_ref[1:2, :]
    k_k = vec_ref[2:3, :]
    k_a = vec_ref[3:4, :]
    r_k = vec_ref[4:5, :]

    w_raw = w0 + _bdot(jnp.tanh(lora_in), w2_ref[...])
    logw = -math.exp(-0.5) * _sigmoid(w_raw)
    a = _sigmoid(a0 + _bdot(lora_in, a2_ref[...]))
    gate_ref[...] = _bdot(_sigmoid(g_d), g2_ref[...])

    gsum = gsum_ref[...]
    kk = k * k_k
    kk = kk / jnp.maximum(jnp.sqrt(_dot_exact_rhs(kk * kk, gsum)), 1e-12)
    k_mod = k * (1.0 + (a - 1.0) * k_a)
    b = kk * a
    bonus_ref[...] = _dot_exact_rhs(r * k_mod * r_k, gsum) * v

    lane = lax.broadcasted_iota(jnp.int32, (CHUNK, PAIR), 1)
    head0 = lane < RW_HEAD_DIM
    strict = mask_ref[0]
    incl = mask_ref[1]
    eye = mask_ref[2]
    tri = tri_ref[...]

    def stack(x):
        return jnp.concatenate([jnp.where(head0, x, 0.0), jnp.where(head0, 0.0, x)], axis=0)

    for c in range(rows // CHUNK):
        rs = slice(c * CHUNK, (c + 1) * CHUNK)
        lw = logw[rs]
        cum = _dot_exact_lhs(tri, lw)
        cum_last = cum[CHUNK - 1:CHUNK]
        w_incl = jnp.exp(cum)
        w_inv = jnp.exp(-cum)
        w_prev = jnp.exp(cum - lw)
        w_end = jnp.exp(cum_last - cum)
        w_last = jnp.exp(cum_last)
        at_f = -kk[rs] * w_prev
        rt_f = r[rs] * w_incl
        bt_f = b[rs] * w_inv
        kt_f = k_mod[rs] * w_inv
        be_f = b[rs] * w_end
        ke_f = k_mod[rs] * w_end
        v_f = v[rs]
        for p in range(RW_PAIRS):
            ls = slice(p * PAIR, (p + 1) * PAIR)
            at, rt, bt, kt = stack(at_f[:, ls]), stack(rt_f[:, ls]), stack(bt_f[:, ls]), stack(kt_f[:, ls])
            be, ke, v2 = stack(be_f[:, ls]), stack(ke_f[:, ls]), stack(v_f[:, ls])
            prod = _bdot_nt(jnp.concatenate([at, rt], axis=0), jnp.concatenate([bt, kt], axis=0))
            n_ab = prod[:PAIR, :PAIR] * strict
            a_ak = prod[:PAIR, PAIR:] * strict
            a_rb = prod[PAIR:, :PAIR] * incl
            a_rk = prod[PAIR:, PAIR:] * incl
            t_inv = eye + n_ab * mask_ref[3]
            for lvl in range(1, INV_LEVELS):
                t_inv = t_inv + _bdot(_bdot(t_inv, n_ab * mask_ref[3 + lvl]), t_inv)
            akv = _bdot(a_ak, v2)
            pq = _bdot(t_inv, jnp.concatenate([at, akv], axis=1))
            gy = _bdot(a_rb, pq)
            g2 = rt + gy[:, :PAIR]
            yi2 = gy[:, PAIR:] + _bdot(a_rk, v2)
            g_ref[rs, ls] = g2[:CHUNK] + g2[CHUNK:]
            yi_ref[rs, ls] = yi2[:CHUNK] + yi2[CHUNK:]
            mc = _bdot_tn(be, pq)
            m_ref[c, p] = eye * w_last[:, ls] + mc[:, :PAIR]
            c_ref[c, p] = mc[:, PAIR:] + _bdot_tn(ke, v2)


def _rw1_call(zrw, w2p, a2p, g2, vec, gsum, tri, masks):
    t = zrw.shape[0]
    rows = RW1_CHUNKS * CHUNK
    nchunks = t // CHUNK
    tok = lambda i: (i, 0)
    blk = lambda i: (i, 0, 0, 0)
    tok_spec = pl.BlockSpec((rows, RW_WIDTH), tok)
    mc_spec = pl.BlockSpec((RW1_CHUNKS, RW_PAIRS, PAIR, PAIR), blk)
    tok_shape = jax.ShapeDtypeStruct((t, RW_WIDTH), F32)
    mc_shape = jax.ShapeDtypeStruct((nchunks, RW_PAIRS, PAIR, PAIR), F32)
    return pl.pallas_call(
        _rw1_kernel,
        grid=(t // rows,),
        in_specs=[pl.BlockSpec((rows, RW_COLS), tok),
                  _const_spec(w2p.shape), _const_spec(a2p.shape), _const_spec(g2.shape),
                  _const_spec(vec.shape), _const_spec(gsum.shape), _const_spec(tri.shape),
                  _const_spec(masks.shape)],
        out_specs=[tok_spec, tok_spec, tok_spec, tok_spec, mc_spec, mc_spec],
        out_shape=[tok_shape, tok_shape, tok_shape, tok_shape, mc_shape, mc_shape],
        compiler_params=_params(("parallel",)),
        name="rwkv_chunk_local",
    )(zrw, w2p, a2p, g2, vec, gsum, tri, masks)


RW2_CHUNKS = 8


def _rw2_kernel(g_ref, yi_ref, bonus_ref, gate_ref, m_ref, c_ref, gmean_ref, gn_ref, y_ref, h_ref):
    @pl.when(pl.program_id(1) == 0)
    def _():
        h_ref[...] = jnp.zeros(h_ref.shape, F32)

    ys = []
    for c in range(RW2_CHUNKS):
        rs = slice(c * CHUNK, (c + 1) * CHUNK)
        cols = []
        for p in range(RW_PAIRS):
            ls = slice(p * PAIR, (p + 1) * PAIR)
            h = h_ref[p]
            cols.append(_dot_f32(g_ref[rs, ls], h) + yi_ref[rs, ls])
            h_ref[p] = _dot_f32(m_ref[c, p], h) + c_ref[c, p]
        ys.append(jnp.concatenate(cols, axis=1))
    y = jnp.concatenate(ys, axis=0)
    gmean = gmean_ref[...]
    mu = _dot_exact_rhs(y, gmean)
    d = y - mu
    var = _dot_exact_rhs(d * d, gmean)
    y = d * lax.rsqrt(var + GN_EPS) * gn_ref[0:1, :] + gn_ref[1:2, :]
    y_ref[...] = ((y + bonus_ref[...]) * gate_ref[...]).astype(BF16)


def _rw2_call(g, yi, bonus, gate, m, c, gmean, gn, bsz, seq):
    t = g.shape[0]
    rows = RW2_CHUNKS * CHUNK
    nt = seq // rows
    tok = lambda b, i: (b * nt + i, 0)
    blk = lambda b, i: (b * nt + i, 0, 0, 0)
    tok_spec = pl.BlockSpec((rows, RW_WIDTH), tok)
    mc_spec = pl.BlockSpec((RW2_CHUNKS, RW_PAIRS, PAIR, PAIR), blk)
    return pl.pallas_call(
        _rw2_kernel,
        grid=(bsz, nt),
        in_specs=[tok_spec, tok_spec, tok_spec, tok_spec, mc_spec, mc_spec,
                  _const_spec(gmean.shape), _const_spec(gn.shape)],
        out_specs=tok_spec,
        out_shape=jax.ShapeDtypeStruct((t, RW_WIDTH), BF16),
        scratch_shapes=[pltpu.VMEM((RW_PAIRS, PAIR, PAIR), F32)],
        compiler_params=_params(("parallel", "arbitrary")),
        name="rwkv_scan",
    )(g, yi, bonus, gate, m, c, gmean, gn)


ML_CHUNKS = 4


def _log_sigmoid(x):
    return jnp.minimum(x, 0.0) - jnp.log(1.0 + jnp.exp(-jnp.abs(x)))


def _ml_kernel(z_ref, zif_ref, convw_ref, convb_ref, ifb_ref, norm_ref, tri_ref,
               y_ref, carry_ref, state_ref, m_ref):
    @pl.when(pl.program_id(1) == 0)
    def _():
        carry_ref[...] = jnp.zeros(carry_ref.shape, F32)
        state_ref[...] = jnp.zeros(state_ref.shape, F32)
        m_ref[...] = jnp.zeros(m_ref.shape, F32)

    rows = z_ref.shape[0]
    u = z_ref[:, 0:2 * ML_WIDTH]
    carry = carry_ref[...]
    conv = convb_ref[...] + convw_ref[0:1, :] * u
    for j in range(1, ML_CONV):
        conv = conv + convw_ref[j:j + 1, :] * _shift_rows(u, carry, j)
    carry_ref[...] = u[rows - SUBLANES:]
    qk = _silu(conv)
    q_all = qk[:, 0:ML_WIDTH] * (ML_HEAD_DIM ** -0.5)
    k_all = qk[:, ML_WIDTH:]
    v_all = z_ref[:, 2 * ML_WIDTH:3 * ML_WIDTH]
    o_all = z_ref[:, 3 * ML_WIDTH:4 * ML_WIDTH]

    pre = zif_ref[...] + ifb_ref[...]
    logf = _log_sigmoid(pre)
    tri = tri_ref[...]
    lane = lax.broadcasted_iota(jnp.int32, (CHUNK, LANES), 1)
    ones_col = jnp.where(lane == 0, 1.0, 0.0)
    tq = lax.broadcasted_iota(jnp.int32, (CHUNK, CHUNK), 0)
    tk = lax.broadcasted_iota(jnp.int32, (CHUNK, CHUNK), 1)
    causal = tk <= tq

    for c in range(rows // CHUNK):
        rs = slice(c * CHUNK, (c + 1) * CHUNK)
        bcum = _dot_exact_lhs(tri, logf[rs])
        ig = pre[rs]
        bcum_t = bcum.T
        ig_t = ig.T
        outs = []
        for h in range(ML_HEADS):
            ls = slice(h * ML_HEAD_DIM, (h + 1) * ML_HEAD_DIM)
            b_col = bcum[:, ML_HEADS + h:ML_HEADS + h + 1]
            i_col = ig[:, h:h + 1]
            b_row = bcum_t[ML_HEADS + h:ML_HEADS + h + 1, :]
            i_row = ig_t[h:h + 1, :]
            b_end = b_col[CHUNK - 1:CHUNK]
            m_prev = m_ref[h][:, 0:1]
            q = q_all[rs, ls]
            k = k_all[rs, ls]
            v_aug = jnp.concatenate([v_all[rs, ls], ones_col], axis=1)
            state = state_ref[h]

            log_d = jnp.where(causal, b_col - b_row + i_row, NEG_BIG)
            log_inter = b_col + m_prev
            m_row = jnp.maximum(log_inter, jnp.max(log_d, axis=-1, keepdims=True))
            p = jnp.exp(log_d - m_row) * _bdot_nt(q, k)
            w_inter = jnp.exp(log_inter - m_row)
            both = _bdot(p, v_aug) + w_inter * _bdot(q, state)
            num = both[:, :ML_HEAD_DIM]
            den = both[:, ML_HEAD_DIM:ML_HEAD_DIM + 1]
            hh = num / jnp.maximum(jnp.abs(den), jnp.exp(-m_row))

            log_w_end = b_end - b_col + i_col
            m_chunk = jnp.max(log_w_end, axis=0, keepdims=True)
            w_end = jnp.exp(log_w_end - m_chunk)
            m_new = jnp.maximum(b_end + m_prev, m_chunk)
            dec = jnp.exp(b_end + m_prev - m_new)
            sc = jnp.exp(m_chunk - m_new)
            state_ref[h] = dec * state + sc * _bdot_tn(k, w_end * v_aug)
            m_ref[h] = jnp.broadcast_to(m_new, (1, LANES))

            mu = jnp.mean(hh, axis=-1, keepdims=True)
            d = hh - mu
            var = jnp.mean(d * d, axis=-1, keepdims=True)
            hn = d * lax.rsqrt(var + ML_NORM_EPS)
            outs.append(hn * norm_ref[:, ls] * _sigmoid(o_all[rs, ls]))
        y_ref[rs, :] = jnp.concatenate(outs, axis=1).astype(BF16)


def _ml_call(zml, zif, convw, convb, ifb, norm, tri, bsz, seq):
    t = zml.shape[0]
    rows = ML_CHUNKS * CHUNK
    nt = seq // rows
    tok = lambda b, i: (b * nt + i, 0)
    return pl.pallas_call(
        _ml_kernel,
        grid=(bsz, nt),
        in_specs=[pl.BlockSpec((rows, ML_QKVO), tok), pl.BlockSpec((rows, IF_PAD), tok),
                  _const_spec(convw.shape), _const_spec(convb.shape), _const_spec(ifb.shape),
                  _const_spec(norm.shape), _const_spec(tri.shape)],
        out_specs=pl.BlockSpec((rows, ML_WIDTH), tok),
        out_shape=jax.ShapeDtypeStruct((t, ML_WIDTH), BF16),
        scratch_shapes=[pltpu.VMEM((SUBLANES, 2 * ML_WIDTH), F32),
                        pltpu.VMEM((ML_HEADS, ML_HEAD_DIM, 2 * ML_HEAD_DIM), F32),
                        pltpu.VMEM((ML_HEADS, 1, LANES), F32)],
        compiler_params=_params(("parallel", "arbitrary")),
        name="mlstm",
    )(zml, zif, convw, convb, ifb, norm, tri)


MERGE_TM = 512


def _merge_kernel(x_ref, yrw_ref, yml_ref, gate_ref, wur_ref, wum_ref, wo_ref, o_ref):
    d = x_ref.shape[1]
    up_rw = jnp.dot(yrw_ref[...], wur_ref[...], preferred_element_type=F32)
    up_ml = jnp.dot(yml_ref[...], wum_ref[...], preferred_element_type=F32)
    merged = gate_ref[:, 0:d] * up_rw + gate_ref[:, d:] * up_ml
    o_ref[...] = x_ref[...] + _bdot(merged, wo_ref[...])


def _merge_call(x, yrw, yml, gates, wur, wum, wo):
    t, d = x.shape
    tm = min(MERGE_TM, t)
    row = lambda i: (i, 0)
    return pl.pallas_call(
        _merge_kernel,
        grid=(t // tm,),
        in_specs=[pl.BlockSpec((tm, d), row), pl.BlockSpec((tm, RW_WIDTH), row),
                  pl.BlockSpec((tm, ML_WIDTH), row), pl.BlockSpec((tm, 2 * d), row),
                  _const_spec(wur.shape), _const_spec(wum.shape), _const_spec(wo.shape)],
        out_specs=pl.BlockSpec((tm, d), row),
        out_shape=jax.ShapeDtypeStruct((t, d), F32),
        compiler_params=_params(("parallel",)),
        name="merge",
    )(x, yrw, yml, gates, wur, wum, wo)


def _memkv_kernel(mem_ref, g_ref, wkv_ref, k_ref, v_ref):
    d = mem_ref.shape[1]
    hm = _rmsnorm(mem_ref[...], g_ref[...])
    kv = _bdot(hm, wkv_ref[...])
    k_ref[...] = kv[:, 0:d].astype(BF16)
    v_ref[...] = kv[:, d:].astype(BF16)


def _memkv_call(mem2d, g, wkv):
    n, d = mem2d.shape
    row = lambda i: (i, 0)
    return pl.pallas_call(
        _memkv_kernel,
        grid=(n // MEM_LEN,),
        in_specs=[pl.BlockSpec((MEM_LEN, d), row), _const_spec(g.shape), _const_spec(wkv.shape)],
        out_specs=[pl.BlockSpec((MEM_LEN, d), row), pl.BlockSpec((MEM_LEN, d), row)],
        out_shape=[jax.ShapeDtypeStruct((n, d), BF16), jax.ShapeDtypeStruct((n, d), BF16)],
        compiler_params=_params(("parallel",)),
        name="mem_kv",
    )(mem2d, g, wkv)


XA_TM = 512


def _xattn_kernel(x_ref, g_ref, wq_ref, k_ref, v_ref, wo_ref, o_ref):
    x = x_ref[...]
    hq = _rmsnorm(x, g_ref[...])
    q = _bdot(hq, wq_ref[...]) * (XA_HEAD_DIM ** -0.5)
    outs = []
    for h in range(XA_HEADS):
        ls = slice(h * XA_HEAD_DIM, (h + 1) * XA_HEAD_DIM)
        s = _bdot_nt(q[:, ls], k_ref[:, ls])
        s = s - jnp.max(s, axis=-1, keepdims=True)
        e = jnp.exp(s)
        p = e / jnp.sum(e, axis=-1, keepdims=True)
        outs.append(_bdot(p, v_ref[:, ls]))
    o = jnp.concatenate(outs, axis=1)
    o_ref[...] = x + _bdot(o, wo_ref[...])


def _xattn_call(x, g, wq, kmem, vmem, wo, bsz, seq):
    t, d = x.shape
    tm = min(XA_TM, seq)
    nt = seq // tm
    row = lambda b, i: (b * nt + i, 0)
    mem = lambda b, i: (b, 0)
    return pl.pallas_call(
        _xattn_kernel,
        grid=(bsz, nt),
        in_specs=[pl.BlockSpec((tm, d), row), _const_spec(g.shape), _const_spec(wq.shape),
                  pl.BlockSpec((MEM_LEN, d), mem), pl.BlockSpec((MEM_LEN, d), mem),
                  _const_spec(wo.shape)],
        out_specs=pl.BlockSpec((tm, d), row),
        out_shape=jax.ShapeDtypeStruct((t, d), F32),
        compiler_params=_params(("parallel", "parallel")),
        name="mem_xattn",
    )(x, g, wq, kmem, vmem, wo)


def _row(v):
    return v.reshape(1, -1).astype(F32)


def _layer(x, mem2d, bsz, seq, p):
    d = D_MODEL
    bf = lambda w: w.astype(BF16)
    masks_np, tri_np = _rw_masks()
    masks = jnp.asarray(masks_np)
    tri = jnp.asarray(tri_np, dtype=BF16)
    gsum = jnp.asarray(_group_ones(RW_WIDTH, RW_HEAD_DIM), dtype=BF16)
    gmean = jnp.asarray(_group_ones(RW_WIDTH, RW_HEAD_DIM) / RW_HEAD_DIM, dtype=BF16)

    x = _ffn_call(x, _row(p["ffn1_norm"]), bf(p["ffn1_w_gate"]), bf(p["ffn1_w_up"]), bf(p["ffn1_w_down"]),
                  _row(p["ffn1_norm"]), final=False)

    w_in = p["w_in"]
    c0 = RW_COLS
    c1 = c0 + ML_QKVO
    c2 = c1 + 2 * ML_HEADS
    w_if = jnp.pad(w_in[:, c1:c2], ((0, 0), (0, IF_PAD - 2 * ML_HEADS)))
    zrw, zml, zif, gates = _inproj_call(x, _row(p["mix_norm"]), bf(w_in[:, :c0]), bf(w_in[:, c0:c1]), bf(w_if),
                                        bf(w_in[:, c2:]), _row(p["rw_mu"]), bsz, seq)

    zpad = jnp.zeros((DECAY_LORA, RW_WIDTH), F32)
    w2p = bf(jnp.concatenate([p["rw_w2"], zpad], axis=0))
    a2p = bf(jnp.concatenate([zpad, p["rw_a2"]], axis=0))
    vec = jnp.stack([p["rw_w0"], p["rw_a0"], p["rw_k_k"], p["rw_k_a"], p["rw_r_k"].reshape(-1),
                     jnp.zeros_like(p["rw_w0"]), jnp.zeros_like(p["rw_w0"]), jnp.zeros_like(p["rw_w0"])]).astype(F32)
    g, yi, bonus, gate, m, c = _rw1_call(zrw, w2p, a2p, bf(p["rw_g2"]), vec, gsum, tri, masks)
    gn = jnp.concatenate([jnp.stack([p["rw_gn_w"], p["rw_gn_b"]]).astype(F32),
                          jnp.zeros((SUBLANES - 2, RW_WIDTH), F32)], axis=0)
    y_rw = _rw2_call(g, yi, bonus, gate, m, c, gmean, gn, bsz, seq)

    ifb = jnp.pad(jnp.concatenate([p["ml_i_b"], p["ml_f_b"]]).astype(F32), (0, IF_PAD - 2 * ML_HEADS)).reshape(1, -1)
    convw = jnp.concatenate([p["ml_conv_w"].astype(F32), jnp.zeros((SUBLANES - ML_CONV, 2 * ML_WIDTH), F32)], axis=0)
    y_ml = _ml_call(zml, zif, convw, _row(p["ml_conv_b"]), ifb, _row(p["ml_norm"]), tri, bsz, seq)

    x = _merge_call(x, y_rw, y_ml, gates, bf(p["w_up_rw"]), bf(p["w_up_ml"]), bf(p["w_out"]))

    kmem, vmem = _memkv_call(mem2d, _row(p["mem_norm"]), bf(p["xa_wkv"]))
    x = _xattn_call(x, _row(p["xa_norm"]), bf(p["xa_wq"]), kmem, vmem, bf(p["xa_wo"]), bsz, seq)
    return x


def kernel(x, mem, ffn1_norm, ffn1_w_gate, ffn1_w_up, ffn1_w_down, mix_norm, w_in, rw_mu, rw_w0, rw_w2, rw_a0,
           rw_a2, rw_g2, rw_k_k, rw_k_a, rw_r_k, rw_gn_w, rw_gn_b, ml_conv_w, ml_conv_b, ml_i_b, ml_f_b, ml_norm,
           w_up_rw, w_up_ml, w_out, xa_norm, mem_norm, xa_wq, xa_wkv, xa_wo, ffn2_norm, ffn2_w_gate, ffn2_w_up,
           ffn2_w_down, final_norm):
    bsz, seq, d = x.shape
    depth = ffn1_norm.shape[0]
    stacked = dict(ffn1_norm=ffn1_norm, ffn1_w_gate=ffn1_w_gate, ffn1_w_up=ffn1_w_up, ffn1_w_down=ffn1_w_down,
                   mix_norm=mix_norm, w_in=w_in, rw_mu=rw_mu, rw_w0=rw_w0, rw_w2=rw_w2, rw_a0=rw_a0, rw_a2=rw_a2,
                   rw_g2=rw_g2, rw_k_k=rw_k_k, rw_k_a=rw_k_a, rw_r_k=rw_r_k, rw_gn_w=rw_gn_w, rw_gn_b=rw_gn_b,
                   ml_conv_w=ml_conv_w, ml_conv_b=ml_conv_b, ml_i_b=ml_i_b, ml_f_b=ml_f_b, ml_norm=ml_norm,
                   w_up_rw=w_up_rw, w_up_ml=w_up_ml, w_out=w_out, xa_norm=xa_norm, mem_norm=mem_norm,
                   xa_wq=xa_wq, xa_wkv=xa_wkv, xa_wo=xa_wo, ffn2_norm=ffn2_norm, ffn2_w_gate=ffn2_w_gate,
                   ffn2_w_up=ffn2_w_up, ffn2_w_down=ffn2_w_down)
    h = x.reshape(bsz * seq, d)
    mem2d = mem.reshape(bsz * mem.shape[1], d)
    bf = lambda w: w.astype(BF16)
    for l in range(depth):
        p = {name: w[l] for name, w in stacked.items()}
        h = _layer(h, mem2d, bsz, seq, p)
        last = l == depth - 1
        h = _ffn_call(h, _row(p["ffn2_norm"]), bf(p["ffn2_w_gate"]), bf(p["ffn2_w_up"]), bf(p["ffn2_w_down"]),
                      _row(final_norm), final=last)
    return h.reshape(bsz, seq, d)
```

```python
import functools
import math

import jax
import jax.numpy as jnp
import numpy as np
from jax import lax
from jax.experimental import pallas as pl
from jax.experimental.pallas import tpu as pltpu

F32 = jnp.float32
BF16 = jnp.bfloat16

D_MODEL = 1024
D_FF = 2816
FFN_HALF = 0.5
NORM_EPS = 1e-5
MEM_LEN = 256

RW_HEADS = 8
RW_HEAD_DIM = 64
RW_WIDTH = 512
DECAY_LORA = 64
AAA_LORA = 64
GATE_LORA = 128
RW_COLS = 3 * RW_WIDTH + DECAY_LORA + AAA_LORA + GATE_LORA
GN_EPS = 64e-5
RW_PAIRS = RW_HEADS // 2
PAIR = 2 * RW_HEAD_DIM

ML_HEADS = 4
ML_HEAD_DIM = 128
ML_WIDTH = 512
ML_CONV = 4
ML_NORM_EPS = 1e-6
ML_QKVO = 4 * ML_WIDTH
IF_PAD = 128

XA_HEADS = 4
XA_HEAD_DIM = 256

CHUNK = 64
LANES = 128
SUBLANES = 8
NEG_BIG = -1e30

VMEM_LIMIT = 56 * 1024 * 1024


def _bdot(a, b):
    return jnp.dot(a.astype(BF16), b.astype(BF16), preferred_element_type=F32)


def _bdot_nt(a, b):
    return lax.dot_general(a.astype(BF16), b.astype(BF16), (((1,), (1,)), ((), ())),
                           preferred_element_type=F32)


def _bdot_tn(a, b):
    return lax.dot_general(a.astype(BF16), b.astype(BF16), (((0,), (0,)), ((), ())),
                           preferred_element_type=F32)


def _split3(x):
    hi = x.astype(BF16)
    r1 = x - hi.astype(F32)
    mid = r1.astype(BF16)
    lo = (r1 - mid.astype(F32)).astype(BF16)
    return hi, mid, lo


def _dot_exact_lhs(c_bf16, x):
    hi, mid, lo = _split3(x)
    f = lambda p: jnp.dot(c_bf16, p, preferred_element_type=F32)
    return f(hi) + f(mid) + f(lo)


def _dot_exact_rhs(x, c_bf16):
    hi, mid, lo = _split3(x)
    f = lambda p: jnp.dot(p, c_bf16, preferred_element_type=F32)
    return f(hi) + f(mid) + f(lo)


def _dot_f32(a, b):
    a_hi = a.astype(BF16)
    a_lo = (a - a_hi.astype(F32)).astype(BF16)
    b_hi = b.astype(BF16)
    b_lo = (b - b_hi.astype(F32)).astype(BF16)
    f = lambda p, q: jnp.dot(p, q, preferred_element_type=F32)
    return f(a_hi, b_hi) + f(a_hi, b_lo) + f(a_lo, b_hi)


def _sigmoid(x):
    return 1.0 / (1.0 + jnp.exp(-x))


def _silu(x):
    return x * _sigmoid(x)


def _rmsnorm(x, g):
    ms = jnp.mean(x * x, axis=-1, keepdims=True)
    return x * lax.rsqrt(ms + NORM_EPS) * g


def _shift_rows(u, carry8, j):
    rolled = pltpu.roll(u, j, axis=0)
    rolled_c = pltpu.roll(carry8, j, axis=0)
    row = lax.broadcasted_iota(jnp.int32, (SUBLANES, u.shape[1]), 0)
    top = jnp.where(row < j, rolled_c, rolled[:SUBLANES])
    return jnp.concatenate([top, rolled[SUBLANES:]], axis=0)


def _const_spec(shape):
    nd = len(shape)
    return pl.BlockSpec(shape, lambda *_: (0,) * nd, pipeline_mode=pl.Buffered(1))


def _params(sem):
    return pltpu.CompilerParams(dimension_semantics=sem, vmem_limit_bytes=VMEM_LIMIT)


FF_CHUNK = 256
FFN_TM = 512


def _ffn_kernel(x_ref, g_ref, wg_ref, wu_ref, wd_ref, fg_ref, o_ref, *, final):
    x = x_ref[...]
    h = _rmsnorm(x, g_ref[...]).astype(BF16)
    acc = jnp.zeros(x.shape, F32)
    for c in range(D_FF // FF_CHUNK):
        sl = slice(c * FF_CHUNK, (c + 1) * FF_CHUNK)
        gate = jnp.dot(h, wg_ref[:, sl], preferred_element_type=F32)
        up = jnp.dot(h, wu_ref[:, sl], preferred_element_type=F32)
        act = (_silu(gate) * up).astype(BF16)
        acc = acc + jnp.dot(act, wd_ref[sl, :], preferred_element_type=F32)
    y = x + FFN_HALF * acc
    if final:
        y = _rmsnorm(y, fg_ref[...])
    o_ref[...] = y


def _ffn_call(x, g, wg, wu, wd, fg, final):
    t, d = x.shape
    tm = min(FFN_TM, t)
    return pl.pallas_call(
        functools.partial(_ffn_kernel, final=final),
        grid=(t // tm,),
        in_specs=[pl.BlockSpec((tm, d), lambda i: (i, 0)),
                  _const_spec((1, d)), _const_spec((d, D_FF)), _const_spec((d, D_FF)),
                  _const_spec((D_FF, d)), _const_spec((1, d))],
        out_specs=pl.BlockSpec((tm, d), lambda i: (i, 0)),
        out_shape=jax.ShapeDtypeStruct((t, d), F32),
        compiler_params=_params(("parallel",)),
        name="ffn_final" if final else "ffn",
    )(x, g, wg, wu, wd, fg)


INPROJ_TM = 256


def _inproj_kernel(x_ref, g_ref, wrw_ref, wml_ref, wif_ref, wgt_ref, mu_ref,
                   zrw_ref, zml_ref, zif_ref, gate_ref, carry_ref):
    @pl.when(pl.program_id(1) == 0)
    def _():
        carry_ref[...] = jnp.zeros(carry_ref.shape, F32)

    h = _rmsnorm(x_ref[...], g_ref[...]).astype(BF16)
    z = jnp.dot(h, wrw_ref[...], preferred_element_type=F32)
    prev = _shift_rows(z, carry_ref[...], 1)
    carry_ref[...] = z[z.shape[0] - SUBLANES:]
    zrw_ref[...] = z + (prev - z) * mu_ref[...]
    zml_ref[...] = jnp.dot(h, wml_ref[...], preferred_element_type=F32)
    zif_ref[...] = jnp.dot(h, wif_ref[...], preferred_element_type=F32)
    gate_ref[...] = _sigmoid(jnp.dot(h, wgt_ref[...], preferred_element_type=F32))


def _inproj_call(x, g, wrw, wml, wif, wgt, mu, bsz, seq):
    t, d = x.shape
    tm = min(INPROJ_TM, seq)
    nt = seq // tm
    row = lambda b, i: (b * nt + i, 0)
    return pl.pallas_call(
        _inproj_kernel,
        grid=(bsz, nt),
        in_specs=[pl.BlockSpec((tm, d), row),
                  _const_spec((1, d)), _const_spec((d, RW_COLS)), _const_spec((d, ML_QKVO)),
                  _const_spec((d, IF_PAD)), _const_spec((d, 2 * d)), _const_spec((1, RW_COLS))],
        out_specs=[pl.BlockSpec((tm, RW_COLS), row), pl.BlockSpec((tm, ML_QKVO), row),
                   pl.BlockSpec((tm, IF_PAD), row), pl.BlockSpec((tm, 2 * d), row)],
        out_shape=[jax.ShapeDtypeStruct((t, RW_COLS), F32), jax.ShapeDtypeStruct((t, ML_QKVO), F32),
                   jax.ShapeDtypeStruct((t, IF_PAD), F32), jax.ShapeDtypeStruct((t, 2 * d), F32)],
        scratch_shapes=[pltpu.VMEM((SUBLANES, RW_COLS), F32)],
        compiler_params=_params(("parallel", "arbitrary")),
        name="inproj",
    )(x, g, wrw, wml, wif, wgt, mu)


RW1_CHUNKS = 4
RW1_GROUP = 16
INV_LEVELS = 6


def _rw_masks():
    n = 2 * CHUNK
    t = np.arange(n)[:, None]
    s = np.arange(n)[None, :]
    same = (t // CHUNK) == (s // CHUNK)
    strict = same & (s < t)
    incl = same & (s <= t)
    levels = []
    for k in range(INV_LEVELS):
        size = 1 << k
        levels.append(((t // (2 * size)) == (s // (2 * size))) & ((t // size) % 2 == 1) & ((s // size) % 2 == 0))
    m = np.stack([strict, incl, np.eye(n, dtype=bool)] + levels).astype(np.float32)
    tri = (np.arange(CHUNK)[:, None] >= np.arange(CHUNK)[None, :]).astype(np.float32)
    return m, tri


def _group_ones(width, group):
    i = np.arange(width)
    return (i[:, None] // group == i[None, :] // group).astype(np.float32)


def _rw1_kernel(z_ref, w2_ref, a2_ref, g2_ref, vec_ref, gsum_ref, tri_ref, mask_ref,
                g_ref, yi_ref, bonus_ref, gate_ref, m_ref, c_ref):
    z = z_ref[...]
    rows = z.shape[0]
    r = z[:, 0:RW_WIDTH]
    k = z[:, RW_WIDTH:2 * RW_WIDTH]
    v = z[:, 2 * RW_WIDTH:3 * RW_WIDTH]
    lora_in = z[:, 3 * RW_WIDTH:3 * RW_WIDTH + DECAY_LORA + AAA_LORA]
    g_d = z[:, 3 * RW_WIDTH + DECAY_LORA + AAA_LORA:]
    w0 = vec_ref[0:1, :]
    a0 = vec_ref[1:2, :]
    k_k = vec_ref[2:3, :]
    k_a = vec_ref[3:4, :]
    r_k = vec_ref[4:5, :]

    w_raw = w0 + _bdot(jnp.tanh(lora_in), w2_ref[...])
    logw = -math.exp(-0.5) * _sigmoid(w_raw)
    a = _sigmoid(a0 + _bdot(lora_in, a2_ref[...]))
    gate_ref[...] = _bdot(_sigmoid(g_d), g2_ref[...])

    gsum = gsum_ref[...]
    kk = k * k_k
    kk = kk / jnp.maximum(jnp.sqrt(_dot_exact_rhs(kk * kk, gsum)), 1e-12)
    k_mod = k * (1.0 + (a - 1.0) * k_a)
    b = kk * a
    bonus_ref[...] = _dot_exact_rhs(r * k_mod * r_k, gsum) * v

    lane = lax.broadcasted_iota(jnp.int32, (CHUNK, PAIR), 1)
    head0 = lane < RW_HEAD_DIM
    strict = mask_ref[0]
    incl = mask_ref[1]
    eye = mask_ref[2]
    tri = tri_ref[...]

    def stack(x):
        return jnp.concatenate([jnp.where(head0, x, 0.0), jnp.where(head0, 0.0, x)], axis=0)

    ops = {}
    for c in range(rows // CHUNK):
        rs = slice(c * CHUNK, (c + 1) * CHUNK)
        lw = logw[rs]
        cum = _dot_exact_lhs(tri, lw)
        cum_last = cum[CHUNK - 1:CHUNK]
        w_incl = jnp.exp(cum)
        w_inv = jnp.exp(-cum)
        w_prev = jnp.exp(cum - lw)
        w_end = jnp.exp(cum_last - cum)
        w_last = jnp.exp(cum_last)
        full = dict(at=-kk[rs] * w_prev, rt=r[rs] * w_incl, bt=b[rs] * w_inv, kt=k_mod[rs] * w_inv,
                    be=b[rs] * w_end, ke=k_mod[rs] * w_end, v2=v[rs])
        for p in range(RW_PAIRS):
            ls = slice(p * PAIR, (p + 1) * PAIR)
            item = {name: stack(val[:, ls]) for name, val in full.items()}
            item["w_last"] = w_last[:, ls]
            ops[(c, p)] = item

    keys = list(ops)
    for g0 in range(0, len(keys), RW1_GROUP):
        group = keys[g0:g0 + RW1_GROUP]
        n_ab, a_ak, a_rb, a_rk, t_inv = {}, {}, {}, {}, {}
        for key in group:
            o = ops[key]
            prod = _bdot_nt(jnp.concatenate([o["at"], o["rt"]], axis=0),
                            jnp.concatenate([o["bt"], o["kt"]], axis=0))
            n_ab[key] = prod[:PAIR, :PAIR] * strict
            a_ak[key] = prod[:PAIR, PAIR:] * strict
            a_rb[key] = prod[PAIR:, :PAIR] * incl
            a_rk[key] = prod[PAIR:, PAIR:] * incl
            t_inv[key] = eye + n_ab[key] * mask_ref[3]
        for lvl in range(1, INV_LEVELS):
            half = {key: _bdot(t_inv[key], n_ab[key] * mask_ref[3 + lvl]) for key in group}
            t_inv = {key: t_inv[key] + _bdot(half[key], t_inv[key]) for key in group}
        akv = {key: _bdot(a_ak[key], ops[key]["v2"]) for key in group}
        pq = {key: _bdot(t_inv[key], jnp.concatenate([ops[key]["at"], akv[key]], axis=1))
              for key in group}
        gy = {key: _bdot(a_rb[key], pq[key]) for key in group}
        ark_v = {key: _bdot(a_rk[key], ops[key]["v2"]) for key in group}
        mc = {key: _bdot_tn(ops[key]["be"], pq[key]) for key in group}
        ke_v = {key: _bdot_tn(ops[key]["ke"], ops[key]["v2"]) for key in group}
        for key in group:
            c, p = key
            rs = slice(c * CHUNK, (c + 1) * CHUNK)
            ls = slice(p * PAIR, (p + 1) * PAIR)
            g2 = ops[key]["rt"] + gy[key][:, :PAIR]
            yi2 = gy[key][:, PAIR:] + ark_v[key]
            g_ref[rs, ls] = g2[:CHUNK] + g2[CHUNK:]
            yi_ref[rs, ls] = yi2[:CHUNK] + yi2[CHUNK:]
            m_ref[c, p] = eye * ops[key]["w_last"] + mc[key][:, :PAIR]
            c_ref[c, p] = mc[key][:, PAIR:] + ke_v[key]


def _rw1_call(zrw, w2p, a2p, g2, vec, gsum, tri, masks):
    t = zrw.shape[0]
    rows = RW1_CHUNKS * CHUNK
    nchunks = t // CHUNK
    tok = lambda i: (i, 0)
    blk = lambda i: (i, 0, 0, 0)
    tok_spec = pl.BlockSpec((rows, RW_WIDTH), tok)
    mc_spec = pl.BlockSpec((RW1_CHUNKS, RW_PAIRS, PAIR, PAIR), blk)
    tok_shape = jax.ShapeDtypeStruct((t, RW_WIDTH), F32)
    mc_shape = jax.ShapeDtypeStruct((nchunks, RW_PAIRS, PAIR, PAIR), F32)
    return pl.pallas_call(
        _rw1_kernel,
        grid=(t // rows,),
        in_specs=[pl.BlockSpec((rows, RW_COLS), tok),
                  _const_spec(w2p.shape), _const_spec(a2p.shape), _const_spec(g2.shape),
                  _const_spec(vec.shape), _const_spec(gsum.shape), _const_spec(tri.shape),
                  _const_spec(masks.shape)],
        out_specs=[tok_spec, tok_spec, tok_spec, tok_spec, mc_spec, mc_spec],
        out_shape=[tok_shape, tok_shape, tok_shape, tok_shape, mc_shape, mc_shape],
        compiler_params=_params(("parallel",)),
        name="rwkv_chunk_local",
    )(zrw, w2p, a2p, g2, vec, gsum, tri, masks)


RW2_CHUNKS = 8


def _rw2_kernel(g_ref, yi_ref, bonus_ref, gate_ref, m_ref, c_ref, gmean_ref, gn_ref, y_ref, h_ref):
    @pl.when(pl.program_id(1) == 0)
    def _():
        h_ref[...] = jnp.zeros(h_ref.shape, F32)

    ys = []
    for c in range(RW2_CHUNKS):
        rs = slice(c * CHUNK, (c + 1) * CHUNK)
        cols = []
        for p in range(RW_PAIRS):
            ls = slice(p * PAIR, (p + 1) * PAIR)
            h = h_ref[p]
            cols.append(_dot_f32(g_ref[rs, ls], h) + yi_ref[rs, ls])
            h_ref[p] = _dot_f32(m_ref[c, p], h) + c_ref[c, p]
        ys.append(jnp.concatenate(cols, axis=1))
    y = jnp.concatenate(ys, axis=0)
    gmean = gmean_ref[...]
    mu = _dot_exact_rhs(y, gmean)
    d = y - mu
    var = _dot_exact_rhs(d * d, gmean)
    y = d * lax.rsqrt(var + GN_EPS) * gn_ref[0:1, :] + gn_ref[1:2, :]
    y_ref[...] = ((y + bonus_ref[...]) * gate_ref[...]).astype(BF16)


def _rw2_call(g, yi, bonus, gate, m, c, gmean, gn, bsz, seq):
    t = g.shape[0]
    rows = RW2_CHUNKS * CHUNK
    nt = seq // rows
    tok = lambda b, i: (b * nt + i, 0)
    blk = lambda b, i: (b * nt + i, 0, 0, 0)
    tok_spec = pl.BlockSpec((rows, RW_WIDTH), tok)
    mc_spec = pl.BlockSpec((RW2_CHUNKS, RW_PAIRS, PAIR, PAIR), blk)
    return pl.pallas_call(
        _rw2_kernel,
        grid=(bsz, nt),
        in_specs=[tok_spec, tok_spec, tok_spec, tok_spec, mc_spec, mc_spec,
                  _const_spec(gmean.shape), _const_spec(gn.shape)],
        out_specs=tok_spec,
        out_shape=jax.ShapeDtypeStruct((t, RW_WIDTH), BF16),
        scratch_shapes=[pltpu.VMEM((RW_PAIRS, PAIR, PAIR), F32)],
        compiler_params=_params(("parallel", "arbitrary")),
        name="rwkv_scan",
    )(g, yi, bonus, gate, m, c, gmean, gn)


ML_CHUNKS = 4


def _log_sigmoid(x):
    return jnp.minimum(x, 0.0) - jnp.log(1.0 + jnp.exp(-jnp.abs(x)))


def _ml_kernel(z_ref, zif_ref, convw_ref, convb_ref, ifb_ref, norm_ref, tri_ref,
               y_ref, carry_ref, state_ref, m_ref):
    @pl.when(pl.program_id(1) == 0)
    def _():
        carry_ref[...] = jnp.zeros(carry_ref.shape, F32)
        state_ref[...] = jnp.zeros(state_ref.shape, F32)
        m_ref[...] = jnp.zeros(m_ref.shape, F32)

    rows = z_ref.shape[0]
    u = z_ref[:, 0:2 * ML_WIDTH]
    carry = carry_ref[...]
    conv = convb_ref[...] + convw_ref[0:1, :] * u
    for j in range(1, ML_CONV):
        conv = conv + convw_ref[j:j + 1, :] * _shift_rows(u, carry, j)
    carry_ref[...] = u[rows - SUBLANES:]
    qk = _silu(conv)
    q_all = qk[:, 0:ML_WIDTH] * (ML_HEAD_DIM ** -0.5)
    k_all = qk[:, ML_WIDTH:]
    v_all = z_ref[:, 2 * ML_WIDTH:3 * ML_WIDTH]
    o_all = z_ref[:, 3 * ML_WIDTH:4 * ML_WIDTH]

    pre = zif_ref[...] + ifb_ref[...]
    logf = _log_sigmoid(pre)
    tri = tri_ref[...]
    lane = lax.broadcasted_iota(jnp.int32, (CHUNK, LANES), 1)
    ones_col = jnp.where(lane == 0, 1.0, 0.0)
    tq = lax.broadcasted_iota(jnp.int32, (CHUNK, CHUNK), 0)
    tk = lax.broadcasted_iota(jnp.int32, (CHUNK, CHUNK), 1)
    causal = tk <= tq

    for c in range(rows // CHUNK):
        rs = slice(c * CHUNK, (c + 1) * CHUNK)
        bcum = _dot_exact_lhs(tri, logf[rs])
        ig = pre[rs]
        bcum_t = bcum.T
        ig_t = ig.T
        outs = []
        for h in range(ML_HEADS):
            ls = slice(h * ML_HEAD_DIM, (h + 1) * ML_HEAD_DIM)
            b_col = bcum[:, ML_HEADS + h:ML_HEADS + h + 1]
            i_col = ig[:, h:h + 1]
            b_row = bcum_t[ML_HEADS + h:ML_HEADS + h + 1, :]
            i_row = ig_t[h:h + 1, :]
            b_end = b_col[CHUNK - 1:CHUNK]
            m_prev = m_ref[h][:, 0:1]
            q = q_all[rs, ls]
            k = k_all[rs, ls]
            v_aug = jnp.concatenate([v_all[rs, ls], ones_col], axis=1)
            state = state_ref[h]

            log_d = jnp.where(causal, b_col - b_row + i_row, NEG_BIG)
            log_inter = b_col + m_prev
            m_row = jnp.maximum(log_inter, jnp.max(log_d, axis=-1, keepdims=True))
            p = jnp.exp(log_d - m_row) * _bdot_nt(q, k)
            w_inter = jnp.exp(log_inter - m_row)
            both = _bdot(p, v_aug) + w_inter * _bdot(q, state)
            num = both[:, :ML_HEAD_DIM]
            den = both[:, ML_HEAD_DIM:ML_HEAD_DIM + 1]
            hh = num / jnp.maximum(jnp.abs(den), jnp.exp(-m_row))

            log_w_end = b_end - b_col + i_col
            m_chunk = jnp.max(log_w_end, axis=0, keepdims=True)
            w_end = jnp.exp(log_w_end - m_chunk)
            m_new = jnp.maximum(b_end + m_prev, m_chunk)
            dec = jnp.exp(b_end + m_prev - m_new)
            sc = jnp.exp(m_chunk - m_new)
            state_ref[h] = dec * state + sc * _bdot_tn(k, w_end * v_aug)
            m_ref[h] = jnp.broadcast_to(m_new, (1, LANES))

            mu = jnp.mean(hh, axis=-1, keepdims=True)
            d = hh - mu
            var = jnp.mean(d * d, axis=-1, keepdims=True)
            hn = d * lax.rsqrt(var + ML_NORM_EPS)
            outs.append(hn * norm_ref[:, ls] * _sigmoid(o_all[rs, ls]))
        y_ref[rs, :] = jnp.concatenate(outs, axis=1).astype(BF16)


def _ml_call(zml, zif, convw, convb, ifb, norm, tri, bsz, seq):
    t = zml.shape[0]
    rows = ML_CHUNKS * CHUNK
    nt = seq // rows
    tok = lambda b, i: (b * nt + i, 0)
    return pl.pallas_call(
        _ml_kernel,
        grid=(bsz, nt),
        in_specs=[pl.BlockSpec((rows, ML_QKVO), tok), pl.BlockSpec((rows, IF_PAD), tok),
                  _const_spec(convw.shape), _const_spec(convb.shape), _const_spec(ifb.shape),
                  _const_spec(norm.shape), _const_spec(tri.shape)],
        out_specs=pl.BlockSpec((rows, ML_WIDTH), tok),
        out_shape=jax.ShapeDtypeStruct((t, ML_WIDTH), BF16),
        scratch_shapes=[pltpu.VMEM((SUBLANES, 2 * ML_WIDTH), F32),
                        pltpu.VMEM((ML_HEADS, ML_HEAD_DIM, 2 * ML_HEAD_DIM), F32),
                        pltpu.VMEM((ML_HEADS, 1, LANES), F32)],
        compiler_params=_params(("parallel", "arbitrary")),
        name="mlstm",
    )(zml, zif, convw, convb, ifb, norm, tri)


MERGE_TM = 512


def _merge_kernel(x_ref, yrw_ref, yml_ref, gate_ref, wur_ref, wum_ref, wo_ref, o_ref):
    d = x_ref.shape[1]
    up_rw = jnp.dot(yrw_ref[...], wur_ref[...], preferred_element_type=F32)
    up_ml = jnp.dot(yml_ref[...], wum_ref[...], preferred_element_type=F32)
    merged = gate_ref[:, 0:d] * up_rw + gate_ref[:, d:] * up_ml
    o_ref[...] = x_ref[...] + _bdot(merged, wo_ref[...])


def _merge_call(x, yrw, yml, gates, wur, wum, wo):
    t, d = x.shape
    tm = min(MERGE_TM, t)
    row = lambda i: (i, 0)
    return pl.pallas_call(
        _merge_kernel,
        grid=(t // tm,),
        in_specs=[pl.BlockSpec((tm, d), row), pl.BlockSpec((tm, RW_WIDTH), row),
                  pl.BlockSpec((tm, ML_WIDTH), row), pl.BlockSpec((tm, 2 * d), row),
                  _const_spec(wur.shape), _const_spec(wum.shape), _const_spec(wo.shape)],
        out_specs=pl.BlockSpec((tm, d), row),
        out_shape=jax.ShapeDtypeStruct((t, d), F32),
        compiler_params=_params(("parallel",)),
        name="merge",
    )(x, yrw, yml, gates, wur, wum, wo)


def _memkv_kernel(mem_ref, g_ref, wkv_ref, k_ref, v_ref):
    d = mem_ref.shape[1]
    hm = _rmsnorm(mem_ref[...], g_ref[...])
    kv = _bdot(hm, wkv_ref[...])
    k_ref[...] = kv[:, 0:d].astype(BF16)
    v_ref[...] = kv[:, d:].astype(BF16)


def _memkv_call(mem2d, g, wkv):
    n, d = mem2d.shape
    row = lambda i: (i, 0)
    return pl.pallas_call(
        _memkv_kernel,
        grid=(n // MEM_LEN,),
        in_specs=[pl.BlockSpec((MEM_LEN, d), row), _const_spec(g.shape), _const_spec(wkv.shape)],
        out_specs=[pl.BlockSpec((MEM_LEN, d), row), pl.BlockSpec((MEM_LEN, d), row)],
        out_shape=[jax.ShapeDtypeStruct((n, d), BF16), jax.ShapeDtypeStruct((n, d), BF16)],
        compiler_params=_params(("parallel",)),
        name="mem_kv",
    )(mem2d, g, wkv)


XA_TM = 512


def _xattn_kernel(x_ref, g_ref, wq_ref, k_ref, v_ref, wo_ref, o_ref):
    x = x_ref[...]
    hq = _rmsnorm(x, g_ref[...])
    q = _bdot(hq, wq_ref[...]) * (XA_HEAD_DIM ** -0.5)
    outs = []
    for h in range(XA_HEADS):
        ls = slice(h * XA_HEAD_DIM, (h + 1) * XA_HEAD_DIM)
        s = _bdot_nt(q[:, ls], k_ref[:, ls])
        s = s - jnp.max(s, axis=-1, keepdims=True)
        e = jnp.exp(s)
        p = e / jnp.sum(e, axis=-1, keepdims=True)
        outs.append(_bdot(p, v_ref[:, ls]))
    o = jnp.concatenate(outs, axis=1)
    o_ref[...] = x + _bdot(o, wo_ref[...])


def _xattn_call(x, g, wq, kmem, vmem, wo, bsz, seq):
    t, d = x.shape
    tm = min(XA_TM, seq)
    nt = seq // tm
    row = lambda b, i: (b * nt + i, 0)
    mem = lambda b, i: (b, 0)
    return pl.pallas_call(
        _xattn_kernel,
        grid=(bsz, nt),
        in_specs=[pl.BlockSpec((tm, d), row), _const_spec(g.shape), _const_spec(wq.shape),
                  pl.BlockSpec((MEM_LEN, d), mem), pl.BlockSpec((MEM_LEN, d), mem),
                  _const_spec(wo.shape)],
        out_specs=pl.BlockSpec((tm, d), row),
        out_shape=jax.ShapeDtypeStruct((t, d), F32),
        compiler_params=_params(("parallel", "parallel")),
        name="mem_xattn",
    )(x, g, wq, kmem, vmem, wo)


def _row(v):
    return v.reshape(1, -1).astype(F32)


def _layer(x, mem2d, bsz, seq, p):
    d = D_MODEL
    bf = lambda w: w.astype(BF16)
    masks_np, tri_np = _rw_masks()
    masks = jnp.asarray(masks_np)
    tri = jnp.asarray(tri_np, dtype=BF16)
    gsum = jnp.asarray(_group_ones(RW_WIDTH, RW_HEAD_DIM), dtype=BF16)
    gmean = jnp.asarray(_group_ones(RW_WIDTH, RW_HEAD_DIM) / RW_HEAD_DIM, dtype=BF16)

    x = _ffn_call(x, _row(p["ffn1_norm"]), bf(p["ffn1_w_gate"]), bf(p["ffn1_w_up"]), bf(p["ffn1_w_down"]),
                  _row(p["ffn1_norm"]), final=False)

    w_in = p["w_in"]
    c0 = RW_COLS
    c1 = c0 + ML_QKVO
    c2 = c1 + 2 * ML_HEADS
    w_if = jnp.pad(w_in[:, c1:c2], ((0, 0), (0, IF_PAD - 2 * ML_HEADS)))
    zrw, zml, zif, gates = _inproj_call(x, _row(p["mix_norm"]), bf(w_in[:, :c0]), bf(w_in[:, c0:c1]), bf(w_if),
                                        bf(w_in[:, c2:]), _row(p["rw_mu"]), bsz, seq)

    zpad = jnp.zeros((DECAY_LORA, RW_WIDTH), F32)
    w2p = bf(jnp.concatenate([p["rw_w2"], zpad], axis=0))
    a2p = bf(jnp.concatenate([zpad, p["rw_a2"]], axis=0))
    vec = jnp.stack([p["rw_w0"], p["rw_a0"], p["rw_k_k"], p["rw_k_a"], p["rw_r_k"].reshape(-1),
                     jnp.zeros_like(p["rw_w0"]), jnp.zeros_like(p["rw_w0"]), jnp.zeros_like(p["rw_w0"])]).astype(F32)
    g, yi, bonus, gate, m, c = _rw1_call(zrw, w2p, a2p, bf(p["rw_g2"]), vec, gsum, tri, masks)
    gn = jnp.concatenate([jnp.stack([p["rw_gn_w"], p["rw_gn_b"]]).astype(F32),
                          jnp.zeros((SUBLANES - 2, RW_WIDTH), F32)], axis=0)
    y_rw = _rw2_call(g, yi, bonus, gate, m, c, gmean, gn, bsz, seq)

    ifb = jnp.pad(jnp.concatenate([p["ml_i_b"], p["ml_f_b"]]).astype(F32), (0, IF_PAD - 2 * ML_HEADS)).reshape(1, -1)
    convw = jnp.concatenate([p["ml_conv_w"].astype(F32), jnp.zeros((SUBLANES - ML_CONV, 2 * ML_WIDTH), F32)], axis=0)
    y_ml = _ml_call(zml, zif, convw, _row(p["ml_conv_b"]), ifb, _row(p["ml_norm"]), tri, bsz, seq)

    x = _merge_call(x, y_rw, y_ml, gates, bf(p["w_up_rw"]), bf(p["w_up_ml"]), bf(p["w_out"]))

    kmem, vmem = _memkv_call(mem2d, _row(p["mem_norm"]), bf(p["xa_wkv"]))
    x = _xattn_call(x, _row(p["xa_norm"]), bf(p["xa_wq"]), kmem, vmem, bf(p["xa_wo"]), bsz, seq)
    return x


def kernel(x, mem, ffn1_norm, ffn1_w_gate, ffn1_w_up, ffn1_w_down, mix_norm, w_in, rw_mu, rw_w0, rw_w2, rw_a0,
           rw_a2, rw_g2, rw_k_k, rw_k_a, rw_r_k, rw_gn_w, rw_gn_b, ml_conv_w, ml_conv_b, ml_i_b, ml_f_b, ml_norm,
           w_up_rw, w_up_ml, w_out, xa_norm, mem_norm, xa_wq, xa_wkv, xa_wo, ffn2_norm, ffn2_w_gate, ffn2_w_up,
           ffn2_w_down, final_norm):
    bsz, seq, d = x.shape
    depth = ffn1_norm.shape[0]
    stacked = dict(ffn1_norm=ffn1_norm, ffn1_w_gate=ffn1_w_gate, ffn1_w_up=ffn1_w_up, ffn1_w_down=ffn1_w_down,
                   mix_norm=mix_norm, w_in=w_in, rw_mu=rw_mu, rw_w0=rw_w0, rw_w2=rw_w2, rw_a0=rw_a0, rw_a2=rw_a2,
                   rw_g2=rw_g2, rw_k_k=rw_k_k, rw_k_a=rw_k_a, rw_r_k=rw_r_k, rw_gn_w=rw_gn_w, rw_gn_b=rw_gn_b,
                   ml_conv_w=ml_conv_w, ml_conv_b=ml_conv_b, ml_i_b=ml_i_b, ml_f_b=ml_f_b, ml_norm=ml_norm,
                   w_up_rw=w_up_rw, w_up_ml=w_up_ml, w_out=w_out, xa_norm=xa_norm, mem_norm=mem_norm,
                   xa_wq=xa_wq, xa_wkv=xa_wkv, xa_wo=xa_wo, ffn2_norm=ffn2_norm, ffn2_w_gate=ffn2_w_gate,
                   ffn2_w_up=ffn2_w_up, ffn2_w_down=ffn2_w_down)
    h = x.reshape(bsz * seq, d)
    mem2d = mem.reshape(bsz * mem.shape[1], d)
    bf = lambda w: w.astype(BF16)
    for l in range(depth):
        p = {name: w[l] for name, w in stacked.items()}
        h = _layer(h, mem2d, bsz, seq, p)
        last = l == depth - 1
        h = _ffn_call(h, _row(p["ffn2_norm"]), bf(p["ffn2_w_gate"]), bf(p["ffn2_w_up"]), bf(p["ffn2_w_down"]),
                      _row(final_norm), final=last)
    return h.reshape(bsz, seq, d)
```

```python
import functools
import math

import jax
import jax.numpy as jnp
import numpy as np
from jax import lax
from jax.experimental import pallas as pl
from jax.experimental.pallas import tpu as pltpu

F32 = jnp.float32
BF16 = jnp.bfloat16

D_MODEL = 1024
D_FF = 2816
FFN_HALF = 0.5
NORM_EPS = 1e-5
MEM_LEN = 256

RW_HEADS = 8
RW_HEAD_DIM = 64
RW_WIDTH = 512
DECAY_LORA = 64
AAA_LORA = 64
GATE_LORA = 128
RW_COLS = 3 * RW_WIDTH + DECAY_LORA + AAA_LORA + GATE_LORA
GN_EPS = 64e-5
RW_PAIRS = RW_HEADS // 2
PAIR = 2 * RW_HEAD_DIM

ML_HEADS = 4
ML_HEAD_DIM = 128
ML_WIDTH = 512
ML_CONV = 4
ML_NORM_EPS = 1e-6
ML_QKVO = 4 * ML_WIDTH
IF_PAD = 128

XA_HEADS = 4
XA_HEAD_DIM = 256

CHUNK = 64
LANES = 128
SUBLANES = 8
NEG_BIG = -1e30

VMEM_LIMIT = 56 * 1024 * 1024


def _bdot(a, b):
    return jnp.dot(a.astype(BF16), b.astype(BF16), preferred_element_type=F32)


def _bdot_nt(a, b):
    return lax.dot_general(a.astype(BF16), b.astype(BF16), (((1,), (1,)), ((), ())),
                           preferred_element_type=F32)


def _bdot_tn(a, b):
    return lax.dot_general(a.astype(BF16), b.astype(BF16), (((0,), (0,)), ((), ())),
                           preferred_element_type=F32)


def _split3(x):
    hi = x.astype(BF16)
    r1 = x - hi.astype(F32)
    mid = r1.astype(BF16)
    lo = (r1 - mid.astype(F32)).astype(BF16)
    return hi, mid, lo


def _dot_exact_lhs(c_bf16, x):
    hi, mid, lo = _split3(x)
    f = lambda p: jnp.dot(c_bf16, p, preferred_element_type=F32)
    return f(hi) + f(mid) + f(lo)


def _dot_split2_rhs(x, c_bf16):
    hi = x.astype(BF16)
    lo = (x - hi.astype(F32)).astype(BF16)
    f = lambda p: jnp.dot(p, c_bf16, preferred_element_type=F32)
    return f(hi) + f(lo)


def _dot_f32(a, b):
    a_hi = a.astype(BF16)
    a_lo = (a - a_hi.astype(F32)).astype(BF16)
    b_hi = b.astype(BF16)
    b_lo = (b - b_hi.astype(F32)).astype(BF16)
    f = lambda p, q: jnp.dot(p, q, preferred_element_type=F32)
    return f(a_hi, b_hi) + f(a_hi, b_lo) + f(a_lo, b_hi)


def _sigmoid(x):
    return 1.0 / (1.0 + jnp.exp(-x))


def _silu(x):
    return x * _sigmoid(x)


def _rmsnorm(x, g):
    ms = jnp.mean(x * x, axis=-1, keepdims=True)
    return x * lax.rsqrt(ms + NORM_EPS) * g


def _shift_rows(u, carry8, j):
    rolled = pltpu.roll(u, j, axis=0)
    rolled_c = pltpu.roll(carry8, j, axis=0)
    row = lax.broadcasted_iota(jnp.int32, (SUBLANES, u.shape[1]), 0)
    top = jnp.where(row < j, rolled_c, rolled[:SUBLANES])
    return jnp.concatenate([top, rolled[SUBLANES:]], axis=0)


def _const_spec(shape):
    nd = len(shape)
    return pl.BlockSpec(shape, lambda *_: (0,) * nd, pipeline_mode=pl.Buffered(1))


def _params(sem):
    return pltpu.CompilerParams(dimension_semantics=sem, vmem_limit_bytes=VMEM_LIMIT)


FF_CHUNK = 256
FFN_TM = 512


def _ffn_kernel(x_ref, g_ref, wg_ref, wu_ref, wd_ref, fg_ref, o_ref, *, final):
    x = x_ref[...]
    h = _rmsnorm(x, g_ref[...]).astype(BF16)
    acc = jnp.zeros(x.shape, F32)
    for c in range(D_FF // FF_CHUNK):
        sl = slice(c * FF_CHUNK, (c + 1) * FF_CHUNK)
        gate = jnp.dot(h, wg_ref[:, sl], preferred_element_type=F32)
        up = jnp.dot(h, wu_ref[:, sl], preferred_element_type=F32)
        act = (_silu(gate) * up).astype(BF16)
        acc = acc + jnp.dot(act, wd_ref[sl, :], preferred_element_type=F32)
    y = x + FFN_HALF * acc
    if final:
        y = _rmsnorm(y, fg_ref[...])
    o_ref[...] = y


def _ffn_call(x, g, wg, wu, wd, fg, final):
    t, d = x.shape
    tm = min(FFN_TM, t)
    return pl.pallas_call(
        functools.partial(_ffn_kernel, final=final),
        grid=(t // tm,),
        in_specs=[pl.BlockSpec((tm, d), lambda i: (i, 0)),
                  _const_spec((1, d)), _const_spec((d, D_FF)), _const_spec((d, D_FF)),
                  _const_spec((D_FF, d)), _const_spec((1, d))],
        out_specs=pl.BlockSpec((tm, d), lambda i: (i, 0)),
        out_shape=jax.ShapeDtypeStruct((t, d), F32),
        compiler_params=_params(("parallel",)),
        name="ffn_final" if final else "ffn",
    )(x, g, wg, wu, wd, fg)


INPROJ_TM = 256


def _inproj_kernel(x_ref, g_ref, wrw_ref, wml_ref, wif_ref, wgt_ref, mu_ref,
                   zrw_ref, zml_ref, zif_ref, gate_ref, carry_ref):
    @pl.when(pl.program_id(1) == 0)
    def _():
        carry_ref[...] = jnp.zeros(carry_ref.shape, F32)

    h = _rmsnorm(x_ref[...], g_ref[...]).astype(BF16)
    z = jnp.dot(h, wrw_ref[...], preferred_element_type=F32)
    prev = _shift_rows(z, carry_ref[...], 1)
    carry_ref[...] = z[z.shape[0] - SUBLANES:]
    zrw_ref[...] = z + (prev - z) * mu_ref[...]
    zml_ref[...] = jnp.dot(h, wml_ref[...], preferred_element_type=F32)
    zif_ref[...] = jnp.dot(h, wif_ref[...], preferred_element_type=F32)
    gate_ref[...] = _sigmoid(jnp.dot(h, wgt_ref[...], preferred_element_type=F32))


def _inproj_call(x, g, wrw, wml, wif, wgt, mu, bsz, seq):
    t, d = x.shape
    tm = min(INPROJ_TM, seq)
    nt = seq // tm
    row = lambda b, i: (b * nt + i, 0)
    return pl.pallas_call(
        _inproj_kernel,
        grid=(bsz, nt),
        in_specs=[pl.BlockSpec((tm, d), row),
                  _const_spec((1, d)), _const_spec((d, RW_COLS)), _const_spec((d, ML_QKVO)),
                  _const_spec((d, IF_PAD)), _const_spec((d, 2 * d)), _const_spec((1, RW_COLS))],
        out_specs=[pl.BlockSpec((tm, RW_COLS), row), pl.BlockSpec((tm, ML_QKVO), row),
                   pl.BlockSpec((tm, IF_PAD), row), pl.BlockSpec((tm, 2 * d), row)],
        out_shape=[jax.ShapeDtypeStruct((t, RW_COLS), F32), jax.ShapeDtypeStruct((t, ML_QKVO), F32),
                   jax.ShapeDtypeStruct((t, IF_PAD), F32), jax.ShapeDtypeStruct((t, 2 * d), F32)],
        scratch_shapes=[pltpu.VMEM((SUBLANES, RW_COLS), F32)],
        compiler_params=_params(("parallel", "arbitrary")),
        name="inproj",
    )(x, g, wrw, wml, wif, wgt, mu)


RW1_CHUNKS = 4
RW1_GROUP = 16
INV_LEVELS = 6


def _rw_masks():
    n = 2 * CHUNK
    t = np.arange(n)[:, None]
    s = np.arange(n)[None, :]
    same = (t // CHUNK) == (s // CHUNK)
    strict = same & (s < t)
    incl = same & (s <= t)
    levels = []
    for k in range(INV_LEVELS):
        size = 1 << k
        levels.append(((t // (2 * size)) == (s // (2 * size))) & ((t // size) % 2 == 1) & ((s // size) % 2 == 0))
    m = np.stack([strict, incl, np.eye(n, dtype=bool)] + levels).astype(np.float32)
    tri = (np.arange(CHUNK)[:, None] >= np.arange(CHUNK)[None, :]).astype(np.float32)
    return m, tri


def _group_ones(width, group):
    i = np.arange(width)
    return (i[:, None] // group == i[None, :] // group).astype(np.float32)


def _rw1_kernel(z_ref, w2_ref, a2_ref, g2_ref, vec_ref, gsum_ref, tri_ref, mask_ref,
                g_ref, yi_ref, bonus_ref, gate_ref, m_ref, c_ref):
    z = z_ref[...]
    rows = z.shape[0]
    r = z[:, 0:RW_WIDTH]
    k = z[:, RW_WIDTH:2 * RW_WIDTH]
    v = z[:, 2 * RW_WIDTH:3 * RW_WIDTH]
    lora_in = z[:, 3 * RW_WIDTH:3 * RW_WIDTH + DECAY_LORA + AAA_LORA]
    g_d = z[:, 3 * RW_WIDTH + DECAY_LORA + AAA_LORA:]
    w0 = vec_ref[0:1, :]
    a0 = vec_ref[1:2, :]
    k_k = vec_ref[2:3, :]
    k_a = vec_ref[3:4, :]
    r_k = vec_ref[4:5, :]

    w_raw = w0 + _bdot(jnp.tanh(lora_in), w2_ref[...])
    logw = -math.exp(-0.5) * _sigmoid(w_raw)
    a = _sigmoid(a0 + _bdot(lora_in, a2_ref[...]))
    gate_ref[...] = _bdot(_sigmoid(g_d), g2_ref[...])

    gsum = gsum_ref[...]
    kk = k * k_k
    kk = kk / jnp.maximum(jnp.sqrt(_bdot(kk * kk, gsum)), 1e-12)
    k_mod = k * (1.0 + (a - 1.0) * k_a)
    b = kk * a
    bonus_ref[...] = _dot_split2_rhs(r * k_mod * r_k, gsum) * v

    lane = lax.broadcasted_iota(jnp.int32, (CHUNK, PAIR), 1)
    head0 = lane < RW_HEAD_DIM
    strict = mask_ref[0]
    incl = mask_ref[1]
    eye = mask_ref[2]
    tri = tri_ref[...]

    def stack(x):
        return jnp.concatenate([jnp.where(head0, x, 0.0), jnp.where(head0, 0.0, x)], axis=0)

    ops = {}
    for c in range(rows // CHUNK):
        rs = slice(c * CHUNK, (c + 1) * CHUNK)
        lw = logw[rs]
        cum = _dot_exact_lhs(tri, lw)
        cum_last = cum[CHUNK - 1:CHUNK]
        w_incl = jnp.exp(cum)
        w_inv = jnp.exp(-cum)
        w_prev = jnp.exp(cum - lw)
        w_end = jnp.exp(cum_last - cum)
        w_last = jnp.exp(cum_last)
        full = dict(at=-kk[rs] * w_prev, rt=r[rs] * w_incl, bt=b[rs] * w_inv, kt=k_mod[rs] * w_inv,
                    be=b[rs] * w_end, ke=k_mod[rs] * w_end, v2=v[rs])
        for p in range(RW_PAIRS):
            ls = slice(p * PAIR, (p + 1) * PAIR)
            item = {name: stack(val[:, ls]) for name, val in full.items()}
            item["w_last"] = w_last[:, ls]
            ops[(c, p)] = item

    keys = list(ops)
    for g0 in range(0, len(keys), RW1_GROUP):
        group = keys[g0:g0 + RW1_GROUP]
        n_ab, a_ak, a_rb, a_rk, t_inv = {}, {}, {}, {}, {}
        for key in group:
            o = ops[key]
            prod = _bdot_nt(jnp.concatenate([o["at"], o["rt"]], axis=0),
                            jnp.concatenate([o["bt"], o["kt"]], axis=0))
            n_ab[key] = prod[:PAIR, :PAIR] * strict
            a_ak[key] = prod[:PAIR, PAIR:] * strict
            a_rb[key] = prod[PAIR:, :PAIR] * incl
            a_rk[key] = prod[PAIR:, PAIR:] * incl
            t_inv[key] = eye + n_ab[key] * mask_ref[3]
        for lvl in range(1, INV_LEVELS):
            half = {key: _bdot(t_inv[key], n_ab[key] * mask_ref[3 + lvl]) for key in group}
            t_inv = {key: t_inv[key] + _bdot(half[key], t_inv[key]) for key in group}
        akv = {key: _bdot(a_ak[key], ops[key]["v2"]) for key in group}
        pq = {key: _bdot(t_inv[key], jnp.concatenate([ops[key]["at"], akv[key]], axis=1))
              for key in group}
        gy = {key: _bdot(a_rb[key], pq[key]) for key in group}
        ark_v = {key: _bdot(a_rk[key], ops[key]["v2"]) for key in group}
        mc = {key: _bdot_tn(ops[key]["be"], pq[key]) for key in group}
        ke_v = {key: _bdot_tn(ops[key]["ke"], ops[key]["v2"]) for key in group}
        for key in group:
            c, p = key
            rs = slice(c * CHUNK, (c + 1) * CHUNK)
            ls = slice(p * PAIR, (p + 1) * PAIR)
            g2 = ops[key]["rt"] + gy[key][:, :PAIR]
            yi2 = gy[key][:, PAIR:] + ark_v[key]
            g_ref[rs, ls] = g2[:CHUNK] + g2[CHUNK:]
            yi_ref[rs, ls] = yi2[:CHUNK] + yi2[CHUNK:]
            m_ref[c, p] = eye * ops[key]["w_last"] + mc[key][:, :PAIR]
            c_ref[c, p] = mc[key][:, PAIR:] + ke_v[key]


def _rw1_call(zrw, w2p, a2p, g2, vec, gsum, tri, masks):
    t = zrw.shape[0]
    rows = RW1_CHUNKS * CHUNK
    nchunks = t // CHUNK
    tok = lambda i: (i, 0)
    blk = lambda i: (i, 0, 0, 0)
    tok_spec = pl.BlockSpec((rows, RW_WIDTH), tok)
    mc_spec = pl.BlockSpec((RW1_CHUNKS, RW_PAIRS, PAIR, PAIR), blk)
    tok_shape = jax.ShapeDtypeStruct((t, RW_WIDTH), F32)
    mc_shape = jax.ShapeDtypeStruct((nchunks, RW_PAIRS, PAIR, PAIR), F32)
    return pl.pallas_call(
        _rw1_kernel,
        grid=(t // rows,),
        in_specs=[pl.BlockSpec((rows, RW_COLS), tok),
                  _const_spec(w2p.shape), _const_spec(a2p.shape), _const_spec(g2.shape),
                  _const_spec(vec.shape), _const_spec(gsum.shape), _const_spec(tri.shape),
                  _const_spec(masks.shape)],
        out_specs=[tok_spec, tok_spec, tok_spec, tok_spec, mc_spec, mc_spec],
        out_shape=[tok_shape, tok_shape, tok_shape, tok_shape, mc_shape, mc_shape],
        compiler_params=_params(("parallel",)),
        name="rwkv_chunk_local",
    )(zrw, w2p, a2p, g2, vec, gsum, tri, masks)


RW2_CHUNKS = 8


def _rw2_kernel(g_ref, yi_ref, bonus_ref, gate_ref, m_ref, c_ref, gmean_ref, gn_ref, y_ref, h_ref):
    @pl.when(pl.program_id(1) == 0)
    def _():
        h_ref[...] = jnp.zeros(h_ref.shape, F32)

    ys = []
    for c in range(RW2_CHUNKS):
        rs = slice(c * CHUNK, (c + 1) * CHUNK)
        cols = []
        for p in range(RW_PAIRS):
            ls = slice(p * PAIR, (p + 1) * PAIR)
            h = h_ref[p]
            cols.append(_bdot(g_ref[rs, ls], h) + yi_ref[rs, ls])
            h_ref[p] = _dot_f32(m_ref[c, p], h) + c_ref[c, p]
        ys.append(jnp.concatenate(cols, axis=1))
    y = jnp.concatenate(ys, axis=0)
    gmean = gmean_ref[...]
    mu = _dot_split2_rhs(y, gmean)
    d = y - mu
    var = _bdot(d * d, gmean)
    y = d * lax.rsqrt(var + GN_EPS) * gn_ref[0:1, :] + gn_ref[1:2, :]
    y_ref[...] = ((y + bonus_ref[...]) * gate_ref[...]).astype(BF16)


def _rw2_call(g, yi, bonus, gate, m, c, gmean, gn, bsz, seq):
    t = g.shape[0]
    rows = RW2_CHUNKS * CHUNK
    nt = seq // rows
    tok = lambda b, i: (b * nt + i, 0)
    blk = lambda b, i: (b * nt + i, 0, 0, 0)
    tok_spec = pl.BlockSpec((rows, RW_WIDTH), tok)
    mc_spec = pl.BlockSpec((RW2_CHUNKS, RW_PAIRS, PAIR, PAIR), blk)
    return pl.pallas_call(
        _rw2_kernel,
        grid=(bsz, nt),
        in_specs=[tok_spec, tok_spec, tok_spec, tok_spec, mc_spec, mc_spec,
                  _const_spec(gmean.shape), _const_spec(gn.shape)],
        out_specs=tok_spec,
        out_shape=jax.ShapeDtypeStruct((t, RW_WIDTH), BF16),
        scratch_shapes=[pltpu.VMEM((RW_PAIRS, PAIR, PAIR), F32)],
        compiler_params=_params(("parallel", "arbitrary")),
        name="rwkv_scan",
    )(g, yi, bonus, gate, m, c, gmean, gn)


ML_CHUNK = 128
ML_STEP_CHUNKS = 2


def _log_sigmoid(x):
    return jnp.minimum(x, 0.0) - jnp.log(1.0 + jnp.exp(-jnp.abs(x)))


def _ml_kernel(z_ref, zif_ref, convw_ref, convb_ref, ifb_ref, norm_ref, tri_ref,
               y_ref, carry_ref, state_ref, m_ref):
    @pl.when(pl.program_id(1) == 0)
    def _():
        carry_ref[...] = jnp.zeros(carry_ref.shape, F32)
        state_ref[...] = jnp.zeros(state_ref.shape, F32)
        m_ref[...] = jnp.zeros(m_ref.shape, F32)

    rows = z_ref.shape[0]
    u = z_ref[:, 0:2 * ML_WIDTH]
    carry = carry_ref[...]
    conv = convb_ref[...] + convw_ref[0:1, :] * u
    for j in range(1, ML_CONV):
        conv = conv + convw_ref[j:j + 1, :] * _shift_rows(u, carry, j)
    carry_ref[...] = u[rows - SUBLANES:]
    qk = _silu(conv)
    q_all = qk[:, 0:ML_WIDTH] * (ML_HEAD_DIM ** -0.5)
    k_all = qk[:, ML_WIDTH:]
    v_all = z_ref[:, 2 * ML_WIDTH:3 * ML_WIDTH]
    o_all = z_ref[:, 3 * ML_WIDTH:4 * ML_WIDTH]

    pre = zif_ref[...] + ifb_ref[...]
    logf = _log_sigmoid(pre)
    tri = tri_ref[...]
    lane = lax.broadcasted_iota(jnp.int32, (ML_CHUNK, LANES), 1)
    tq = lax.broadcasted_iota(jnp.int32, (ML_CHUNK, ML_CHUNK), 0)
    tk = lax.broadcasted_iota(jnp.int32, (ML_CHUNK, ML_CHUNK), 1)
    causal = tk <= tq
    ones = jnp.ones((ML_CHUNK, ML_HEAD_DIM), F32)
    dense = (ML_CHUNK, ML_CHUNK)

    for c in range(rows // ML_CHUNK):
        rs = slice(c * ML_CHUNK, (c + 1) * ML_CHUNK)
        bcum = _dot_exact_lhs(tri, logf[rs])
        z = jnp.where(lane < ML_HEADS, pre[rs], bcum)
        z_t = z.T
        outs = []
        for h in range(ML_HEADS):
            ls = slice(h * ML_HEAD_DIM, (h + 1) * ML_HEAD_DIM)
            b_col = jnp.broadcast_to(z[:, ML_HEADS + h:ML_HEADS + h + 1], dense)
            ib_col = jnp.broadcast_to(z[:, h:h + 1] - z[:, ML_HEADS + h:ML_HEADS + h + 1], dense)
            ib_row = z_t[h:h + 1, :] - z_t[ML_HEADS + h:ML_HEADS + h + 1, :]
            b_end = b_col[ML_CHUNK - 1:ML_CHUNK]
            m_prev = m_ref[h]
            q = q_all[rs, ls]
            k = k_all[rs, ls]
            v_aug = jnp.concatenate([v_all[rs, ls], ones], axis=1)
            state = state_ref[h]

            log_d = jnp.where(causal, b_col + ib_row, NEG_BIG)
            log_inter = b_col + m_prev
            m_row = jnp.maximum(log_inter, jnp.max(log_d, axis=-1, keepdims=True))
            p = jnp.exp(log_d - m_row) * _bdot_nt(q, k)
            w_inter = jnp.exp(log_inter - m_row)
            intra = _bdot(p, v_aug)
            inter = _bdot(q, state)
            num = intra[:, :ML_HEAD_DIM] + w_inter * inter[:, :ML_HEAD_DIM]
            den = intra[:, ML_HEAD_DIM:] + w_inter * inter[:, ML_HEAD_DIM:]
            hh = num / jnp.maximum(jnp.abs(den), jnp.exp(-m_row))

            log_w_end = ib_col + b_end
            m_chunk = jnp.max(log_w_end, axis=0, keepdims=True)
            w_end = jnp.exp(log_w_end - m_chunk)
            m_new = jnp.maximum(b_end + m_prev, m_chunk)
            dec = jnp.exp(b_end + m_prev - m_new)
            sc = jnp.exp(m_chunk - m_new)
            kv = _bdot_tn(k, jnp.concatenate([w_end * v_all[rs, ls], w_end], axis=1))
            state_ref[h] = jnp.concatenate([dec, dec], axis=1) * state + jnp.concatenate([sc, sc], axis=1) * kv
            m_ref[h] = m_new

            mu = jnp.mean(hh, axis=-1, keepdims=True)
            d = hh - mu
            var = jnp.mean(d * d, axis=-1, keepdims=True)
            hn = d * lax.rsqrt(var + ML_NORM_EPS)
            outs.append(hn * norm_ref[:, ls] * _sigmoid(o_all[rs, ls]))
        y_ref[rs, :] = jnp.concatenate(outs, axis=1).astype(BF16)


def _ml_call(zml, zif, convw, convb, ifb, norm, tri, bsz, seq):
    t = zml.shape[0]
    rows = ML_STEP_CHUNKS * ML_CHUNK
    nt = seq // rows
    tok = lambda b, i: (b * nt + i, 0)
    return pl.pallas_call(
        _ml_kernel,
        grid=(bsz, nt),
        in_specs=[pl.BlockSpec((rows, ML_QKVO), tok), pl.BlockSpec((rows, IF_PAD), tok),
                  _const_spec(convw.shape), _const_spec(convb.shape), _const_spec(ifb.shape),
                  _const_spec(norm.shape), _const_spec(tri.shape)],
        out_specs=pl.BlockSpec((rows, ML_WIDTH), tok),
        out_shape=jax.ShapeDtypeStruct((t, ML_WIDTH), BF16),
        scratch_shapes=[pltpu.VMEM((SUBLANES, 2 * ML_WIDTH), F32),
                        pltpu.VMEM((ML_HEADS, ML_HEAD_DIM, 2 * ML_HEAD_DIM), F32),
                        pltpu.VMEM((ML_HEADS, 1, LANES), F32)],
        compiler_params=_params(("parallel", "arbitrary")),
        name="mlstm",
    )(zml, zif, convw, convb, ifb, norm, tri)


MERGE_TM = 512


def _merge_kernel(x_ref, yrw_ref, yml_ref, gate_ref, wur_ref, wum_ref, wo_ref, o_ref):
    d = x_ref.shape[1]
    up_rw = jnp.dot(yrw_ref[...], wur_ref[...], preferred_element_type=F32)
    up_ml = jnp.dot(yml_ref[...], wum_ref[...], preferred_element_type=F32)
    merged = gate_ref[:, 0:d] * up_rw + gate_ref[:, d:] * up_ml
    o_ref[...] = x_ref[...] + _bdot(merged, wo_ref[...])


def _merge_call(x, yrw, yml, gates, wur, wum, wo):
    t, d = x.shape
    tm = min(MERGE_TM, t)
    row = lambda i: (i, 0)
    return pl.pallas_call(
        _merge_kernel,
        grid=(t // tm,),
        in_specs=[pl.BlockSpec((tm, d), row), pl.BlockSpec((tm, RW_WIDTH), row),
                  pl.BlockSpec((tm, ML_WIDTH), row), pl.BlockSpec((tm, 2 * d), row),
                  _const_spec(wur.shape), _const_spec(wum.shape), _const_spec(wo.shape)],
        out_specs=pl.BlockSpec((tm, d), row),
        out_shape=jax.ShapeDtypeStruct((t, d), F32),
        compiler_params=_params(("parallel",)),
        name="merge",
    )(x, yrw, yml, gates, wur, wum, wo)


def _memkv_kernel(mem_ref, g_ref, wkv_ref, k_ref, v_ref):
    d = mem_ref.shape[1]
    hm = _rmsnorm(mem_ref[...], g_ref[...])
    kv = _bdot(hm, wkv_ref[...])
    k_ref[...] = kv[:, 0:d].astype(BF16)
    v_ref[...] = kv[:, d:].astype(BF16)


def _memkv_call(mem2d, g, wkv):
    n, d = mem2d.shape
    row = lambda i: (i, 0)
    return pl.pallas_call(
        _memkv_kernel,
        grid=(n // MEM_LEN,),
        in_specs=[pl.BlockSpec((MEM_LEN, d), row), _const_spec(g.shape), _const_spec(wkv.shape)],
        out_specs=[pl.BlockSpec((MEM_LEN, d), row), pl.BlockSpec((MEM_LEN, d), row)],
        out_shape=[jax.ShapeDtypeStruct((n, d), BF16), jax.ShapeDtypeStruct((n, d), BF16)],
        compiler_params=_params(("parallel",)),
        name="mem_kv",
    )(mem2d, g, wkv)


XA_TM = 512


def _xattn_kernel(x_ref, g_ref, wq_ref, k_ref, v_ref, wo_ref, o_ref):
    x = x_ref[...]
    hq = _rmsnorm(x, g_ref[...])
    q = _bdot(hq, wq_ref[...]) * (XA_HEAD_DIM ** -0.5)
    outs = []
    for h in range(XA_HEADS):
        ls = slice(h * XA_HEAD_DIM, (h + 1) * XA_HEAD_DIM)
        s = _bdot_nt(q[:, ls], k_ref[:, ls])
        s = s - jnp.max(s, axis=-1, keepdims=True)
        e = jnp.exp(s)
        p = e / jnp.sum(e, axis=-1, keepdims=True)
        outs.append(_bdot(p, v_ref[:, ls]))
    o = jnp.concatenate(outs, axis=1)
    o_ref[...] = x + _bdot(o, wo_ref[...])


def _xattn_call(x, g, wq, kmem, vmem, wo, bsz, seq):
    t, d = x.shape
    tm = min(XA_TM, seq)
    nt = seq // tm
    row = lambda b, i: (b * nt + i, 0)
    mem = lambda b, i: (b, 0)
    return pl.pallas_call(
        _xattn_kernel,
        grid=(bsz, nt),
        in_specs=[pl.BlockSpec((tm, d), row), _const_spec(g.shape), _const_spec(wq.shape),
                  pl.BlockSpec((MEM_LEN, d), mem), pl.BlockSpec((MEM_LEN, d), mem),
                  _const_spec(wo.shape)],
        out_specs=pl.BlockSpec((tm, d), row),
        out_shape=jax.ShapeDtypeStruct((t, d), F32),
        compiler_params=_params(("parallel", "parallel")),
        name="mem_xattn",
    )(x, g, wq, kmem, vmem, wo)


def _row(v):
    return v.reshape(1, -1).astype(F32)


def _layer(x, mem2d, bsz, seq, p):
    d = D_MODEL
    bf = lambda w: w.astype(BF16)
    masks_np, tri_np = _rw_masks()
    masks = jnp.asarray(masks_np)
    tri = jnp.asarray(tri_np, dtype=BF16)
    gsum = jnp.asarray(_group_ones(RW_WIDTH, RW_HEAD_DIM), dtype=BF16)
    gmean = jnp.asarray(_group_ones(RW_WIDTH, RW_HEAD_DIM) / RW_HEAD_DIM, dtype=BF16)

    x = _ffn_call(x, _row(p["ffn1_norm"]), bf(p["ffn1_w_gate"]), bf(p["ffn1_w_up"]), bf(p["ffn1_w_down"]),
                  _row(p["ffn1_norm"]), final=False)

    w_in = p["w_in"]
    c0 = RW_COLS
    c1 = c0 + ML_QKVO
    c2 = c1 + 2 * ML_HEADS
    w_if = jnp.pad(w_in[:, c1:c2], ((0, 0), (0, IF_PAD - 2 * ML_HEADS)))
    zrw, zml, zif, gates = _inproj_call(x, _row(p["mix_norm"]), bf(w_in[:, :c0]), bf(w_in[:, c0:c1]), bf(w_if),
                                        bf(w_in[:, c2:]), _row(p["rw_mu"]), bsz, seq)

    zpad = jnp.zeros((DECAY_LORA, RW_WIDTH), F32)
    w2p = bf(jnp.concatenate([p["rw_w2"], zpad], axis=0))
    a2p = bf(jnp.concatenate([zpad, p["rw_a2"]], axis=0))
    vec = jnp.stack([p["rw_w0"], p["rw_a0"], p["rw_k_k"], p["rw_k_a"], p["rw_r_k"].reshape(-1),
                     jnp.zeros_like(p["rw_w0"]), jnp.zeros_like(p["rw_w0"]), jnp.zeros_like(p["rw_w0"])]).astype(F32)
    g, yi, bonus, gate, m, c = _rw1_call(zrw, w2p, a2p, bf(p["rw_g2"]), vec, gsum, tri, masks)
    gn = jnp.concatenate([jnp.stack([p["rw_gn_w"], p["rw_gn_b"]]).astype(F32),
                          jnp.zeros((SUBLANES - 2, RW_WIDTH), F32)], axis=0)
    y_rw = _rw2_call(g, yi, bonus, gate, m, c, gmean, gn, bsz, seq)

    ifb = jnp.pad(jnp.concatenate([p["ml_i_b"], p["ml_f_b"]]).astype(F32), (0, IF_PAD - 2 * ML_HEADS)).reshape(1, -1)
    convw = jnp.concatenate([p["ml_conv_w"].astype(F32), jnp.zeros((SUBLANES - ML_CONV, 2 * ML_WIDTH), F32)], axis=0)
    tri_ml = jnp.asarray(np.tril(np.ones((ML_CHUNK, ML_CHUNK), np.float32)), dtype=BF16)
    y_ml = _ml_call(zml, zif, convw, _row(p["ml_conv_b"]), ifb, _row(p["ml_norm"]), tri_ml, bsz, seq)

    x = _merge_call(x, y_rw, y_ml, gates, bf(p["w_up_rw"]), bf(p["w_up_ml"]), bf(p["w_out"]))

    kmem, vmem = _memkv_call(mem2d, _row(p["mem_norm"]), bf(p["xa_wkv"]))
    x = _xattn_call(x, _row(p["xa_norm"]), bf(p["xa_wq"]), kmem, vmem, bf(p["xa_wo"]), bsz, seq)
    return x


def kernel(x, mem, ffn1_norm, ffn1_w_gate, ffn1_w_up, ffn1_w_down, mix_norm, w_in, rw_mu, rw_w0, rw_w2, rw_a0,
           rw_a2, rw_g2, rw_k_k, rw_k_a, rw_r_k, rw_gn_w, rw_gn_b, ml_conv_w, ml_conv_b, ml_i_b, ml_f_b, ml_norm,
           w_up_rw, w_up_ml, w_out, xa_norm, mem_norm, xa_wq, xa_wkv, xa_wo, ffn2_norm, ffn2_w_gate, ffn2_w_up,
           ffn2_w_down, final_norm):
    bsz, seq, d = x.shape
    depth = ffn1_norm.shape[0]
    stacked = dict(ffn1_norm=ffn1_norm, ffn1_w_gate=ffn1_w_gate, ffn1_w_up=ffn1_w_up, ffn1_w_down=ffn1_w_down,
                   mix_norm=mix_norm, w_in=w_in, rw_mu=rw_mu, rw_w0=rw_w0, rw_w2=rw_w2, rw_a0=rw_a0, rw_a2=rw_a2,
                   rw_g2=rw_g2, rw_k_k=rw_k_k, rw_k_a=rw_k_a, rw_r_k=rw_r_k, rw_gn_w=rw_gn_w, rw_gn_b=rw_gn_b,
                   ml_conv_w=ml_conv_w, ml_conv_b=ml_conv_b, ml_i_b=ml_i_b, ml_f_b=ml_f_b, ml_norm=ml_norm,
                   w_up_rw=w_up_rw, w_up_ml=w_up_ml, w_out=w_out, xa_norm=xa_norm, mem_norm=mem_norm,
                   xa_wq=xa_wq, xa_wkv=xa_wkv, xa_wo=xa_wo, ffn2_norm=ffn2_norm, ffn2_w_gate=ffn2_w_gate,
                   ffn2_w_up=ffn2_w_up, ffn2_w_down=ffn2_w_down)
    h = x.reshape(bsz * seq, d)
    mem2d = mem.reshape(bsz * mem.shape[1], d)
    bf = lambda w: w.astype(BF16)
    for l in range(depth):
        p = {name: w[l] for name, w in stacked.items()}
        h = _layer(h, mem2d, bsz, seq, p)
        last = l == depth - 1
        h = _ffn_call(h, _row(p["ffn2_norm"]), bf(p["ffn2_w_gate"]), bf(p["ffn2_w_up"]), bf(p["ffn2_w_down"]),
                      _row(final_norm), final=last)
    return h.reshape(bsz, seq, d)
```

```python
import functools
import math

import jax
import jax.numpy as jnp
import numpy as np
from jax import lax
from jax.experimental import pallas as pl
from jax.experimental.pallas import tpu as pltpu

F32 = jnp.float32
BF16 = jnp.bfloat16

D_MODEL = 1024
D_FF = 2816
FFN_HALF = 0.5
NORM_EPS = 1e-5
MEM_LEN = 256

RW_HEADS = 8
RW_HEAD_DIM = 64
RW_WIDTH = 512
DECAY_LORA = 64
AAA_LORA = 64
GATE_LORA = 128
RW_COLS = 3 * RW_WIDTH + DECAY_LORA + AAA_LORA + GATE_LORA
GN_EPS = 64e-5
RW_PAIRS = RW_HEADS // 2
PAIR = 2 * RW_HEAD_DIM

ML_HEADS = 4
ML_HEAD_DIM = 128
ML_WIDTH = 512
ML_CONV = 4
ML_NORM_EPS = 1e-6
ML_QKVO = 4 * ML_WIDTH
IF_PAD = 128

XA_HEADS = 4
XA_HEAD_DIM = 256

CHUNK = 64
LANES = 128
SUBLANES = 8
NEG_BIG = -1e30

VMEM_LIMIT = 56 * 1024 * 1024


def _bdot(a, b):
    return jnp.dot(a.astype(BF16), b.astype(BF16), preferred_element_type=F32)


def _bdot_nt(a, b):
    return lax.dot_general(a.astype(BF16), b.astype(BF16), (((1,), (1,)), ((), ())),
                           preferred_element_type=F32)


def _bdot_tn(a, b):
    return lax.dot_general(a.astype(BF16), b.astype(BF16), (((0,), (0,)), ((), ())),
                           preferred_element_type=F32)


def _split3(x):
    hi = x.astype(BF16)
    r1 = x - hi.astype(F32)
    mid = r1.astype(BF16)
    lo = (r1 - mid.astype(F32)).astype(BF16)
    return hi, mid, lo


def _dot_exact_lhs(c_bf16, x):
    hi, mid, lo = _split3(x)
    f = lambda p: jnp.dot(c_bf16, p, preferred_element_type=F32)
    return f(hi) + f(mid) + f(lo)


def _dot_split2_rhs(x, c_bf16):
    hi = x.astype(BF16)
    lo = (x - hi.astype(F32)).astype(BF16)
    f = lambda p: jnp.dot(p, c_bf16, preferred_element_type=F32)
    return f(hi) + f(lo)


def _per_slab(fn, x, c_bf16):
    n = x.shape[1] // LANES
    return jnp.concatenate([fn(x[:, i * LANES:(i + 1) * LANES], c_bf16) for i in range(n)], axis=1)


def _dot_f32(a, b):
    a_hi = a.astype(BF16)
    a_lo = (a - a_hi.astype(F32)).astype(BF16)
    b_hi = b.astype(BF16)
    b_lo = (b - b_hi.astype(F32)).astype(BF16)
    f = lambda p, q: jnp.dot(p, q, preferred_element_type=F32)
    return f(a_hi, b_hi) + f(a_hi, b_lo) + f(a_lo, b_hi)


def _sigmoid(x):
    return 1.0 / (1.0 + jnp.exp(-x))


def _silu(x):
    return x * _sigmoid(x)


def _rmsnorm(x, g):
    ms = jnp.mean(x * x, axis=-1, keepdims=True)
    return x * lax.rsqrt(ms + NORM_EPS) * g


def _shift_rows(u, carry8, j):
    rolled = pltpu.roll(u, j, axis=0)
    rolled_c = pltpu.roll(carry8, j, axis=0)
    row = lax.broadcasted_iota(jnp.int32, (SUBLANES, u.shape[1]), 0)
    top = jnp.where(row < j, rolled_c, rolled[:SUBLANES])
    return jnp.concatenate([top, rolled[SUBLANES:]], axis=0)


def _const_spec(shape):
    nd = len(shape)
    return pl.BlockSpec(shape, lambda *_: (0,) * nd, pipeline_mode=pl.Buffered(1))


def _params(sem):
    return pltpu.CompilerParams(dimension_semantics=sem, vmem_limit_bytes=VMEM_LIMIT)


FF_CHUNK = 256
FFN_TM = 512


def _ffn_kernel(x_ref, g_ref, wg_ref, wu_ref, wd_ref, fg_ref, o_ref, *, final):
    x = x_ref[...]
    h = _rmsnorm(x, g_ref[...]).astype(BF16)
    acc = jnp.zeros(x.shape, F32)
    for c in range(D_FF // FF_CHUNK):
        sl = slice(c * FF_CHUNK, (c + 1) * FF_CHUNK)
        gate = jnp.dot(h, wg_ref[:, sl], preferred_element_type=F32)
        up = jnp.dot(h, wu_ref[:, sl], preferred_element_type=F32)
        act = (_silu(gate) * up).astype(BF16)
        acc = acc + jnp.dot(act, wd_ref[sl, :], preferred_element_type=F32)
    y = x + FFN_HALF * acc
    if final:
        y = _rmsnorm(y, fg_ref[...])
    o_ref[...] = y


def _ffn_call(x, g, wg, wu, wd, fg, final):
    t, d = x.shape
    tm = min(FFN_TM, t)
    return pl.pallas_call(
        functools.partial(_ffn_kernel, final=final),
        grid=(t // tm,),
        in_specs=[pl.BlockSpec((tm, d), lambda i: (i, 0)),
                  _const_spec((1, d)), _const_spec((d, D_FF)), _const_spec((d, D_FF)),
                  _const_spec((D_FF, d)), _const_spec((1, d))],
        out_specs=pl.BlockSpec((tm, d), lambda i: (i, 0)),
        out_shape=jax.ShapeDtypeStruct((t, d), F32),
        compiler_params=_params(("parallel",)),
        name="ffn_final" if final else "ffn",
    )(x, g, wg, wu, wd, fg)


INPROJ_TM = 512


def _inproj_kernel(x_ref, g_ref, wrw_ref, wml_ref, wif_ref, wgt_ref, mu_ref,
                   zrw_ref, zml_ref, zif_ref, gate_ref, carry_ref):
    @pl.when(pl.program_id(1) == 0)
    def _():
        carry_ref[...] = jnp.zeros(carry_ref.shape, F32)

    h = _rmsnorm(x_ref[...], g_ref[...]).astype(BF16)
    z = jnp.dot(h, wrw_ref[...], preferred_element_type=F32)
    prev = _shift_rows(z, carry_ref[...], 1)
    carry_ref[...] = z[z.shape[0] - SUBLANES:]
    zrw_ref[...] = z + (prev - z) * mu_ref[...]
    zml_ref[...] = jnp.dot(h, wml_ref[...], preferred_element_type=F32)
    zif_ref[...] = jnp.dot(h, wif_ref[...], preferred_element_type=F32)
    gate_ref[...] = _sigmoid(jnp.dot(h, wgt_ref[...], preferred_element_type=F32))


def _inproj_call(x, g, wrw, wml, wif, wgt, mu, bsz, seq):
    t, d = x.shape
    tm = min(INPROJ_TM, seq)
    nt = seq // tm
    row = lambda b, i: (b * nt + i, 0)
    return pl.pallas_call(
        _inproj_kernel,
        grid=(bsz, nt),
        in_specs=[pl.BlockSpec((tm, d), row),
                  _const_spec((1, d)), _const_spec((d, RW_COLS)), _const_spec((d, ML_QKVO)),
                  _const_spec((d, IF_PAD)), _const_spec((d, 2 * d)), _const_spec((1, RW_COLS))],
        out_specs=[pl.BlockSpec((tm, RW_COLS), row), pl.BlockSpec((tm, ML_QKVO), row),
                   pl.BlockSpec((tm, IF_PAD), row), pl.BlockSpec((tm, 2 * d), row)],
        out_shape=[jax.ShapeDtypeStruct((t, RW_COLS), F32), jax.ShapeDtypeStruct((t, ML_QKVO), F32),
                   jax.ShapeDtypeStruct((t, IF_PAD), F32), jax.ShapeDtypeStruct((t, 2 * d), F32)],
        scratch_shapes=[pltpu.VMEM((SUBLANES, RW_COLS), F32)],
        compiler_params=_params(("parallel", "arbitrary")),
        name="inproj",
    )(x, g, wrw, wml, wif, wgt, mu)


RW1_CHUNKS = 4
RW1_GROUP = 16
INV_LEVELS = 6


def _rw_masks():
    t = np.arange(CHUNK)[:, None]
    s = np.tile(np.arange(CHUNK), 2)[None, :]
    levels = []
    for k in range(INV_LEVELS):
        size = 1 << k
        levels.append(((t // (2 * size)) == (s // (2 * size))) & ((t // size) % 2 == 1) & ((s // size) % 2 == 0))
    m = np.stack([s < t, s <= t, s == t] + levels).astype(np.float32)
    tri = (np.arange(CHUNK)[:, None] >= np.arange(CHUNK)[None, :]).astype(np.float32)
    return m, tri


def _group_ones(width, group):
    i = np.arange(width)
    return (i[:, None] // group == i[None, :] // group).astype(np.float32)


def _rw1_kernel(z_ref, w2_ref, a2_ref, g2_ref, vec_ref, gsum_ref, tri_ref, mask_ref,
                g_ref, yi_ref, bonus_ref, gate_ref, m_ref, c_ref):
    z = z_ref[...]
    rows = z.shape[0]
    r = z[:, 0:RW_WIDTH]
    k = z[:, RW_WIDTH:2 * RW_WIDTH]
    v = z[:, 2 * RW_WIDTH:3 * RW_WIDTH]
    lora_in = z[:, 3 * RW_WIDTH:3 * RW_WIDTH + DECAY_LORA + AAA_LORA]
    g_d = z[:, 3 * RW_WIDTH + DECAY_LORA + AAA_LORA:]
    w0 = vec_ref[0:1, :]
    a0 = vec_ref[1:2, :]
    k_k = vec_ref[2:3, :]
    k_a = vec_ref[3:4, :]
    r_k = vec_ref[4:5, :]

    w_raw = w0 + _bdot(jnp.tanh(lora_in), w2_ref[...])
    logw = -math.exp(-0.5) * _sigmoid(w_raw)
    a = _sigmoid(a0 + _bdot(lora_in, a2_ref[...]))
    gate_ref[...] = _bdot(_sigmoid(g_d), g2_ref[...])

    gsum = gsum_ref[...]
    kk = k * k_k
    kk = kk / jnp.maximum(jnp.sqrt(_per_slab(_bdot, kk * kk, gsum)), 1e-12)
    k_mod = k * (1.0 + (a - 1.0) * k_a)
    b = kk * a
    bonus_ref[...] = _per_slab(_dot_split2_rhs, r * k_mod * r_k, gsum) * v

    lane = lax.broadcasted_iota(jnp.int32, (CHUNK, PAIR), 1)
    head0 = lane < RW_HEAD_DIM
    strict = mask_ref[0]
    incl = mask_ref[1]
    eye = mask_ref[2]
    tri = tri_ref[...]

    def stack(x):
        return jnp.concatenate([jnp.where(head0, x, 0.0), jnp.where(head0, 0.0, x)], axis=0).astype(BF16)

    ops = {}
    for c in range(rows // CHUNK):
        rs = slice(c * CHUNK, (c + 1) * CHUNK)
        lw = logw[rs]
        cum = _dot_exact_lhs(tri, lw)
        cum_last = cum[CHUNK - 1:CHUNK]
        w_incl = jnp.exp(cum)
        w_inv = jnp.exp(-cum)
        w_prev = jnp.exp(cum - lw)
        w_end = jnp.exp(cum_last - cum)
        w_last = jnp.exp(cum_last)
        full = dict(at=-kk[rs] * w_prev, rt=r[rs] * w_incl, bt=b[rs] * w_inv, kt=k_mod[rs] * w_inv,
                    be=b[rs] * w_end, ke=k_mod[rs] * w_end, v2=v[rs])
        for p in range(RW_PAIRS):
            ls = slice(p * PAIR, (p + 1) * PAIR)
            item = {name: val[:, ls] for name, val in full.items()}
            item["w_last"] = w_last[:, ls]
            ops[(c, p)] = item

    keys = list(ops)
    for g0 in range(0, len(keys), RW1_GROUP):
        group = keys[g0:g0 + RW1_GROUP]
        n_ab, a_ak, a_rb, a_rk, t_inv, sv = {}, {}, {}, {}, {}, {}
        for key in group:
            o = ops[key]
            prod = _bdot_nt(jnp.concatenate([o["at"], o["rt"]], axis=0),
                            jnp.concatenate([stack(o["bt"]), stack(o["kt"])], axis=0))
            n_ab[key] = prod[:CHUNK, :PAIR] * strict
            a_ak[key] = prod[:CHUNK, PAIR:] * strict
            a_rb[key] = prod[CHUNK:, :PAIR] * incl
            a_rk[key] = prod[CHUNK:, PAIR:] * incl
            sv[key] = stack(o["v2"])
            t_inv[key] = eye + n_ab[key] * mask_ref[3]
        for lvl in range(1, INV_LEVELS):
            half = {key: _bdot(t_inv[key], stack(n_ab[key] * mask_ref[3 + lvl])) for key in group}
            t_inv = {key: t_inv[key] + _bdot(half[key], stack(t_inv[key])) for key in group}
        akv = {key: _bdot(a_ak[key], sv[key]) for key in group}
        pq = {key: _bdot(t_inv[key], jnp.concatenate([stack(ops[key]["at"]), stack(akv[key])], axis=1))
              for key in group}
        gy = {key: _bdot(a_rb[key], jnp.concatenate([stack(pq[key][:, :PAIR]), stack(pq[key][:, PAIR:])], axis=1))
              for key in group}
        ark_v = {key: _bdot(a_rk[key], sv[key]) for key in group}
        mc = {key: _bdot_tn(ops[key]["be"], pq[key]) for key in group}
        ke_v = {key: _bdot_tn(ops[key]["ke"], ops[key]["v2"]) for key in group}
        for key in group:
            c, p = key
            rs = slice(c * CHUNK, (c + 1) * CHUNK)
            ls = slice(p * PAIR, (p + 1) * PAIR)
            g_ref[rs, ls] = ops[key]["rt"] + gy[key][:, :PAIR]
            yi_ref[rs, ls] = gy[key][:, PAIR:] + ark_v[key]
            cv = mc[key][:, PAIR:] + ke_v[key]
            m_ref[c, p] = eye * ops[key]["w_last"] + jnp.where(head0, mc[key][:CHUNK, :PAIR], mc[key][CHUNK:, :PAIR])
            c_ref[c, p] = jnp.where(head0, cv[:CHUNK], cv[CHUNK:])


def _rw1_call(zrw, w2p, a2p, g2, vec, gsum, tri, masks):
    t = zrw.shape[0]
    rows = RW1_CHUNKS * CHUNK
    nchunks = t // CHUNK
    tok = lambda i: (i, 0)
    blk = lambda i: (i, 0, 0, 0)
    tok_spec = pl.BlockSpec((rows, RW_WIDTH), tok)
    mc_spec = pl.BlockSpec((RW1_CHUNKS, RW_PAIRS, CHUNK, PAIR), blk)
    tok_shape = jax.ShapeDtypeStruct((t, RW_WIDTH), F32)
    mc_shape = jax.ShapeDtypeStruct((nchunks, RW_PAIRS, CHUNK, PAIR), F32)
    return pl.pallas_call(
        _rw1_kernel,
        grid=(t // rows,),
        in_specs=[pl.BlockSpec((rows, RW_COLS), tok),
                  _const_spec(w2p.shape), _const_spec(a2p.shape), _const_spec(g2.shape),
                  _const_spec(vec.shape), _const_spec(gsum.shape), _const_spec(tri.shape),
                  _const_spec(masks.shape)],
        out_specs=[tok_spec, tok_spec, tok_spec, tok_spec, mc_spec, mc_spec],
        out_shape=[tok_shape, tok_shape, tok_shape, tok_shape, mc_shape, mc_shape],
        compiler_params=_params(("parallel",)),
        name="rwkv_chunk_local",
    )(zrw, w2p, a2p, g2, vec, gsum, tri, masks)


RW2_CHUNKS = 8


def _rw2_kernel(g_ref, yi_ref, bonus_ref, gate_ref, m_ref, c_ref, gmean_ref, gn_ref, y_ref, h_ref):
    @pl.when(pl.program_id(1) == 0)
    def _():
        h_ref[...] = jnp.zeros(h_ref.shape, F32)

    lane = lax.broadcasted_iota(jnp.int32, (CHUNK, PAIR), 1)
    head0 = lane < RW_HEAD_DIM

    def block_diag(x):
        return jnp.concatenate([jnp.where(head0, x, 0.0), jnp.where(head0, 0.0, x)], axis=0)

    ys = []
    for c in range(RW2_CHUNKS):
        rs = slice(c * CHUNK, (c + 1) * CHUNK)
        cols = []
        for p in range(RW_PAIRS):
            ls = slice(p * PAIR, (p + 1) * PAIR)
            h = h_ref[p]
            cols.append(_bdot(g_ref[rs, ls], h) + yi_ref[rs, ls])
            h_ref[p] = _dot_f32(block_diag(m_ref[c, p]), h) + block_diag(c_ref[c, p])
        ys.append(jnp.concatenate(cols, axis=1))
    y = jnp.concatenate(ys, axis=0)
    gmean = gmean_ref[...]
    mu = _per_slab(_dot_split2_rhs, y, gmean)
    d = y - mu
    var = _per_slab(_bdot, d * d, gmean)
    y = d * lax.rsqrt(var + GN_EPS) * gn_ref[0:1, :] + gn_ref[1:2, :]
    y_ref[...] = ((y + bonus_ref[...]) * gate_ref[...]).astype(BF16)


def _rw2_call(g, yi, bonus, gate, m, c, gmean, gn, bsz, seq):
    t = g.shape[0]
    rows = RW2_CHUNKS * CHUNK
    nt = seq // rows
    tok = lambda b, i: (b * nt + i, 0)
    blk = lambda b, i: (b * nt + i, 0, 0, 0)
    tok_spec = pl.BlockSpec((rows, RW_WIDTH), tok)
    mc_spec = pl.BlockSpec((RW2_CHUNKS, RW_PAIRS, CHUNK, PAIR), blk)
    return pl.pallas_call(
        _rw2_kernel,
        grid=(bsz, nt),
        in_specs=[tok_spec, tok_spec, tok_spec, tok_spec, mc_spec, mc_spec,
                  _const_spec(gmean.shape), _const_spec(gn.shape)],
        out_specs=tok_spec,
        out_shape=jax.ShapeDtypeStruct((t, RW_WIDTH), BF16),
        scratch_shapes=[pltpu.VMEM((RW_PAIRS, PAIR, PAIR), F32)],
        compiler_params=_params(("parallel", "arbitrary")),
        name="rwkv_scan",
    )(g, yi, bonus, gate, m, c, gmean, gn)


ML_CHUNK = 128
ML_STEP_CHUNKS = 2


def _log_sigmoid(x):
    return jnp.minimum(x, 0.0) - jnp.log(1.0 + jnp.exp(-jnp.abs(x)))


def _ml_kernel(z_ref, zif_ref, convw_ref, convb_ref, ifb_ref, norm_ref, tri_ref,
               y_ref, carry_ref, state_ref, m_ref):
    @pl.when(pl.program_id(1) == 0)
    def _():
        carry_ref[...] = jnp.zeros(carry_ref.shape, F32)
        state_ref[...] = jnp.zeros(state_ref.shape, F32)
        m_ref[...] = jnp.zeros(m_ref.shape, F32)

    rows = z_ref.shape[0]
    u = z_ref[:, 0:2 * ML_WIDTH]
    carry = carry_ref[...]
    conv = convb_ref[...] + convw_ref[0:1, :] * u
    for j in range(1, ML_CONV):
        conv = conv + convw_ref[j:j + 1, :] * _shift_rows(u, carry, j)
    carry_ref[...] = u[rows - SUBLANES:]
    qk = _silu(conv)
    q_all = qk[:, 0:ML_WIDTH] * (ML_HEAD_DIM ** -0.5)
    k_all = qk[:, ML_WIDTH:]
    v_all = z_ref[:, 2 * ML_WIDTH:3 * ML_WIDTH]
    o_all = z_ref[:, 3 * ML_WIDTH:4 * ML_WIDTH]

    pre = zif_ref[...] + ifb_ref[...]
    logf = _log_sigmoid(pre)
    tri = tri_ref[...]
    lane = lax.broadcasted_iota(jnp.int32, (ML_CHUNK, LANES), 1)
    tq = lax.broadcasted_iota(jnp.int32, (ML_CHUNK, ML_CHUNK), 0)
    tk = lax.broadcasted_iota(jnp.int32, (ML_CHUNK, ML_CHUNK), 1)
    causal = tk <= tq
    ones = jnp.ones((ML_CHUNK, ML_HEAD_DIM), F32)
    dense = (ML_CHUNK, ML_CHUNK)

    for c in range(rows // ML_CHUNK):
        rs = slice(c * ML_CHUNK, (c + 1) * ML_CHUNK)
        bcum = _dot_exact_lhs(tri, logf[rs])
        z = jnp.where(lane < ML_HEADS, pre[rs], bcum)
        z_t = z.T
        outs = []
        for h in range(ML_HEADS):
            ls = slice(h * ML_HEAD_DIM, (h + 1) * ML_HEAD_DIM)
            b_col = jnp.broadcast_to(z[:, ML_HEADS + h:ML_HEADS + h + 1], dense)
            ib_col = jnp.broadcast_to(z[:, h:h + 1] - z[:, ML_HEADS + h:ML_HEADS + h + 1], dense)
            ib_row = z_t[h:h + 1, :] - z_t[ML_HEADS + h:ML_HEADS + h + 1, :]
            b_end = b_col[ML_CHUNK - 1:ML_CHUNK]
            m_prev = m_ref[h]
            q = q_all[rs, ls]
            k = k_all[rs, ls]
            v_aug = jnp.concatenate([v_all[rs, ls], ones], axis=1)
            state = state_ref[h]

            log_d = jnp.where(causal, b_col + ib_row, NEG_BIG)
            log_inter = b_col + m_prev
            m_row = jnp.maximum(log_inter, jnp.max(log_d, axis=-1, keepdims=True))
            p = jnp.exp(log_d - m_row) * _bdot_nt(q, k)
            w_inter = jnp.exp(log_inter - m_row)
            intra = _bdot(p, v_aug)
            inter = _bdot(q, state)
            num = intra[:, :ML_HEAD_DIM] + w_inter * inter[:, :ML_HEAD_DIM]
            den = intra[:, ML_HEAD_DIM:] + w_inter * inter[:, ML_HEAD_DIM:]
            hh = num / jnp.maximum(jnp.abs(den), jnp.exp(-m_row))

            log_w_end = ib_col + b_end
            m_chunk = jnp.max(log_w_end, axis=0, keepdims=True)
            w_end = jnp.exp(log_w_end - m_chunk)
            m_new = jnp.maximum(b_end + m_prev, m_chunk)
            dec = jnp.exp(b_end + m_prev - m_new)
            sc = jnp.exp(m_chunk - m_new)
            kv = _bdot_tn(k, jnp.concatenate([w_end * v_all[rs, ls], w_end], axis=1))
            state_ref[h] = jnp.concatenate([dec, dec], axis=1) * state + jnp.concatenate([sc, sc], axis=1) * kv
            m_ref[h] = m_new

            mu = jnp.mean(hh, axis=-1, keepdims=True)
            d = hh - mu
            var = jnp.mean(d * d, axis=-1, keepdims=True)
            hn = d * lax.rsqrt(var + ML_NORM_EPS)
            outs.append(hn * norm_ref[:, ls] * _sigmoid(o_all[rs, ls]))
        y_ref[rs, :] = jnp.concatenate(outs, axis=1).astype(BF16)


def _ml_call(zml, zif, convw, convb, ifb, norm, tri, bsz, seq):
    t = zml.shape[0]
    rows = ML_STEP_CHUNKS * ML_CHUNK
    nt = seq // rows
    tok = lambda b, i: (b * nt + i, 0)
    return pl.pallas_call(
        _ml_kernel,
        grid=(bsz, nt),
        in_specs=[pl.BlockSpec((rows, ML_QKVO), tok), pl.BlockSpec((rows, IF_PAD), tok),
                  _const_spec(convw.shape), _const_spec(convb.shape), _const_spec(ifb.shape),
                  _const_spec(norm.shape), _const_spec(tri.shape)],
        out_specs=pl.BlockSpec((rows, ML_WIDTH), tok),
        out_shape=jax.ShapeDtypeStruct((t, ML_WIDTH), BF16),
        scratch_shapes=[pltpu.VMEM((SUBLANES, 2 * ML_WIDTH), F32),
                        pltpu.VMEM((ML_HEADS, ML_HEAD_DIM, 2 * ML_HEAD_DIM), F32),
                        pltpu.VMEM((ML_HEADS, 1, LANES), F32)],
        compiler_params=_params(("parallel", "arbitrary")),
        name="mlstm",
    )(zml, zif, convw, convb, ifb, norm, tri)


MERGE_TM = 512


def _merge_kernel(x_ref, yrw_ref, yml_ref, gate_ref, wur_ref, wum_ref, wo_ref, o_ref):
    d = x_ref.shape[1]
    up_rw = jnp.dot(yrw_ref[...], wur_ref[...], preferred_element_type=F32)
    up_ml = jnp.dot(yml_ref[...], wum_ref[...], preferred_element_type=F32)
    merged = gate_ref[:, 0:d] * up_rw + gate_ref[:, d:] * up_ml
    o_ref[...] = x_ref[...] + _bdot(merged, wo_ref[...])


def _merge_call(x, yrw, yml, gates, wur, wum, wo):
    t, d = x.shape
    tm = min(MERGE_TM, t)
    row = lambda i: (i, 0)
    return pl.pallas_call(
        _merge_kernel,
        grid=(t // tm,),
        in_specs=[pl.BlockSpec((tm, d), row), pl.BlockSpec((tm, RW_WIDTH), row),
                  pl.BlockSpec((tm, ML_WIDTH), row), pl.BlockSpec((tm, 2 * d), row),
                  _const_spec(wur.shape), _const_spec(wum.shape), _const_spec(wo.shape)],
        out_specs=pl.BlockSpec((tm, d), row),
        out_shape=jax.ShapeDtypeStruct((t, d), F32),
        compiler_params=_params(("parallel",)),
        name="merge",
    )(x, yrw, yml, gates, wur, wum, wo)


def _memkv_kernel(mem_ref, g_ref, wkv_ref, k_ref, v_ref):
    d = mem_ref.shape[1]
    hm = _rmsnorm(mem_ref[...], g_ref[...])
    kv = _bdot(hm, wkv_ref[...])
    k_ref[...] = kv[:, 0:d].astype(BF16)
    v_ref[...] = kv[:, d:].astype(BF16)


def _memkv_call(mem2d, g, wkv):
    n, d = mem2d.shape
    row = lambda i: (i, 0)
    return pl.pallas_call(
        _memkv_kernel,
        grid=(n // MEM_LEN,),
        in_specs=[pl.BlockSpec((MEM_LEN, d), row), _const_spec(g.shape), _const_spec(wkv.shape)],
        out_specs=[pl.BlockSpec((MEM_LEN, d), row), pl.BlockSpec((MEM_LEN, d), row)],
        out_shape=[jax.ShapeDtypeStruct((n, d), BF16), jax.ShapeDtypeStruct((n, d), BF16)],
        compiler_params=_params(("parallel",)),
        name="mem_kv",
    )(mem2d, g, wkv)


XA_TM = 512


def _xattn_kernel(x_ref, g_ref, wq_ref, k_ref, v_ref, wo_ref, o_ref):
    x = x_ref[...]
    hq = _rmsnorm(x, g_ref[...])
    q = _bdot(hq, wq_ref[...]) * (XA_HEAD_DIM ** -0.5)
    outs = []
    for h in range(XA_HEADS):
        ls = slice(h * XA_HEAD_DIM, (h + 1) * XA_HEAD_DIM)
        s = _bdot_nt(q[:, ls], k_ref[:, ls])
        s = s - jnp.max(s, axis=-1, keepdims=True)
        e = jnp.exp(s)
        p = e / jnp.sum(e, axis=-1, keepdims=True)
        outs.append(_bdot(p, v_ref[:, ls]))
    o = jnp.concatenate(outs, axis=1)
    o_ref[...] = x + _bdot(o, wo_ref[...])


def _xattn_call(x, g, wq, kmem, vmem, wo, bsz, seq):
    t, d = x.shape
    tm = min(XA_TM, seq)
    nt = seq // tm
    row = lambda b, i: (b * nt + i, 0)
    mem = lambda b, i: (b, 0)
    return pl.pallas_call(
        _xattn_kernel,
        grid=(bsz, nt),
        in_specs=[pl.BlockSpec((tm, d), row), _const_spec(g.shape), _const_spec(wq.shape),
                  pl.BlockSpec((MEM_LEN, d), mem), pl.BlockSpec((MEM_LEN, d), mem),
                  _const_spec(wo.shape)],
        out_specs=pl.BlockSpec((tm, d), row),
        out_shape=jax.ShapeDtypeStruct((t, d), F32),
        compiler_params=_params(("parallel", "parallel")),
        name="mem_xattn",
    )(x, g, wq, kmem, vmem, wo)


def _row(v):
    return v.reshape(1, -1).astype(F32)


def _layer(x, mem2d, bsz, seq, p):
    d = D_MODEL
    bf = lambda w: w.astype(BF16)
    masks_np, tri_np = _rw_masks()
    masks = jnp.asarray(masks_np)
    tri = jnp.asarray(tri_np, dtype=BF16)
    gsum = jnp.asarray(_group_ones(PAIR, RW_HEAD_DIM), dtype=BF16)
    gmean = jnp.asarray(_group_ones(PAIR, RW_HEAD_DIM) / RW_HEAD_DIM, dtype=BF16)

    x = _ffn_call(x, _row(p["ffn1_norm"]), bf(p["ffn1_w_gate"]), bf(p["ffn1_w_up"]), bf(p["ffn1_w_down"]),
                  _row(p["ffn1_norm"]), final=False)

    w_in = p["w_in"]
    c0 = RW_COLS
    c1 = c0 + ML_QKVO
    c2 = c1 + 2 * ML_HEADS
    w_if = jnp.pad(w_in[:, c1:c2], ((0, 0), (0, IF_PAD - 2 * ML_HEADS)))
    zrw, zml, zif, gates = _inproj_call(x, _row(p["mix_norm"]), bf(w_in[:, :c0]), bf(w_in[:, c0:c1]), bf(w_if),
                                        bf(w_in[:, c2:]), _row(p["rw_mu"]), bsz, seq)

    zpad = jnp.zeros((DECAY_LORA, RW_WIDTH), F32)
    w2p = bf(jnp.concatenate([p["rw_w2"], zpad], axis=0))
    a2p = bf(jnp.concatenate([zpad, p["rw_a2"]], axis=0))
    vec = jnp.stack([p["rw_w0"], p["rw_a0"], p["rw_k_k"], p["rw_k_a"], p["rw_r_k"].reshape(-1),
                     jnp.zeros_like(p["rw_w0"]), jnp.zeros_like(p["rw_w0"]), jnp.zeros_like(p["rw_w0"])]).astype(F32)
    g, yi, bonus, gate, m, c = _rw1_call(zrw, w2p, a2p, bf(p["rw_g2"]), vec, gsum, tri, masks)
    gn = jnp.concatenate([jnp.stack([p["rw_gn_w"], p["rw_gn_b"]]).astype(F32),
                          jnp.zeros((SUBLANES - 2, RW_WIDTH), F32)], axis=0)
    y_rw = _rw2_call(g, yi, bonus, gate, m, c, gmean, gn, bsz, seq)

    ifb = jnp.pad(jnp.concatenate([p["ml_i_b"], p["ml_f_b"]]).astype(F32), (0, IF_PAD - 2 * ML_HEADS)).reshape(1, -1)
    convw = jnp.concatenate([p["ml_conv_w"].astype(F32), jnp.zeros((SUBLANES - ML_CONV, 2 * ML_WIDTH), F32)], axis=0)
    tri_ml = jnp.asarray(np.tril(np.ones((ML_CHUNK, ML_CHUNK), np.float32)), dtype=BF16)
    y_ml = _ml_call(zml, zif, convw, _row(p["ml_conv_b"]), ifb, _row(p["ml_norm"]), tri_ml, bsz, seq)

    x = _merge_call(x, y_rw, y_ml, gates, bf(p["w_up_rw"]), bf(p["w_up_ml"]), bf(p["w_out"]))

    kmem, vmem = _memkv_call(mem2d, _row(p["mem_norm"]), bf(p["xa_wkv"]))
    x = _xattn_call(x, _row(p["xa_norm"]), bf(p["xa_wq"]), kmem, vmem, bf(p["xa_wo"]), bsz, seq)
    return x


def kernel(x, mem, ffn1_norm, ffn1_w_gate, ffn1_w_up, ffn1_w_down, mix_norm, w_in, rw_mu, rw_w0, rw_w2, rw_a0,
           rw_a2, rw_g2, rw_k_k, rw_k_a, rw_r_k, rw_gn_w, rw_gn_b, ml_conv_w, ml_conv_b, ml_i_b, ml_f_b, ml_norm,
           w_up_rw, w_up_ml, w_out, xa_norm, mem_norm, xa_wq, xa_wkv, xa_wo, ffn2_norm, ffn2_w_gate, ffn2_w_up,
           ffn2_w_down, final_norm):
    bsz, seq, d = x.shape
    depth = ffn1_norm.shape[0]
    stacked = dict(ffn1_norm=ffn1_norm, ffn1_w_gate=ffn1_w_gate, ffn1_w_up=ffn1_w_up, ffn1_w_down=ffn1_w_down,
                   mix_norm=mix_norm, w_in=w_in, rw_mu=rw_mu, rw_w0=rw_w0, rw_w2=rw_w2, rw_a0=rw_a0, rw_a2=rw_a2,
                   rw_g2=rw_g2, rw_k_k=rw_k_k, rw_k_a=rw_k_a, rw_r_k=rw_r_k, rw_gn_w=rw_gn_w, rw_gn_b=rw_gn_b,
                   ml_conv_w=ml_conv_w, ml_conv_b=ml_conv_b, ml_i_b=ml_i_b, ml_f_b=ml_f_b, ml_norm=ml_norm,
                   w_up_rw=w_up_rw, w_up_ml=w_up_ml, w_out=w_out, xa_norm=xa_norm, mem_norm=mem_norm,
                   xa_wq=xa_wq, xa_wkv=xa_wkv, xa_wo=xa_wo, ffn2_norm=ffn2_norm, ffn2_w_gate=ffn2_w_gate,
                   ffn2_w_up=ffn2_w_up, ffn2_w_down=ffn2_w_down)
    h = x.reshape(bsz * seq, d)
    mem2d = mem.reshape(bsz * mem.shape[1], d)
    bf = lambda w: w.astype(BF16)
    for l in range(depth):
        p = {name: w[l] for name, w in stacked.items()}
        h = _layer(h, mem2d, bsz, seq, p)
        last = l == depth - 1
        h = _ffn_call(h, _row(p["ffn2_norm"]), bf(p["ffn2_w_gate"]), bf(p["ffn2_w_up"]), bf(p["ffn2_w_down"]),
                      _row(final_norm), final=last)
    return h.reshape(bsz, seq, d)
```

```python
import functools
import math

import jax
import jax.numpy as jnp
import numpy as np
from jax import lax
from jax.experimental import pallas as pl
from jax.experimental.pallas import tpu as pltpu

F32 = jnp.float32
BF16 = jnp.bfloat16

D_MODEL = 1024
D_FF = 2816
FFN_HALF = 0.5
NORM_EPS = 1e-5
MEM_LEN = 256

RW_HEADS = 8
RW_HEAD_DIM = 64
RW_WIDTH = 512
DECAY_LORA = 64
AAA_LORA = 64
GATE_LORA = 128
RW_COLS = 3 * RW_WIDTH + DECAY_LORA + AAA_LORA + GATE_LORA
GN_EPS = 64e-5
RW_PAIRS = RW_HEADS // 2
PAIR = 2 * RW_HEAD_DIM

ML_HEADS = 4
ML_HEAD_DIM = 128
ML_WIDTH = 512
ML_CONV = 4
ML_NORM_EPS = 1e-6
ML_QKVO = 4 * ML_WIDTH
IF_PAD = 128

XA_HEADS = 4
XA_HEAD_DIM = 256

CHUNK = 64
LANES = 128
SUBLANES = 8
NEG_BIG = -1e30

VMEM_LIMIT = 56 * 1024 * 1024


def _bdot(a, b):
    return jnp.dot(a.astype(BF16), b.astype(BF16), preferred_element_type=F32)


def _bdot_nt(a, b):
    return lax.dot_general(a.astype(BF16), b.astype(BF16), (((1,), (1,)), ((), ())),
                           preferred_element_type=F32)


def _bdot_tn(a, b):
    return lax.dot_general(a.astype(BF16), b.astype(BF16), (((0,), (0,)), ((), ())),
                           preferred_element_type=F32)


def _split3(x):
    hi = x.astype(BF16)
    r1 = x - hi.astype(F32)
    mid = r1.astype(BF16)
    lo = (r1 - mid.astype(F32)).astype(BF16)
    return hi, mid, lo


def _dot_exact_lhs(c_bf16, x):
    hi, mid, lo = _split3(x)
    f = lambda p: jnp.dot(c_bf16, p, preferred_element_type=F32)
    return f(hi) + f(mid) + f(lo)


def _dot_split2_rhs(x, c_bf16):
    hi = x.astype(BF16)
    lo = (x - hi.astype(F32)).astype(BF16)
    f = lambda p: jnp.dot(p, c_bf16, preferred_element_type=F32)
    return f(hi) + f(lo)


def _per_slab(fn, x, c_bf16):
    n = x.shape[1] // LANES
    return jnp.concatenate([fn(x[:, i * LANES:(i + 1) * LANES], c_bf16) for i in range(n)], axis=1)


def _dot_f32(a, b):
    a_hi = a.astype(BF16)
    a_lo = (a - a_hi.astype(F32)).astype(BF16)
    b_hi = b.astype(BF16)
    b_lo = (b - b_hi.astype(F32)).astype(BF16)
    f = lambda p, q: jnp.dot(p, q, preferred_element_type=F32)
    return f(a_hi, b_hi) + f(a_hi, b_lo) + f(a_lo, b_hi)


def _sigmoid(x):
    return 1.0 / (1.0 + jnp.exp(-x))


def _silu(x):
    return x * _sigmoid(x)


def _rmsnorm(x, g):
    ms = jnp.mean(x * x, axis=-1, keepdims=True)
    return x * lax.rsqrt(ms + NORM_EPS) * g


def _shift_rows(u, carry8, j):
    rolled = pltpu.roll(u, j, axis=0)
    rolled_c = pltpu.roll(carry8, j, axis=0)
    row = lax.broadcasted_iota(jnp.int32, (SUBLANES, u.shape[1]), 0)
    top = jnp.where(row < j, rolled_c, rolled[:SUBLANES])
    return jnp.concatenate([top, rolled[SUBLANES:]], axis=0)


def _const_spec(shape):
    nd = len(shape)
    return pl.BlockSpec(shape, lambda *_: (0,) * nd, pipeline_mode=pl.Buffered(1))


def _params(sem):
    return pltpu.CompilerParams(dimension_semantics=sem, vmem_limit_bytes=VMEM_LIMIT)


FF_CHUNK = 256
FFN_TM = 512


def _ffn_kernel(x_ref, g_ref, wg_ref, wu_ref, wd_ref, fg_ref, o_ref, *, final):
    x = x_ref[...]
    h = _rmsnorm(x, g_ref[...]).astype(BF16)
    acc = jnp.zeros(x.shape, F32)
    for c in range(D_FF // FF_CHUNK):
        sl = slice(c * FF_CHUNK, (c + 1) * FF_CHUNK)
        gate = jnp.dot(h, wg_ref[:, sl], preferred_element_type=F32)
        up = jnp.dot(h, wu_ref[:, sl], preferred_element_type=F32)
        act = (_silu(gate) * up).astype(BF16)
        acc = acc + jnp.dot(act, wd_ref[sl, :], preferred_element_type=F32)
    y = x + FFN_HALF * acc
    if final:
        y = _rmsnorm(y, fg_ref[...])
    o_ref[...] = y


def _ffn_call(x, g, wg, wu, wd, fg, final):
    t, d = x.shape
    tm = min(FFN_TM, t)
    return pl.pallas_call(
        functools.partial(_ffn_kernel, final=final),
        grid=(t // tm,),
        in_specs=[pl.BlockSpec((tm, d), lambda i: (i, 0)),
                  _const_spec((1, d)), _const_spec((d, D_FF)), _const_spec((d, D_FF)),
                  _const_spec((D_FF, d)), _const_spec((1, d))],
        out_specs=pl.BlockSpec((tm, d), lambda i: (i, 0)),
        out_shape=jax.ShapeDtypeStruct((t, d), F32),
        compiler_params=_params(("parallel",)),
        name="ffn_final" if final else "ffn",
    )(x, g, wg, wu, wd, fg)


INPROJ_TM = 512
INPROJ_COLS = 256
INPROJ_LAG = 1


def _inproj_kernel(x_ref, g_ref, wrw_ref, wml_ref, wif_ref, mu_ref, convw_ref, convb_ref,
                   zrw_ref, zml_ref, zif_ref, carry_rw_ref, carry_ml_ref):
    @pl.when(pl.program_id(1) == 0)
    def _():
        carry_rw_ref[...] = jnp.zeros(carry_rw_ref.shape, F32)
        carry_ml_ref[...] = jnp.zeros(carry_ml_ref.shape, F32)

    rows = x_ref.shape[0]
    h = _rmsnorm(x_ref[...], g_ref[...]).astype(BF16)

    def conv_stage(cols, scale):
        def finish(u):
            carry = carry_ml_ref[:, cols]
            w0, w1, w2, w3 = (convw_ref[j:j + 1, cols] for j in range(ML_CONV))
            u1 = _shift_rows(u, carry, 1)
            pair = w2 * u + w3 * u1
            pair_carry = w2 * carry + w3 * pltpu.roll(carry, 1, axis=0)
            conv = convb_ref[:, cols] + w0 * u + w1 * u1 + _shift_rows(pair, pair_carry, 2)
            carry_ml_ref[:, cols] = u[rows - SUBLANES:]
            zml_ref[:, cols] = _silu(conv) * scale if scale != 1.0 else _silu(conv)
        return wml_ref, cols, finish

    def plain_stage(cols, fn):
        def finish(u):
            zml_ref[:, cols] = fn(u)
        return wml_ref, cols, finish

    def lerp_stage(cols):
        def finish(z):
            prev = _shift_rows(z, carry_rw_ref[:, cols], 1)
            carry_rw_ref[:, cols] = z[rows - SUBLANES:]
            zrw_ref[:, cols] = z + (prev - z) * mu_ref[:, cols]
        return wrw_ref, cols, finish

    blk = lambda c: slice(c * INPROJ_COLS, (c + 1) * INPROJ_COLS)
    n_head_blocks = ML_WIDTH // INPROJ_COLS
    heavy, light = [], []
    for c in range(ML_QKVO // INPROJ_COLS):
        group = c // n_head_blocks
        if group < 2:
            heavy.append(conv_stage(blk(c), ML_HEAD_DIM ** -0.5 if group == 0 else 1.0))
        else:
            light.append(plain_stage(blk(c), _sigmoid if group == 3 else (lambda u: u)))
    light += [lerp_stage(blk(c)) for c in range(RW_COLS // INPROJ_COLS)]

    per_heavy = -(-len(light) // len(heavy))
    stages = []
    for i, stage in enumerate(heavy):
        stages.append(stage)
        stages += light[i * per_heavy:(i + 1) * per_heavy]
    pending = []
    for w_ref, cols, finish in stages:
        acc = jnp.dot(h, w_ref[:, cols], preferred_element_type=F32)
        if len(pending) == INPROJ_LAG:
            fn, val = pending.pop(0)
            fn(val)
        pending.append((finish, acc))
    zif_ref[...] = jnp.dot(h, wif_ref[...], preferred_element_type=F32)
    for fn, val in pending:
        fn(val)


def _inproj_call(x, g, wrw, wml, wif, mu, convw, convb, bsz, seq):
    t, d = x.shape
    tm = min(INPROJ_TM, seq)
    nt = seq // tm
    row = lambda b, i: (b * nt + i, 0)
    return pl.pallas_call(
        _inproj_kernel,
        grid=(bsz, nt),
        in_specs=[pl.BlockSpec((tm, d), row),
                  _const_spec((1, d)), _const_spec((d, RW_COLS)), _const_spec((d, ML_QKVO)),
                  _const_spec((d, IF_PAD)), _const_spec((1, RW_COLS)),
                  _const_spec(convw.shape), _const_spec(convb.shape)],
        out_specs=[pl.BlockSpec((tm, RW_COLS), row), pl.BlockSpec((tm, ML_QKVO), row),
                   pl.BlockSpec((tm, IF_PAD), row)],
        out_shape=[jax.ShapeDtypeStruct((t, RW_COLS), F32), jax.ShapeDtypeStruct((t, ML_QKVO), F32),
                   jax.ShapeDtypeStruct((t, IF_PAD), F32)],
        scratch_shapes=[pltpu.VMEM((SUBLANES, RW_COLS), F32), pltpu.VMEM((SUBLANES, 2 * ML_WIDTH), F32)],
        compiler_params=_params(("parallel", "arbitrary")),
        name="inproj",
    )(x, g, wrw, wml, wif, mu, convw, convb)


RW1_CHUNKS = 4
RW1_GROUP = 16
INV_LEVELS = 6


def _rw_masks():
    t = np.arange(CHUNK)[:, None]
    s = np.tile(np.arange(CHUNK), 2)[None, :]
    levels = []
    for k in range(INV_LEVELS):
        size = 1 << k
        levels.append(((t // (2 * size)) == (s // (2 * size))) & ((t // size) % 2 == 1) & ((s // size) % 2 == 0))
    m = np.stack([s < t, s <= t, s == t] + levels).astype(np.float32)
    tri = (np.arange(CHUNK)[:, None] >= np.arange(CHUNK)[None, :]).astype(np.float32)
    return m, tri


def _group_ones(width, group):
    i = np.arange(width)
    return (i[:, None] // group == i[None, :] // group).astype(np.float32)


def _rw1_kernel(z_ref, w2_ref, a2_ref, g2_ref, vec_ref, gsum_ref, tri_ref, mask_ref,
                g_ref, yi_ref, bonus_ref, gate_ref, m_ref, c_ref):
    z = z_ref[...]
    rows = z.shape[0]
    r = z[:, 0:RW_WIDTH]
    k = z[:, RW_WIDTH:2 * RW_WIDTH]
    v = z[:, 2 * RW_WIDTH:3 * RW_WIDTH]
    lora_in = z[:, 3 * RW_WIDTH:3 * RW_WIDTH + DECAY_LORA + AAA_LORA]
    g_d = z[:, 3 * RW_WIDTH + DECAY_LORA + AAA_LORA:]
    w0 = vec_ref[0:1, :]
    a0 = vec_ref[1:2, :]
    k_k = vec_ref[2:3, :]
    k_a = vec_ref[3:4, :]
    r_k = vec_ref[4:5, :]

    w_raw = w0 + _bdot(jnp.tanh(lora_in), w2_ref[...])
    logw = -math.exp(-0.5) * _sigmoid(w_raw)
    a = _sigmoid(a0 + _bdot(lora_in, a2_ref[...]))
    gate_ref[...] = _bdot(_sigmoid(g_d), g2_ref[...])

    gsum = gsum_ref[...]
    kk = k * k_k
    kk = kk / jnp.maximum(jnp.sqrt(_per_slab(_bdot, kk * kk, gsum)), 1e-12)
    k_mod = k * (1.0 + (a - 1.0) * k_a)
    b = kk * a
    bonus_ref[...] = _per_slab(_dot_split2_rhs, r * k_mod * r_k, gsum) * v

    lane = lax.broadcasted_iota(jnp.int32, (CHUNK, PAIR), 1)
    head0 = lane < RW_HEAD_DIM
    strict = mask_ref[0]
    incl = mask_ref[1]
    eye = mask_ref[2]
    tri = tri_ref[...]

    def stack(x):
        return jnp.concatenate([jnp.where(head0, x, 0.0), jnp.where(head0, 0.0, x)], axis=0).astype(BF16)

    ops = {}
    for c in range(rows // CHUNK):
        rs = slice(c * CHUNK, (c + 1) * CHUNK)
        lw = logw[rs]
        cum = _dot_exact_lhs(tri, lw)
        cum_last = cum[CHUNK - 1:CHUNK]
        w_incl = jnp.exp(cum)
        w_inv = jnp.exp(-cum)
        w_prev = jnp.exp(cum - lw)
        w_end = jnp.exp(cum_last - cum)
        w_last = jnp.exp(cum_last)
        full = dict(at=-kk[rs] * w_prev, rt=r[rs] * w_incl, bt=b[rs] * w_inv, kt=k_mod[rs] * w_inv,
                    be=b[rs] * w_end, ke=k_mod[rs] * w_end, v2=v[rs])
        for p in range(RW_PAIRS):
            ls = slice(p * PAIR, (p + 1) * PAIR)
            item = {name: val[:, ls] for name, val in full.items()}
            item["w_last"] = w_last[:, ls]
            ops[(c, p)] = item

    keys = list(ops)
    for g0 in range(0, len(keys), RW1_GROUP):
        group = keys[g0:g0 + RW1_GROUP]
        n_ab, a_ak, a_rb, a_rk, t_inv, sv = {}, {}, {}, {}, {}, {}
        for key in group:
            o = ops[key]
            prod = _bdot_nt(jnp.concatenate([o["at"], o["rt"]], axis=0),
                            jnp.concatenate([stack(o["bt"]), stack(o["kt"])], axis=0))
            n_ab[key] = prod[:CHUNK, :PAIR] * strict
            a_ak[key] = prod[:CHUNK, PAIR:] * strict
            a_rb[key] = prod[CHUNK:, :PAIR] * incl
            a_rk[key] = prod[CHUNK:, PAIR:] * incl
            sv[key] = stack(o["v2"])
            t_inv[key] = eye + n_ab[key] * mask_ref[3]
        for lvl in range(1, INV_LEVELS):
            half = {key: _bdot(t_inv[key], stack(n_ab[key] * mask_ref[3 + lvl])) for key in group}
            t_inv = {key: t_inv[key] + _bdot(half[key], stack(t_inv[key])) for key in group}
        akv = {key: _bdot(a_ak[key], sv[key]) for key in group}
        pq = {key: _bdot(t_inv[key], jnp.concatenate([stack(ops[key]["at"]), stack(akv[key])], axis=1))
              for key in group}
        gy = {key: _bdot(a_rb[key], jnp.concatenate([stack(pq[key][:, :PAIR]), stack(pq[key][:, PAIR:])], axis=1))
              for key in group}
        ark_v = {key: _bdot(a_rk[key], sv[key]) for key in group}
        mc = {key: _bdot_tn(ops[key]["be"], pq[key]) for key in group}
        ke_v = {key: _bdot_tn(ops[key]["ke"], ops[key]["v2"]) for key in group}
        for key in group:
            c, p = key
            rs = slice(c * CHUNK, (c + 1) * CHUNK)
            ls = slice(p * PAIR, (p + 1) * PAIR)
            g_ref[rs, ls] = ops[key]["rt"] + gy[key][:, :PAIR]
            yi_ref[rs, ls] = gy[key][:, PAIR:] + ark_v[key]
            cv = mc[key][:, PAIR:] + ke_v[key]
            m_ref[c, p] = eye * ops[key]["w_last"] + jnp.where(head0, mc[key][:CHUNK, :PAIR], mc[key][CHUNK:, :PAIR])
            c_ref[c, p] = jnp.where(head0, cv[:CHUNK], cv[CHUNK:])


def _rw1_call(zrw, w2p, a2p, g2, vec, gsum, tri, masks):
    t = zrw.shape[0]
    rows = RW1_CHUNKS * CHUNK
    nchunks = t // CHUNK
    tok = lambda i: (i, 0)
    blk = lambda i: (i, 0, 0, 0)
    tok_spec = pl.BlockSpec((rows, RW_WIDTH), tok)
    mc_spec = pl.BlockSpec((RW1_CHUNKS, RW_PAIRS, CHUNK, PAIR), blk)
    tok_shape = jax.ShapeDtypeStruct((t, RW_WIDTH), F32)
    mc_shape = jax.ShapeDtypeStruct((nchunks, RW_PAIRS, CHUNK, PAIR), F32)
    return pl.pallas_call(
        _rw1_kernel,
        grid=(t // rows,),
        in_specs=[pl.BlockSpec((rows, RW_COLS), tok),
                  _const_spec(w2p.shape), _const_spec(a2p.shape), _const_spec(g2.shape),
                  _const_spec(vec.shape), _const_spec(gsum.shape), _const_spec(tri.shape),
                  _const_spec(masks.shape)],
        out_specs=[tok_spec, tok_spec, tok_spec, tok_spec, mc_spec, mc_spec],
        out_shape=[tok_shape, tok_shape, tok_shape, tok_shape, mc_shape, mc_shape],
        compiler_params=_params(("parallel",)),
        name="rwkv_chunk_local",
    )(zrw, w2p, a2p, g2, vec, gsum, tri, masks)


RW2_CHUNKS = 8


def _rw2_kernel(g_ref, yi_ref, bonus_ref, gate_ref, m_ref, c_ref, gmean_ref, gn_ref, y_ref, h_ref):
    @pl.when(pl.program_id(1) == 0)
    def _():
        h_ref[...] = jnp.zeros(h_ref.shape, F32)

    lane = lax.broadcasted_iota(jnp.int32, (CHUNK, PAIR), 1)
    head0 = lane < RW_HEAD_DIM

    def block_diag(x):
        return jnp.concatenate([jnp.where(head0, x, 0.0), jnp.where(head0, 0.0, x)], axis=0)

    ys = []
    for c in range(RW2_CHUNKS):
        rs = slice(c * CHUNK, (c + 1) * CHUNK)
        cols = []
        for p in range(RW_PAIRS):
            ls = slice(p * PAIR, (p + 1) * PAIR)
            h = h_ref[p]
            cols.append(_bdot(g_ref[rs, ls], h) + yi_ref[rs, ls])
            h_ref[p] = _dot_f32(block_diag(m_ref[c, p]), h) + block_diag(c_ref[c, p])
        ys.append(jnp.concatenate(cols, axis=1))
    y = jnp.concatenate(ys, axis=0)
    gmean = gmean_ref[...]
    mu = _per_slab(_dot_split2_rhs, y, gmean)
    d = y - mu
    var = _per_slab(_bdot, d * d, gmean)
    y = d * lax.rsqrt(var + GN_EPS) * gn_ref[0:1, :] + gn_ref[1:2, :]
    y_ref[...] = ((y + bonus_ref[...]) * gate_ref[...]).astype(BF16)


def _rw2_call(g, yi, bonus, gate, m, c, gmean, gn, bsz, seq):
    t = g.shape[0]
    rows = RW2_CHUNKS * CHUNK
    nt = seq // rows
    tok = lambda b, i: (b * nt + i, 0)
    blk = lambda b, i: (b * nt + i, 0, 0, 0)
    tok_spec = pl.BlockSpec((rows, RW_WIDTH), tok)
    mc_spec = pl.BlockSpec((RW2_CHUNKS, RW_PAIRS, CHUNK, PAIR), blk)
    return pl.pallas_call(
        _rw2_kernel,
        grid=(bsz, nt),
        in_specs=[tok_spec, tok_spec, tok_spec, tok_spec, mc_spec, mc_spec,
                  _const_spec(gmean.shape), _const_spec(gn.shape)],
        out_specs=tok_spec,
        out_shape=jax.ShapeDtypeStruct((t, RW_WIDTH), BF16),
        scratch_shapes=[pltpu.VMEM((RW_PAIRS, PAIR, PAIR), F32)],
        compiler_params=_params(("parallel", "arbitrary")),
        name="rwkv_scan",
    )(g, yi, bonus, gate, m, c, gmean, gn)


ML_CHUNK = 128
ML_STEP_CHUNKS = 2


def _log_sigmoid(x):
    return jnp.minimum(x, 0.0) - jnp.log(1.0 + jnp.exp(-jnp.abs(x)))


def _ml_kernel(z_ref, zif_ref, ifb_ref, norm_ref, tri_ref, y_ref, state_ref, m_ref):
    @pl.when(pl.program_id(1) == 0)
    def _():
        state_ref[...] = jnp.zeros(state_ref.shape, F32)
        m_ref[...] = jnp.zeros(m_ref.shape, F32)

    rows = z_ref.shape[0]
    q_all = z_ref[:, 0:ML_WIDTH]
    k_all = z_ref[:, ML_WIDTH:2 * ML_WIDTH]
    v_all = z_ref[:, 2 * ML_WIDTH:3 * ML_WIDTH]
    ogate_all = z_ref[:, 3 * ML_WIDTH:4 * ML_WIDTH]

    pre = zif_ref[...] + ifb_ref[...]
    logf = _log_sigmoid(pre)
    tri = tri_ref[...]
    lane = lax.broadcasted_iota(jnp.int32, (ML_CHUNK, LANES), 1)
    tq = lax.broadcasted_iota(jnp.int32, (ML_CHUNK, ML_CHUNK), 0)
    tk = lax.broadcasted_iota(jnp.int32, (ML_CHUNK, ML_CHUNK), 1)
    causal = tk <= tq
    ones = jnp.ones((ML_CHUNK, ML_HEAD_DIM), F32)
    dense = (ML_CHUNK, ML_CHUNK)

    for c in range(rows // ML_CHUNK):
        rs = slice(c * ML_CHUNK, (c + 1) * ML_CHUNK)
        bcum = _dot_exact_lhs(tri, logf[rs])
        z = jnp.where(lane < ML_HEADS, pre[rs], bcum)
        z_t = z.T
        heads = range(ML_HEADS)
        lsl = [slice(h * ML_HEAD_DIM, (h + 1) * ML_HEAD_DIM) for h in heads]
        b_col = [jnp.broadcast_to(z[:, ML_HEADS + h:ML_HEADS + h + 1], dense) for h in heads]
        ib_col = [jnp.broadcast_to(z[:, h:h + 1] - z[:, ML_HEADS + h:ML_HEADS + h + 1], dense) for h in heads]
        ib_row = [z_t[h:h + 1, :] - z_t[ML_HEADS + h:ML_HEADS + h + 1, :] for h in heads]
        b_end = [b_col[h][ML_CHUNK - 1:ML_CHUNK] for h in heads]
        m_prev = [m_ref[h] for h in heads]
        state = [state_ref[h] for h in heads]
        q = [q_all[rs, lsl[h]] for h in heads]
        k = [k_all[rs, lsl[h]] for h in heads]
        s_qk = [_bdot_nt(q[h], k[h]) for h in heads]
        inter = [_bdot(q[h], state[h]) for h in heads]

        log_d = [jnp.where(causal, b_col[h] + ib_row[h], NEG_BIG) for h in heads]
        log_inter = [b_col[h] + m_prev[h] for h in heads]
        m_row = [jnp.maximum(log_inter[h], jnp.max(log_d[h], axis=-1, keepdims=True)) for h in heads]
        p = [jnp.exp(log_d[h] - m_row[h]) * s_qk[h] for h in heads]
        intra = [_bdot(p[h], jnp.concatenate([v_all[rs, lsl[h]], ones], axis=1)) for h in heads]

        log_w_end = [ib_col[h] + b_end[h] for h in heads]
        m_chunk = [jnp.max(log_w_end[h], axis=0, keepdims=True) for h in heads]
        w_end = [jnp.exp(log_w_end[h] - m_chunk[h]) for h in heads]
        kv = [_bdot_tn(k[h], jnp.concatenate([w_end[h] * v_all[rs, lsl[h]], w_end[h]], axis=1)) for h in heads]
        for h in heads:
            m_new = jnp.maximum(b_end[h] + m_prev[h], m_chunk[h])
            dec = jnp.exp(b_end[h] + m_prev[h] - m_new)
            sc = jnp.exp(m_chunk[h] - m_new)
            state_ref[h] = jnp.concatenate([dec, dec], axis=1) * state[h] + jnp.concatenate([sc, sc], axis=1) * kv[h]
            m_ref[h] = m_new

        outs = []
        for h in heads:
            w_inter = jnp.exp(log_inter[h] - m_row[h])
            num = intra[h][:, :ML_HEAD_DIM] + w_inter * inter[h][:, :ML_HEAD_DIM]
            den = intra[h][:, ML_HEAD_DIM:] + w_inter * inter[h][:, ML_HEAD_DIM:]
            hh = num / jnp.maximum(jnp.abs(den), jnp.exp(-m_row[h]))
            mu = jnp.mean(hh, axis=-1, keepdims=True)
            d = hh - mu
            var = jnp.mean(d * d, axis=-1, keepdims=True)
            hn = d * lax.rsqrt(var + ML_NORM_EPS)
            outs.append(hn * norm_ref[:, lsl[h]] * ogate_all[rs, lsl[h]])
        y_ref[rs, :] = jnp.concatenate(outs, axis=1).astype(BF16)


def _ml_call(zml, zif, ifb, norm, tri, bsz, seq):
    t = zml.shape[0]
    rows = ML_STEP_CHUNKS * ML_CHUNK
    nt = seq // rows
    tok = lambda b, i: (b * nt + i, 0)
    return pl.pallas_call(
        _ml_kernel,
        grid=(bsz, nt),
        in_specs=[pl.BlockSpec((rows, ML_QKVO), tok), pl.BlockSpec((rows, IF_PAD), tok),
                  _const_spec(ifb.shape), _const_spec(norm.shape), _const_spec(tri.shape)],
        out_specs=pl.BlockSpec((rows, ML_WIDTH), tok),
        out_shape=jax.ShapeDtypeStruct((t, ML_WIDTH), BF16),
        scratch_shapes=[pltpu.VMEM((ML_HEADS, ML_HEAD_DIM, 2 * ML_HEAD_DIM), F32),
                        pltpu.VMEM((ML_HEADS, 1, LANES), F32)],
        compiler_params=_params(("parallel", "arbitrary")),
        name="mlstm",
    )(zml, zif, ifb, norm, tri)


MERGE_TM = 512


def _merge_kernel(x_ref, g_ref, wgt_ref, yrw_ref, yml_ref, wur_ref, wum_ref, wo_ref, o_ref):
    x = x_ref[...]
    d = x.shape[1]
    h = _rmsnorm(x, g_ref[...]).astype(BF16)
    gates = _sigmoid(jnp.dot(h, wgt_ref[...], preferred_element_type=F32))
    up_rw = jnp.dot(yrw_ref[...], wur_ref[...], preferred_element_type=F32)
    up_ml = jnp.dot(yml_ref[...], wum_ref[...], preferred_element_type=F32)
    merged = gates[:, 0:d] * up_rw + gates[:, d:] * up_ml
    o_ref[...] = x + _bdot(merged, wo_ref[...])


def _merge_call(x, g, wgt, yrw, yml, wur, wum, wo):
    t, d = x.shape
    tm = min(MERGE_TM, t)
    row = lambda i: (i, 0)
    return pl.pallas_call(
        _merge_kernel,
        grid=(t // tm,),
        in_specs=[pl.BlockSpec((tm, d), row), _const_spec(g.shape), _const_spec(wgt.shape),
                  pl.BlockSpec((tm, RW_WIDTH), row), pl.BlockSpec((tm, ML_WIDTH), row),
                  _const_spec(wur.shape), _const_spec(wum.shape), _const_spec(wo.shape)],
        out_specs=pl.BlockSpec((tm, d), row),
        out_shape=jax.ShapeDtypeStruct((t, d), F32),
        compiler_params=_params(("parallel",)),
        name="merge",
    )(x, g, wgt, yrw, yml, wur, wum, wo)


def _memkv_kernel(mem_ref, g_ref, wkv_ref, k_ref, v_ref):
    d = mem_ref.shape[1]
    hm = _rmsnorm(mem_ref[...], g_ref[...])
    kv = _bdot(hm, wkv_ref[...])
    k_ref[...] = kv[:, 0:d].astype(BF16)
    v_ref[...] = kv[:, d:].astype(BF16)


def _memkv_call(mem2d, g, wkv):
    n, d = mem2d.shape
    row = lambda i: (i, 0)
    return pl.pallas_call(
        _memkv_kernel,
        grid=(n // MEM_LEN,),
        in_specs=[pl.BlockSpec((MEM_LEN, d), row), _const_spec(g.shape), _const_spec(wkv.shape)],
        out_specs=[pl.BlockSpec((MEM_LEN, d), row), pl.BlockSpec((MEM_LEN, d), row)],
        out_shape=[jax.ShapeDtypeStruct((n, d), BF16), jax.ShapeDtypeStruct((n, d), BF16)],
        compiler_params=_params(("parallel",)),
        name="mem_kv",
    )(mem2d, g, wkv)


XA_TM = 512


def _xattn_kernel(x_ref, g_ref, wq_ref, k_ref, v_ref, wo_ref, o_ref):
    x = x_ref[...]
    hq = _rmsnorm(x, g_ref[...])
    q = _bdot(hq, wq_ref[...]) * (XA_HEAD_DIM ** -0.5)
    lsl = [slice(h * XA_HEAD_DIM, (h + 1) * XA_HEAD_DIM) for h in range(XA_HEADS)]

    def softmax(s):
        e = jnp.exp(s - jnp.max(s, axis=-1, keepdims=True))
        return e / jnp.sum(e, axis=-1, keepdims=True)

    outs = []
    s_prev = _bdot_nt(q[:, lsl[0]], k_ref[:, lsl[0]])
    for h in range(XA_HEADS):
        s_next = _bdot_nt(q[:, lsl[h + 1]], k_ref[:, lsl[h + 1]]) if h + 1 < XA_HEADS else None
        outs.append(_bdot(softmax(s_prev), v_ref[:, lsl[h]]))
        s_prev = s_next
    o = jnp.concatenate(outs, axis=1)
    o_ref[...] = x + _bdot(o, wo_ref[...])


def _xattn_call(x, g, wq, kmem, vmem, wo, bsz, seq):
    t, d = x.shape
    tm = min(XA_TM, seq)
    nt = seq // tm
    row = lambda b, i: (b * nt + i, 0)
    mem = lambda b, i: (b, 0)
    return pl.pallas_call(
        _xattn_kernel,
        grid=(bsz, nt),
        in_specs=[pl.BlockSpec((tm, d), row), _const_spec(g.shape), _const_spec(wq.shape),
                  pl.BlockSpec((MEM_LEN, d), mem), pl.BlockSpec((MEM_LEN, d), mem),
                  _const_spec(wo.shape)],
        out_specs=pl.BlockSpec((tm, d), row),
        out_shape=jax.ShapeDtypeStruct((t, d), F32),
        compiler_params=_params(("parallel", "parallel")),
        name="mem_xattn",
    )(x, g, wq, kmem, vmem, wo)


def _row(v):
    return v.reshape(1, -1).astype(F32)


def _layer(x, mem2d, bsz, seq, p):
    d = D_MODEL
    bf = lambda w: w.astype(BF16)
    masks_np, tri_np = _rw_masks()
    masks = jnp.asarray(masks_np)
    tri = jnp.asarray(tri_np, dtype=BF16)
    gsum = jnp.asarray(_group_ones(PAIR, RW_HEAD_DIM), dtype=BF16)
    gmean = jnp.asarray(_group_ones(PAIR, RW_HEAD_DIM) / RW_HEAD_DIM, dtype=BF16)

    x = _ffn_call(x, _row(p["ffn1_norm"]), bf(p["ffn1_w_gate"]), bf(p["ffn1_w_up"]), bf(p["ffn1_w_down"]),
                  _row(p["ffn1_norm"]), final=False)

    w_in = p["w_in"]
    c0 = RW_COLS
    c1 = c0 + ML_QKVO
    c2 = c1 + 2 * ML_HEADS
    w_if = jnp.pad(w_in[:, c1:c2], ((0, 0), (0, IF_PAD - 2 * ML_HEADS)))
    convw = jnp.concatenate([p["ml_conv_w"].astype(F32), jnp.zeros((SUBLANES - ML_CONV, 2 * ML_WIDTH), F32)], axis=0)
    zrw, zml, zif = _inproj_call(x, _row(p["mix_norm"]), bf(w_in[:, :c0]), bf(w_in[:, c0:c1]), bf(w_if),
                                 _row(p["rw_mu"]), convw, _row(p["ml_conv_b"]), bsz, seq)

    zpad = jnp.zeros((DECAY_LORA, RW_WIDTH), F32)
    w2p = bf(jnp.concatenate([p["rw_w2"], zpad], axis=0))
    a2p = bf(jnp.concatenate([zpad, p["rw_a2"]], axis=0))
    vec = jnp.stack([p["rw_w0"], p["rw_a0"], p["rw_k_k"], p["rw_k_a"], p["rw_r_k"].reshape(-1),
                     jnp.zeros_like(p["rw_w0"]), jnp.zeros_like(p["rw_w0"]), jnp.zeros_like(p["rw_w0"])]).astype(F32)
    g, yi, bonus, gate, m, c = _rw1_call(zrw, w2p, a2p, bf(p["rw_g2"]), vec, gsum, tri, masks)
    gn = jnp.concatenate([jnp.stack([p["rw_gn_w"], p["rw_gn_b"]]).astype(F32),
                          jnp.zeros((SUBLANES - 2, RW_WIDTH), F32)], axis=0)
    y_rw = _rw2_call(g, yi, bonus, gate, m, c, gmean, gn, bsz, seq)

    ifb = jnp.pad(jnp.concatenate([p["ml_i_b"], p["ml_f_b"]]).astype(F32), (0, IF_PAD - 2 * ML_HEADS)).reshape(1, -1)
    tri_ml = jnp.asarray(np.tril(np.ones((ML_CHUNK, ML_CHUNK), np.float32)), dtype=BF16)
    y_ml = _ml_call(zml, zif, ifb, _row(p["ml_norm"]), tri_ml, bsz, seq)

    x = _merge_call(x, _row(p["mix_norm"]), bf(w_in[:, c2:]), y_rw, y_ml,
                    bf(p["w_up_rw"]), bf(p["w_up_ml"]), bf(p["w_out"]))

    kmem, vmem = _memkv_call(mem2d, _row(p["mem_norm"]), bf(p["xa_wkv"]))
    x = _xattn_call(x, _row(p["xa_norm"]), bf(p["xa_wq"]), kmem, vmem, bf(p["xa_wo"]), bsz, seq)
    return x


def kernel(x, mem, ffn1_norm, ffn1_w_gate, ffn1_w_up, ffn1_w_down, mix_norm, w_in, rw_mu, rw_w0, rw_w2, rw_a0,
           rw_a2, rw_g2, rw_k_k, rw_k_a, rw_r_k, rw_gn_w, rw_gn_b, ml_conv_w, ml_conv_b, ml_i_b, ml_f_b, ml_norm,
           w_up_rw, w_up_ml, w_out, xa_norm, mem_norm, xa_wq, xa_wkv, xa_wo, ffn2_norm, ffn2_w_gate, ffn2_w_up,
           ffn2_w_down, final_norm):
    bsz, seq, d = x.shape
    depth = ffn1_norm.shape[0]
    stacked = dict(ffn1_norm=ffn1_norm, ffn1_w_gate=ffn1_w_gate, ffn1_w_up=ffn1_w_up, ffn1_w_down=ffn1_w_down,
                   mix_norm=mix_norm, w_in=w_in, rw_mu=rw_mu, rw_w0=rw_w0, rw_w2=rw_w2, rw_a0=rw_a0, rw_a2=rw_a2,
                   rw_g2=rw_g2, rw_k_k=rw_k_k, rw_k_a=rw_k_a, rw_r_k=rw_r_k, rw_gn_w=rw_gn_w, rw_gn_b=rw_gn_b,
                   ml_conv_w=ml_conv_w, ml_conv_b=ml_conv_b, ml_i_b=ml_i_b, ml_f_b=ml_f_b, ml_norm=ml_norm,
                   w_up_rw=w_up_rw, w_up_ml=w_up_ml, w_out=w_out, xa_norm=xa_norm, mem_norm=mem_norm,
                   xa_wq=xa_wq, xa_wkv=xa_wkv, xa_wo=xa_wo, ffn2_norm=ffn2_norm, ffn2_w_gate=ffn2_w_gate,
                   ffn2_w_up=ffn2_w_up, ffn2_w_down=ffn2_w_down)
    h = x.reshape(bsz * seq, d)
    mem2d = mem.reshape(bsz * mem.shape[1], d)
    bf = lambda w: w.astype(BF16)
    for l in range(depth):
        p = {name: w[l] for name, w in stacked.items()}
        h = _layer(h, mem2d, bsz, seq, p)
        last = l == depth - 1
        h = _ffn_call(h, _row(p["ffn2_norm"]), bf(p["ffn2_w_gate"]), bf(p["ffn2_w_up"]), bf(p["ffn2_w_down"]),
                      _row(final_norm), final=last)
    return h.reshape(bsz, seq, d)
```

```python
import functools
import math

import jax
import jax.numpy as jnp
import numpy as np
from jax import lax
from jax.experimental import pallas as pl
from jax.experimental.pallas import tpu as pltpu

F32 = jnp.float32
BF16 = jnp.bfloat16

D_MODEL = 1024
D_FF = 2816
FFN_HALF = 0.5
NORM_EPS = 1e-5
MEM_LEN = 256

RW_HEADS = 8
RW_HEAD_DIM = 64
RW_WIDTH = 512
DECAY_LORA = 64
AAA_LORA = 64
GATE_LORA = 128
RW_COLS = 3 * RW_WIDTH + DECAY_LORA + AAA_LORA + GATE_LORA
GN_EPS = 64e-5
RW_PAIRS = RW_HEADS // 2
PAIR = 2 * RW_HEAD_DIM

ML_HEADS = 4
ML_HEAD_DIM = 128
ML_WIDTH = 512
ML_CONV = 4
ML_NORM_EPS = 1e-6
ML_QKVO = 4 * ML_WIDTH
IF_PAD = 128

XA_HEADS = 4
XA_HEAD_DIM = 256

CHUNK = 64
LANES = 128
SUBLANES = 8
NEG_BIG = -1e30

VMEM_LIMIT = 56 * 1024 * 1024


def _bdot(a, b):
    return jnp.dot(a.astype(BF16), b.astype(BF16), preferred_element_type=F32)


def _bdot_nt(a, b):
    return lax.dot_general(a.astype(BF16), b.astype(BF16), (((1,), (1,)), ((), ())),
                           preferred_element_type=F32)


def _bdot_tn(a, b):
    return lax.dot_general(a.astype(BF16), b.astype(BF16), (((0,), (0,)), ((), ())),
                           preferred_element_type=F32)


def _split3(x):
    hi = x.astype(BF16)
    r1 = x - hi.astype(F32)
    mid = r1.astype(BF16)
    lo = (r1 - mid.astype(F32)).astype(BF16)
    return hi, mid, lo


def _dot_exact_lhs(c_bf16, x):
    hi, mid, lo = _split3(x)
    f = lambda p: jnp.dot(c_bf16, p, preferred_element_type=F32)
    return f(hi) + f(mid) + f(lo)


def _dot_split2_rhs(x, c_bf16):
    hi = x.astype(BF16)
    lo = (x - hi.astype(F32)).astype(BF16)
    f = lambda p: jnp.dot(p, c_bf16, preferred_element_type=F32)
    return f(hi) + f(lo)


def _per_slab(fn, x, c_bf16):
    n = x.shape[1] // LANES
    return jnp.concatenate([fn(x[:, i * LANES:(i + 1) * LANES], c_bf16) for i in range(n)], axis=1)


def _dot_f32(a, b):
    a_hi = a.astype(BF16)
    a_lo = (a - a_hi.astype(F32)).astype(BF16)
    b_hi = b.astype(BF16)
    b_lo = (b - b_hi.astype(F32)).astype(BF16)
    f = lambda p, q: jnp.dot(p, q, preferred_element_type=F32)
    return f(a_hi, b_hi) + f(a_hi, b_lo) + f(a_lo, b_hi)


def _sigmoid(x):
    return 1.0 / (1.0 + jnp.exp(-x))


def _silu(x):
    return x * _sigmoid(x)


def _rmsnorm(x, g):
    ms = jnp.mean(x * x, axis=-1, keepdims=True)
    return x * lax.rsqrt(ms + NORM_EPS) * g


def _shift_rows(u, carry8, j):
    rolled = pltpu.roll(u, j, axis=0)
    rolled_c = pltpu.roll(carry8, j, axis=0)
    row = lax.broadcasted_iota(jnp.int32, (SUBLANES, u.shape[1]), 0)
    top = jnp.where(row < j, rolled_c, rolled[:SUBLANES])
    return jnp.concatenate([top, rolled[SUBLANES:]], axis=0)


def _shift_rows_via(buf_ref, u, carry8, j):
    rows = u.shape[0]
    buf_ref[0:SUBLANES, :] = carry8
    buf_ref[SUBLANES:, :] = u
    return buf_ref[SUBLANES - j:SUBLANES - j + rows, :]


def _const_spec(shape):
    nd = len(shape)
    return pl.BlockSpec(shape, lambda *_: (0,) * nd, pipeline_mode=pl.Buffered(1))


def _params(sem):
    return pltpu.CompilerParams(dimension_semantics=sem, vmem_limit_bytes=VMEM_LIMIT)


FF_CHUNK = 256
FFN_TM = 512


def _ffn_kernel(x_ref, g_ref, wg_ref, wu_ref, wd_ref, fg_ref, o_ref, *, final):
    x = x_ref[...]
    h = _rmsnorm(x, g_ref[...]).astype(BF16)
    acc = jnp.zeros(x.shape, F32)
    for c in range(D_FF // FF_CHUNK):
        sl = slice(c * FF_CHUNK, (c + 1) * FF_CHUNK)
        gate = jnp.dot(h, wg_ref[:, sl], preferred_element_type=F32)
        up = jnp.dot(h, wu_ref[:, sl], preferred_element_type=F32)
        act = (_silu(gate) * up).astype(BF16)
        acc = acc + jnp.dot(act, wd_ref[sl, :], preferred_element_type=F32)
    y = x + FFN_HALF * acc
    if final:
        y = _rmsnorm(y, fg_ref[...])
    o_ref[...] = y


def _ffn_call(x, g, wg, wu, wd, fg, final):
    t, d = x.shape
    tm = min(FFN_TM, t)
    return pl.pallas_call(
        functools.partial(_ffn_kernel, final=final),
        grid=(t // tm,),
        in_specs=[pl.BlockSpec((tm, d), lambda i: (i, 0)),
                  _const_spec((1, d)), _const_spec((d, D_FF)), _const_spec((d, D_FF)),
                  _const_spec((D_FF, d)), _const_spec((1, d))],
        out_specs=pl.BlockSpec((tm, d), lambda i: (i, 0)),
        out_shape=jax.ShapeDtypeStruct((t, d), F32),
        compiler_params=_params(("parallel",)),
        name="ffn_final" if final else "ffn",
    )(x, g, wg, wu, wd, fg)


INPROJ_TM = 512
INPROJ_COLS = 256
INPROJ_LAG = 1


def _inproj_kernel(x_ref, g_ref, wrw_ref, wml_ref, wif_ref, mu_ref, convw_ref, convb_ref,
                   zrw_ref, zml_ref, zif_ref, carry_rw_ref, carry_ml_ref, buf_a_ref, buf_b_ref):
    @pl.when(pl.program_id(1) == 0)
    def _():
        carry_rw_ref[...] = jnp.zeros(carry_rw_ref.shape, F32)
        carry_ml_ref[...] = jnp.zeros(carry_ml_ref.shape, F32)

    rows = x_ref.shape[0]
    h = _rmsnorm(x_ref[...], g_ref[...]).astype(BF16)

    def conv_stage(cols, scale):
        def finish(u):
            carry = carry_ml_ref[:, cols]
            w0, w1, w2, w3 = (convw_ref[j:j + 1, cols] for j in range(ML_CONV))
            u1 = _shift_rows_via(buf_a_ref, u, carry, 1)
            pair = w2 * u + w3 * u1
            pair_carry = w2 * carry + w3 * pltpu.roll(carry, 1, axis=0)
            conv = convb_ref[:, cols] + w0 * u + w1 * u1 + _shift_rows_via(buf_b_ref, pair, pair_carry, 2)
            carry_ml_ref[:, cols] = u[rows - SUBLANES:]
            zml_ref[:, cols] = _silu(conv) * scale if scale != 1.0 else _silu(conv)
        return wml_ref, cols, finish

    def plain_stage(cols, fn):
        def finish(u):
            zml_ref[:, cols] = fn(u)
        return wml_ref, cols, finish

    def lerp_stage(cols):
        def finish(z):
            prev = _shift_rows_via(buf_a_ref, z, carry_rw_ref[:, cols], 1)
            carry_rw_ref[:, cols] = z[rows - SUBLANES:]
            zrw_ref[:, cols] = z + (prev - z) * mu_ref[:, cols]
        return wrw_ref, cols, finish

    blk = lambda c: slice(c * INPROJ_COLS, (c + 1) * INPROJ_COLS)
    n_head_blocks = ML_WIDTH // INPROJ_COLS
    heavy, light = [], []
    for c in range(ML_QKVO // INPROJ_COLS):
        group = c // n_head_blocks
        if group < 2:
            heavy.append(conv_stage(blk(c), ML_HEAD_DIM ** -0.5 if group == 0 else 1.0))
        else:
            light.append(plain_stage(blk(c), _sigmoid if group == 3 else (lambda u: u)))
    light += [lerp_stage(blk(c)) for c in range(RW_COLS // INPROJ_COLS)]

    per_heavy = -(-len(light) // len(heavy))
    stages = []
    for i, stage in enumerate(heavy):
        stages.append(stage)
        stages += light[i * per_heavy:(i + 1) * per_heavy]
    pending = []
    for w_ref, cols, finish in stages:
        acc = jnp.dot(h, w_ref[:, cols], preferred_element_type=F32)
        if len(pending) == INPROJ_LAG:
            fn, val = pending.pop(0)
            fn(val)
        pending.append((finish, acc))
    zif_ref[...] = jnp.dot(h, wif_ref[...], preferred_element_type=F32)
    for fn, val in pending:
        fn(val)


def _inproj_call(x, g, wrw, wml, wif, mu, convw, convb, bsz, seq):
    t, d = x.shape
    tm = min(INPROJ_TM, seq)
    nt = seq // tm
    row = lambda b, i: (b * nt + i, 0)
    return pl.pallas_call(
        _inproj_kernel,
        grid=(bsz, nt),
        in_specs=[pl.BlockSpec((tm, d), row),
                  _const_spec((1, d)), _const_spec((d, RW_COLS)), _const_spec((d, ML_QKVO)),
                  _const_spec((d, IF_PAD)), _const_spec((1, RW_COLS)),
                  _const_spec(convw.shape), _const_spec(convb.shape)],
        out_specs=[pl.BlockSpec((tm, RW_COLS), row), pl.BlockSpec((tm, ML_QKVO), row),
                   pl.BlockSpec((tm, IF_PAD), row)],
        out_shape=[jax.ShapeDtypeStruct((t, RW_COLS), F32), jax.ShapeDtypeStruct((t, ML_QKVO), F32),
                   jax.ShapeDtypeStruct((t, IF_PAD), F32)],
        scratch_shapes=[pltpu.VMEM((SUBLANES, RW_COLS), F32), pltpu.VMEM((SUBLANES, 2 * ML_WIDTH), F32),
                        pltpu.VMEM((SUBLANES + tm, INPROJ_COLS), F32),
                        pltpu.VMEM((SUBLANES + tm, INPROJ_COLS), F32)],
        compiler_params=_params(("parallel", "arbitrary")),
        name="inproj",
    )(x, g, wrw, wml, wif, mu, convw, convb)


RW1_CHUNKS = 4
RW1_GROUP = 16
INV_LEVELS = 6


def _rw_masks():
    t = np.arange(CHUNK)[:, None]
    s = np.tile(np.arange(CHUNK), 2)[None, :]
    levels = []
    for k in range(INV_LEVELS):
        size = 1 << k
        levels.append(((t // (2 * size)) == (s // (2 * size))) & ((t // size) % 2 == 1) & ((s // size) % 2 == 0))
    m = np.stack([s < t, s <= t, s == t] + levels).astype(np.float32)
    tri = (np.arange(CHUNK)[:, None] >= np.arange(CHUNK)[None, :]).astype(np.float32)
    return m, tri


def _group_ones(width, group):
    i = np.arange(width)
    return (i[:, None] // group == i[None, :] // group).astype(np.float32)


def _rw1_kernel(z_ref, w2_ref, a2_ref, g2_ref, vec_ref, gsum_ref, tri_ref, mask_ref,
                g_ref, yi_ref, bonus_ref, gate_ref, m_ref, c_ref):
    z = z_ref[...]
    rows = z.shape[0]
    r = z[:, 0:RW_WIDTH]
    k = z[:, RW_WIDTH:2 * RW_WIDTH]
    v = z[:, 2 * RW_WIDTH:3 * RW_WIDTH]
    lora_in = z[:, 3 * RW_WIDTH:3 * RW_WIDTH + DECAY_LORA + AAA_LORA]
    g_d = z[:, 3 * RW_WIDTH + DECAY_LORA + AAA_LORA:]
    w0 = vec_ref[0:1, :]
    a0 = vec_ref[1:2, :]
    k_k = vec_ref[2:3, :]
    k_a = vec_ref[3:4, :]
    r_k = vec_ref[4:5, :]

    w_raw = w0 + _bdot(jnp.tanh(lora_in), w2_ref[...])
    logw = -math.exp(-0.5) * _sigmoid(w_raw)
    a = _sigmoid(a0 + _bdot(lora_in, a2_ref[...]))
    gate_ref[...] = _bdot(_sigmoid(g_d), g2_ref[...])

    gsum = gsum_ref[...]
    kk = k * k_k
    kk = kk / jnp.maximum(jnp.sqrt(_per_slab(_bdot, kk * kk, gsum)), 1e-12)
    k_mod = k * (1.0 + (a - 1.0) * k_a)
    b = kk * a
    bonus_ref[...] = _per_slab(_dot_split2_rhs, r * k_mod * r_k, gsum) * v

    lane = lax.broadcasted_iota(jnp.int32, (CHUNK, PAIR), 1)
    head0 = lane < RW_HEAD_DIM
    strict = mask_ref[0]
    incl = mask_ref[1]
    eye = mask_ref[2]
    tri = tri_ref[...]

    def stack(x):
        return jnp.concatenate([jnp.where(head0, x, 0.0), jnp.where(head0, 0.0, x)], axis=0).astype(BF16)

    ops = {}
    for c in range(rows // CHUNK):
        rs = slice(c * CHUNK, (c + 1) * CHUNK)
        lw = logw[rs]
        cum = _dot_exact_lhs(tri, lw)
        cum_last = cum[CHUNK - 1:CHUNK]
        w_incl = jnp.exp(cum)
        w_inv = jnp.exp(-cum)
        w_prev = jnp.exp(cum - lw)
        w_end = jnp.exp(cum_last - cum)
        w_last = jnp.exp(cum_last)
        full = dict(at=-kk[rs] * w_prev, rt=r[rs] * w_incl, bt=b[rs] * w_inv, kt=k_mod[rs] * w_inv,
                    be=b[rs] * w_end, ke=k_mod[rs] * w_end, v2=v[rs])
        for p in range(RW_PAIRS):
            ls = slice(p * PAIR, (p + 1) * PAIR)
            item = {name: val[:, ls] for name, val in full.items()}
            item["w_last"] = w_last[:, ls]
            ops[(c, p)] = item

    keys = list(ops)
    for g0 in range(0, len(keys), RW1_GROUP):
        group = keys[g0:g0 + RW1_GROUP]
        n_ab, a_ak, a_rb, a_rk, t_inv, sv = {}, {}, {}, {}, {}, {}
        for key in group:
            o = ops[key]
            prod = _bdot_nt(jnp.concatenate([o["at"], o["rt"]], axis=0),
                            jnp.concatenate([stack(o["bt"]), stack(o["kt"])], axis=0))
            n_ab[key] = prod[:CHUNK, :PAIR] * strict
            a_ak[key] = prod[:CHUNK, PAIR:] * strict
            a_rb[key] = prod[CHUNK:, :PAIR] * incl
            a_rk[key] = prod[CHUNK:, PAIR:] * incl
            sv[key] = stack(o["v2"])
            t_inv[key] = eye + n_ab[key] * mask_ref[3]
        for lvl in range(1, INV_LEVELS):
            half = {key: _bdot(t_inv[key], stack(n_ab[key] * mask_ref[3 + lvl])) for key in group}
            t_inv = {key: t_inv[key] + _bdot(half[key], stack(t_inv[key])) for key in group}
        akv = {key: _bdot(a_ak[key], sv[key]) for key in group}
        pq = {key: _bdot(t_inv[key], jnp.concatenate([stack(ops[key]["at"]), stack(akv[key])], axis=1))
              for key in group}
        gy = {key: _bdot(a_rb[key], jnp.concatenate([stack(pq[key][:, :PAIR]), stack(pq[key][:, PAIR:])], axis=1))
              for key in group}
        ark_v = {key: _bdot(a_rk[key], sv[key]) for key in group}
        mc = {key: _bdot_tn(ops[key]["be"], pq[key]) for key in group}
        ke_v = {key: _bdot_tn(ops[key]["ke"], ops[key]["v2"]) for key in group}
        for key in group:
            c, p = key
            rs = slice(c * CHUNK, (c + 1) * CHUNK)
            ls = slice(p * PAIR, (p + 1) * PAIR)
            g_ref[rs, ls] = ops[key]["rt"] + gy[key][:, :PAIR]
            yi_ref[rs, ls] = gy[key][:, PAIR:] + ark_v[key]
            cv = mc[key][:, PAIR:] + ke_v[key]
            m_ref[c, p] = eye * ops[key]["w_last"] + jnp.where(head0, mc[key][:CHUNK, :PAIR], mc[key][CHUNK:, :PAIR])
            c_ref[c, p] = jnp.where(head0, cv[:CHUNK], cv[CHUNK:])


def _rw1_call(zrw, w2p, a2p, g2, vec, gsum, tri, masks):
    t = zrw.shape[0]
    rows = RW1_CHUNKS * CHUNK
    nchunks = t // CHUNK
    tok = lambda i: (i, 0)
    blk = lambda i: (i, 0, 0, 0)
    tok_spec = pl.BlockSpec((rows, RW_WIDTH), tok)
    mc_spec = pl.BlockSpec((RW1_CHUNKS, RW_PAIRS, CHUNK, PAIR), blk)
    tok_shape = jax.ShapeDtypeStruct((t, RW_WIDTH), F32)
    mc_shape = jax.ShapeDtypeStruct((nchunks, RW_PAIRS, CHUNK, PAIR), F32)
    return pl.pallas_call(
        _rw1_kernel,
        grid=(t // rows,),
        in_specs=[pl.BlockSpec((rows, RW_COLS), tok),
                  _const_spec(w2p.shape), _const_spec(a2p.shape), _const_spec(g2.shape),
                  _const_spec(vec.shape), _const_spec(gsum.shape), _const_spec(tri.shape),
                  _const_spec(masks.shape)],
        out_specs=[tok_spec, tok_spec, tok_spec, tok_spec, mc_spec, mc_spec],
        out_shape=[tok_shape, tok_shape, tok_shape, tok_shape, mc_shape, mc_shape],
        compiler_params=_params(("parallel",)),
        name="rwkv_chunk_local",
    )(zrw, w2p, a2p, g2, vec, gsum, tri, masks)


RW2_CHUNKS = 8


def _rw2_kernel(g_ref, yi_ref, bonus_ref, gate_ref, m_ref, c_ref, gmean_ref, gn_ref, y_ref, h_ref):
    @pl.when(pl.program_id(1) == 0)
    def _():
        h_ref[...] = jnp.zeros(h_ref.shape, F32)

    lane = lax.broadcasted_iota(jnp.int32, (CHUNK, PAIR), 1)
    head0 = lane < RW_HEAD_DIM

    def block_diag(x):
        return jnp.concatenate([jnp.where(head0, x, 0.0), jnp.where(head0, 0.0, x)], axis=0)

    ys = []
    for c in range(RW2_CHUNKS):
        rs = slice(c * CHUNK, (c + 1) * CHUNK)
        cols = []
        for p in range(RW_PAIRS):
            ls = slice(p * PAIR, (p + 1) * PAIR)
            h = h_ref[p]
            cols.append(_bdot(g_ref[rs, ls], h) + yi_ref[rs, ls])
            h_ref[p] = _dot_f32(block_diag(m_ref[c, p]), h) + block_diag(c_ref[c, p])
        ys.append(jnp.concatenate(cols, axis=1))
    y = jnp.concatenate(ys, axis=0)
    gmean = gmean_ref[...]
    mu = _per_slab(_dot_split2_rhs, y, gmean)
    d = y - mu
    var = _per_slab(_bdot, d * d, gmean)
    y = d * lax.rsqrt(var + GN_EPS) * gn_ref[0:1, :] + gn_ref[1:2, :]
    y_ref[...] = ((y + bonus_ref[...]) * gate_ref[...]).astype(BF16)


def _rw2_call(g, yi, bonus, gate, m, c, gmean, gn, bsz, seq):
    t = g.shape[0]
    rows = RW2_CHUNKS * CHUNK
    nt = seq // rows
    tok = lambda b, i: (b * nt + i, 0)
    blk = lambda b, i: (b * nt + i, 0, 0, 0)
    tok_spec = pl.BlockSpec((rows, RW_WIDTH), tok)
    mc_spec = pl.BlockSpec((RW2_CHUNKS, RW_PAIRS, CHUNK, PAIR), blk)
    return pl.pallas_call(
        _rw2_kernel,
        grid=(bsz, nt),
        in_specs=[tok_spec, tok_spec, tok_spec, tok_spec, mc_spec, mc_spec,
                  _const_spec(gmean.shape), _const_spec(gn.shape)],
        out_specs=tok_spec,
        out_shape=jax.ShapeDtypeStruct((t, RW_WIDTH), BF16),
        scratch_shapes=[pltpu.VMEM((RW_PAIRS, PAIR, PAIR), F32)],
        compiler_params=_params(("parallel", "arbitrary")),
        name="rwkv_scan",
    )(g, yi, bonus, gate, m, c, gmean, gn)


ML_CHUNK = 128
ML_STEP_CHUNKS = 4


def _log_sigmoid(x):
    return jnp.minimum(x, 0.0) - jnp.log(1.0 + jnp.exp(-jnp.abs(x)))


def _ml_kernel(z_ref, zif_ref, ifb_ref, norm_ref, tri_ref, y_ref, state_ref, m_ref):
    @pl.when(pl.program_id(1) == 0)
    def _():
        state_ref[...] = jnp.zeros(state_ref.shape, F32)
        m_ref[...] = jnp.zeros(m_ref.shape, F32)

    rows = z_ref.shape[0]
    q_all = z_ref[:, 0:ML_WIDTH]
    k_all = z_ref[:, ML_WIDTH:2 * ML_WIDTH]
    v_all = z_ref[:, 2 * ML_WIDTH:3 * ML_WIDTH]
    ogate_all = z_ref[:, 3 * ML_WIDTH:4 * ML_WIDTH]

    pre = zif_ref[...] + ifb_ref[...]
    logf = _log_sigmoid(pre)
    tri = tri_ref[...]
    lane = lax.broadcasted_iota(jnp.int32, (ML_CHUNK, LANES), 1)
    tq = lax.broadcasted_iota(jnp.int32, (ML_CHUNK, ML_CHUNK), 0)
    tk = lax.broadcasted_iota(jnp.int32, (ML_CHUNK, ML_CHUNK), 1)
    causal = tk <= tq
    ones = jnp.ones((ML_CHUNK, ML_HEAD_DIM), F32)
    dense = (ML_CHUNK, ML_CHUNK)

    m_run = [m_ref[h] for h in range(ML_HEADS)]
    state_run = [state_ref[h] for h in range(ML_HEADS)]
    ys = []
    for c in range(rows // ML_CHUNK):
        rs = slice(c * ML_CHUNK, (c + 1) * ML_CHUNK)
        bcum = _dot_exact_lhs(tri, logf[rs])
        z = jnp.where(lane < ML_HEADS, pre[rs], bcum)
        z_t = z.T
        heads = range(ML_HEADS)
        lsl = [slice(h * ML_HEAD_DIM, (h + 1) * ML_HEAD_DIM) for h in heads]
        b_col = [jnp.broadcast_to(z[:, ML_HEADS + h:ML_HEADS + h + 1], dense) for h in heads]
        ib_col = [jnp.broadcast_to(z[:, h:h + 1] - z[:, ML_HEADS + h:ML_HEADS + h + 1], dense) for h in heads]
        ib_row = [z_t[h:h + 1, :] - z_t[ML_HEADS + h:ML_HEADS + h + 1, :] for h in heads]
        b_end = [b_col[h][ML_CHUNK - 1:ML_CHUNK] for h in heads]
        m_prev = m_run
        state = state_run
        q = [q_all[rs, lsl[h]] for h in heads]
        k = [k_all[rs, lsl[h]] for h in heads]
        s_qk = [_bdot_nt(q[h], k[h]) for h in heads]
        inter = [_bdot(q[h], state[h]) for h in heads]

        log_d = [jnp.where(causal, b_col[h] + ib_row[h], NEG_BIG) for h in heads]
        log_inter = [b_col[h] + m_prev[h] for h in heads]
        m_row = [jnp.maximum(log_inter[h], jnp.max(log_d[h], axis=-1, keepdims=True)) for h in heads]
        p = [jnp.exp(log_d[h] - m_row[h]) * s_qk[h] for h in heads]
        intra = [_bdot(p[h], jnp.concatenate([v_all[rs, lsl[h]], ones], axis=1)) for h in heads]

        log_w_end = [ib_col[h] + b_end[h] for h in heads]
        m_chunk = [jnp.max(log_w_end[h], axis=0, keepdims=True) for h in heads]
        w_end = [jnp.exp(log_w_end[h] - m_chunk[h]) for h in heads]
        kv = [_bdot_tn(k[h], jnp.concatenate([w_end[h] * v_all[rs, lsl[h]], w_end[h]], axis=1)) for h in heads]
        m_run, state_run = [], []
        for h in heads:
            m_new = jnp.maximum(b_end[h] + m_prev[h], m_chunk[h])
            dec = jnp.exp(b_end[h] + m_prev[h] - m_new)
            sc = jnp.exp(m_chunk[h] - m_new)
            state_run.append(jnp.concatenate([dec, dec], axis=1) * state[h] + jnp.concatenate([sc, sc], axis=1) * kv[h])
            m_run.append(m_new)

        outs = []
        for h in heads:
            w_inter = jnp.exp(log_inter[h] - m_row[h])
            num = intra[h][:, :ML_HEAD_DIM] + w_inter * inter[h][:, :ML_HEAD_DIM]
            den = intra[h][:, ML_HEAD_DIM:] + w_inter * inter[h][:, ML_HEAD_DIM:]
            hh = num / jnp.maximum(jnp.abs(den), jnp.exp(-m_row[h]))
            mu = jnp.mean(hh, axis=-1, keepdims=True)
            d = hh - mu
            var = jnp.mean(d * d, axis=-1, keepdims=True)
            hn = d * lax.rsqrt(var + ML_NORM_EPS)
            outs.append(hn * norm_ref[:, lsl[h]] * ogate_all[rs, lsl[h]])
        ys.append(jnp.concatenate(outs, axis=1).astype(BF16))
    y_ref[...] = jnp.concatenate(ys, axis=0)
    for h in range(ML_HEADS):
        state_ref[h] = state_run[h]
        m_ref[h] = m_run[h]


def _ml_call(zml, zif, ifb, norm, tri, bsz, seq):
    t = zml.shape[0]
    rows = ML_STEP_CHUNKS * ML_CHUNK
    nt = seq // rows
    tok = lambda b, i: (b * nt + i, 0)
    return pl.pallas_call(
        _ml_kernel,
        grid=(bsz, nt),
        in_specs=[pl.BlockSpec((rows, ML_QKVO), tok), pl.BlockSpec((rows, IF_PAD), tok),
                  _const_spec(ifb.shape), _const_spec(norm.shape), _const_spec(tri.shape)],
        out_specs=pl.BlockSpec((rows, ML_WIDTH), tok),
        out_shape=jax.ShapeDtypeStruct((t, ML_WIDTH), BF16),
        scratch_shapes=[pltpu.VMEM((ML_HEADS, ML_HEAD_DIM, 2 * ML_HEAD_DIM), F32),
                        pltpu.VMEM((ML_HEADS, 1, LANES), F32)],
        compiler_params=_params(("parallel", "arbitrary")),
        name="mlstm",
    )(zml, zif, ifb, norm, tri)


MERGE_TM = 512


def _merge_kernel(x_ref, g_ref, wgt_ref, yrw_ref, yml_ref, wur_ref, wum_ref, wo_ref, o_ref):
    x = x_ref[...]
    d = x.shape[1]
    h = _rmsnorm(x, g_ref[...]).astype(BF16)
    gates = _sigmoid(jnp.dot(h, wgt_ref[...], preferred_element_type=F32))
    up_rw = jnp.dot(yrw_ref[...], wur_ref[...], preferred_element_type=F32)
    up_ml = jnp.dot(yml_ref[...], wum_ref[...], preferred_element_type=F32)
    merged = gates[:, 0:d] * up_rw + gates[:, d:] * up_ml
    o_ref[...] = x + _bdot(merged, wo_ref[...])


def _merge_call(x, g, wgt, yrw, yml, wur, wum, wo):
    t, d = x.shape
    tm = min(MERGE_TM, t)
    row = lambda i: (i, 0)
    return pl.pallas_call(
        _merge_kernel,
        grid=(t // tm,),
        in_specs=[pl.BlockSpec((tm, d), row), _const_spec(g.shape), _const_spec(wgt.shape),
                  pl.BlockSpec((tm, RW_WIDTH), row), pl.BlockSpec((tm, ML_WIDTH), row),
                  _const_spec(wur.shape), _const_spec(wum.shape), _const_spec(wo.shape)],
        out_specs=pl.BlockSpec((tm, d), row),
        out_shape=jax.ShapeDtypeStruct((t, d), F32),
        compiler_params=_params(("parallel",)),
        name="merge",
    )(x, g, wgt, yrw, yml, wur, wum, wo)


def _memkv_kernel(mem_ref, g_ref, wkv_ref, k_ref, v_ref):
    d = mem_ref.shape[1]
    hm = _rmsnorm(mem_ref[...], g_ref[...])
    kv = _bdot(hm, wkv_ref[...])
    k_ref[...] = kv[:, 0:d].astype(BF16)
    v_ref[...] = kv[:, d:].astype(BF16)


def _memkv_call(mem2d, g, wkv):
    n, d = mem2d.shape
    row = lambda i: (i, 0)
    return pl.pallas_call(
        _memkv_kernel,
        grid=(n // MEM_LEN,),
        in_specs=[pl.BlockSpec((MEM_LEN, d), row), _const_spec(g.shape), _const_spec(wkv.shape)],
        out_specs=[pl.BlockSpec((MEM_LEN, d), row), pl.BlockSpec((MEM_LEN, d), row)],
        out_shape=[jax.ShapeDtypeStruct((n, d), BF16), jax.ShapeDtypeStruct((n, d), BF16)],
        compiler_params=_params(("parallel",)),
        name="mem_kv",
    )(mem2d, g, wkv)


XA_TM = 512


def _xattn_kernel(x_ref, g_ref, wq_ref, k_ref, v_ref, wo_ref, o_ref):
    x = x_ref[...]
    hq = _rmsnorm(x, g_ref[...])
    q = _bdot(hq, wq_ref[...]) * (XA_HEAD_DIM ** -0.5)
    lsl = [slice(h * XA_HEAD_DIM, (h + 1) * XA_HEAD_DIM) for h in range(XA_HEADS)]

    def softmax(s):
        e = jnp.exp(s - jnp.max(s, axis=-1, keepdims=True))
        return e / jnp.sum(e, axis=-1, keepdims=True)

    outs = []
    s_prev = _bdot_nt(q[:, lsl[0]], k_ref[:, lsl[0]])
    for h in range(XA_HEADS):
        s_next = _bdot_nt(q[:, lsl[h + 1]], k_ref[:, lsl[h + 1]]) if h + 1 < XA_HEADS else None
        outs.append(_bdot(softmax(s_prev), v_ref[:, lsl[h]]))
        s_prev = s_next
    o = jnp.concatenate(outs, axis=1)
    o_ref[...] = x + _bdot(o, wo_ref[...])


def _xattn_call(x, g, wq, kmem, vmem, wo, bsz, seq):
    t, d = x.shape
    tm = min(XA_TM, seq)
    nt = seq // tm
    row = lambda b, i: (b * nt + i, 0)
    mem = lambda b, i: (b, 0)
    return pl.pallas_call(
        _xattn_kernel,
        grid=(bsz, nt),
        in_specs=[pl.BlockSpec((tm, d), row), _const_spec(g.shape), _const_spec(wq.shape),
                  pl.BlockSpec((MEM_LEN, d), mem), pl.BlockSpec((MEM_LEN, d), mem),
                  _const_spec(wo.shape)],
        out_specs=pl.BlockSpec((tm, d), row),
        out_shape=jax.ShapeDtypeStruct((t, d), F32),
        compiler_params=_params(("parallel", "parallel")),
        name="mem_xattn",
    )(x, g, wq, kmem, vmem, wo)


def _row(v):
    return v.reshape(1, -1).astype(F32)


def _layer(x, mem2d, bsz, seq, p):
    d = D_MODEL
    bf = lambda w: w.astype(BF16)
    masks_np, tri_np = _rw_masks()
    masks = jnp.asarray(masks_np)
    tri = jnp.asarray(tri_np, dtype=BF16)
    gsum = jnp.asarray(_group_ones(PAIR, RW_HEAD_DIM), dtype=BF16)
    gmean = jnp.asarray(_group_ones(PAIR, RW_HEAD_DIM) / RW_HEAD_DIM, dtype=BF16)

    x = _ffn_call(x, _row(p["ffn1_norm"]), bf(p["ffn1_w_gate"]), bf(p["ffn1_w_up"]), bf(p["ffn1_w_down"]),
                  _row(p["ffn1_norm"]), final=False)

    w_in = p["w_in"]
    c0 = RW_COLS
    c1 = c0 + ML_QKVO
    c2 = c1 + 2 * ML_HEADS
    w_if = jnp.pad(w_in[:, c1:c2], ((0, 0), (0, IF_PAD - 2 * ML_HEADS)))
    convw = jnp.concatenate([p["ml_conv_w"].astype(F32), jnp.zeros((SUBLANES - ML_CONV, 2 * ML_WIDTH), F32)], axis=0)
    zrw, zml, zif = _inproj_call(x, _row(p["mix_norm"]), bf(w_in[:, :c0]), bf(w_in[:, c0:c1]), bf(w_if),
                                 _row(p["rw_mu"]), convw, _row(p["ml_conv_b"]), bsz, seq)

    zpad = jnp.zeros((DECAY_LORA, RW_WIDTH), F32)
    w2p = bf(jnp.concatenate([p["rw_w2"], zpad], axis=0))
    a2p = bf(jnp.concatenate([zpad, p["rw_a2"]], axis=0))
    vec = jnp.stack([p["rw_w0"], p["rw_a0"], p["rw_k_k"], p["rw_k_a"], p["rw_r_k"].reshape(-1),
                     jnp.zeros_like(p["rw_w0"]), jnp.zeros_like(p["rw_w0"]), jnp.zeros_like(p["rw_w0"])]).astype(F32)
    g, yi, bonus, gate, m, c = _rw1_call(zrw, w2p, a2p, bf(p["rw_g2"]), vec, gsum, tri, masks)
    gn = jnp.concatenate([jnp.stack([p["rw_gn_w"], p["rw_gn_b"]]).astype(F32),
                          jnp.zeros((SUBLANES - 2, RW_WIDTH), F32)], axis=0)
    y_rw = _rw2_call(g, yi, bonus, gate, m, c, gmean, gn, bsz, seq)

    ifb = jnp.pad(jnp.concatenate([p["ml_i_b"], p["ml_f_b"]]).astype(F32), (0, IF_PAD - 2 * ML_HEADS)).reshape(1, -1)
    tri_ml = jnp.asarray(np.tril(np.ones((ML_CHUNK, ML_CHUNK), np.float32)), dtype=BF16)
    y_ml = _ml_call(zml, zif, ifb, _row(p["ml_norm"]), tri_ml, bsz, seq)

    x = _merge_call(x, _row(p["mix_norm"]), bf(w_in[:, c2:]), y_rw, y_ml,
                    bf(p["w_up_rw"]), bf(p["w_up_ml"]), bf(p["w_out"]))

    kmem, vmem = _memkv_call(mem2d, _row(p["mem_norm"]), bf(p["xa_wkv"]))
    x = _xattn_call(x, _row(p["xa_norm"]), bf(p["xa_wq"]), kmem, vmem, bf(p["xa_wo"]), bsz, seq)
    return x


def kernel(x, mem, ffn1_norm, ffn1_w_gate, ffn1_w_up, ffn1_w_down, mix_norm, w_in, rw_mu, rw_w0, rw_w2, rw_a0,
           rw_a2, rw_g2, rw_k_k, rw_k_a, rw_r_k, rw_gn_w, rw_gn_b, ml_conv_w, ml_conv_b, ml_i_b, ml_f_b, ml_norm,
           w_up_rw, w_up_ml, w_out, xa_norm, mem_norm, xa_wq, xa_wkv, xa_wo, ffn2_norm, ffn2_w_gate, ffn2_w_up,
           ffn2_w_down, final_norm):
    bsz, seq, d = x.shape
    depth = ffn1_norm.shape[0]
    stacked = dict(ffn1_norm=ffn1_norm, ffn1_w_gate=ffn1_w_gate, ffn1_w_up=ffn1_w_up, ffn1_w_down=ffn1_w_down,
                   mix_norm=mix_norm, w_in=w_in, rw_mu=rw_mu, rw_w0=rw_w0, rw_w2=rw_w2, rw_a0=rw_a0, rw_a2=rw_a2,
                   rw_g2=rw_g2, rw_k_k=rw_k_k, rw_k_a=rw_k_a, rw_r_k=rw_r_k, rw_gn_w=rw_gn_w, rw_gn_b=rw_gn_b,
                   ml_conv_w=ml_conv_w, ml_conv_b=ml_conv_b, ml_i_b=ml_i_b, ml_f_b=ml_f_b, ml_norm=ml_norm,
                   w_up_rw=w_up_rw, w_up_ml=w_up_ml, w_out=w_out, xa_norm=xa_norm, mem_norm=mem_norm,
                   xa_wq=xa_wq, xa_wkv=xa_wkv, xa_wo=xa_wo, ffn2_norm=ffn2_norm, ffn2_w_gate=ffn2_w_gate,
                   ffn2_w_up=ffn2_w_up, ffn2_w_down=ffn2_w_down)
    h = x.reshape(bsz * seq, d)
    mem2d = mem.reshape(bsz * mem.shape[1], d)
    bf = lambda w: w.astype(BF16)
    for l in range(depth):
        p = {name: w[l] for name, w in stacked.items()}
        h = _layer(h, mem2d, bsz, seq, p)
        last = l == depth - 1
        h = _ffn_call(h, _row(p["ffn2_norm"]), bf(p["ffn2_w_gate"]), bf(p["ffn2_w_up"]), bf(p["ffn2_w_down"]),
                      _row(final_norm), final=last)
    return h.reshape(bsz, seq, d)
```

```python
import functools
import math

import jax
import jax.numpy as jnp
import numpy as np
from jax import lax
from jax.experimental import pallas as pl
from jax.experimental.pallas import tpu as pltpu

F32 = jnp.float32
BF16 = jnp.bfloat16

D_MODEL = 1024
D_FF = 2816
FFN_HALF = 0.5
NORM_EPS = 1e-5
MEM_LEN = 256

RW_HEADS = 8
RW_HEAD_DIM = 64
RW_WIDTH = 512
DECAY_LORA = 64
AAA_LORA = 64
GATE_LORA = 128
RW_COLS = 3 * RW_WIDTH + DECAY_LORA + AAA_LORA + GATE_LORA
GN_EPS = 64e-5
RW_PAIRS = RW_HEADS // 2
PAIR = 2 * RW_HEAD_DIM

ML_HEADS = 4
ML_HEAD_DIM = 128
ML_WIDTH = 512
ML_CONV = 4
ML_NORM_EPS = 1e-6
ML_QKVO = 4 * ML_WIDTH
IF_PAD = 128

XA_HEADS = 4
XA_HEAD_DIM = 256

CHUNK = 64
LANES = 128
SUBLANES = 8
NEG_BIG = -1e30

VMEM_LIMIT = 56 * 1024 * 1024


def _bdot(a, b):
    return jnp.dot(a.astype(BF16), b.astype(BF16), preferred_element_type=F32)


def _bdot_nt(a, b):
    return lax.dot_general(a.astype(BF16), b.astype(BF16), (((1,), (1,)), ((), ())),
                           preferred_element_type=F32)


def _bdot_tn(a, b):
    return lax.dot_general(a.astype(BF16), b.astype(BF16), (((0,), (0,)), ((), ())),
                           preferred_element_type=F32)


def _split3(x):
    hi = x.astype(BF16)
    r1 = x - hi.astype(F32)
    mid = r1.astype(BF16)
    lo = (r1 - mid.astype(F32)).astype(BF16)
    return hi, mid, lo


def _dot_exact_lhs(c_bf16, x):
    hi, mid, lo = _split3(x)
    f = lambda p: jnp.dot(c_bf16, p, preferred_element_type=F32)
    return f(hi) + f(mid) + f(lo)


def _dot_split2_rhs(x, c_bf16):
    hi = x.astype(BF16)
    lo = (x - hi.astype(F32)).astype(BF16)
    f = lambda p: jnp.dot(p, c_bf16, preferred_element_type=F32)
    return f(hi) + f(lo)


def _per_slab(fn, x, c_bf16):
    n = x.shape[1] // LANES
    return jnp.concatenate([fn(x[:, i * LANES:(i + 1) * LANES], c_bf16) for i in range(n)], axis=1)


def _dot_f32(a, b):
    a_hi = a.astype(BF16)
    a_lo = (a - a_hi.astype(F32)).astype(BF16)
    b_hi = b.astype(BF16)
    b_lo = (b - b_hi.astype(F32)).astype(BF16)
    f = lambda p, q: jnp.dot(p, q, preferred_element_type=F32)
    return f(a_hi, b_hi) + f(a_hi, b_lo) + f(a_lo, b_hi)


def _sigmoid(x):
    return 1.0 / (1.0 + jnp.exp(-x))


def _silu(x):
    return x * _sigmoid(x)


def _rmsnorm(x, g):
    ms = jnp.mean(x * x, axis=-1, keepdims=True)
    return x * lax.rsqrt(ms + NORM_EPS) * g


def _shift_rows(u, carry8, j):
    rolled = pltpu.roll(u, j, axis=0)
    rolled_c = pltpu.roll(carry8, j, axis=0)
    row = lax.broadcasted_iota(jnp.int32, (SUBLANES, u.shape[1]), 0)
    top = jnp.where(row < j, rolled_c, rolled[:SUBLANES])
    return jnp.concatenate([top, rolled[SUBLANES:]], axis=0)


def _shift_rows_via(buf_ref, u, carry8, j):
    rows = u.shape[0]
    buf_ref[0:SUBLANES, :] = carry8
    buf_ref[SUBLANES:, :] = u
    return buf_ref[SUBLANES - j:SUBLANES - j + rows, :]


def _const_spec(shape):
    nd = len(shape)
    return pl.BlockSpec(shape, lambda *_: (0,) * nd, pipeline_mode=pl.Buffered(1))


def _params(sem):
    return pltpu.CompilerParams(dimension_semantics=sem, vmem_limit_bytes=VMEM_LIMIT)


FF_CHUNK = 256
FFN_TM = 1024


def _ffn_kernel(x_ref, g_ref, wg_ref, wu_ref, wd_ref, fg_ref, o_ref, *, final):
    x = x_ref[...]
    h = _rmsnorm(x, g_ref[...]).astype(BF16)
    acc = jnp.zeros(x.shape, F32)
    for c in range(D_FF // FF_CHUNK):
        sl = slice(c * FF_CHUNK, (c + 1) * FF_CHUNK)
        gate = jnp.dot(h, wg_ref[:, sl], preferred_element_type=F32)
        up = jnp.dot(h, wu_ref[:, sl], preferred_element_type=F32)
        act = (_silu(gate) * up).astype(BF16)
        acc = acc + jnp.dot(act, wd_ref[sl, :], preferred_element_type=F32)
    y = x + FFN_HALF * acc
    if final:
        y = _rmsnorm(y, fg_ref[...])
    o_ref[...] = y


def _ffn_call(x, g, wg, wu, wd, fg, final):
    t, d = x.shape
    tm = min(FFN_TM, t)
    return pl.pallas_call(
        functools.partial(_ffn_kernel, final=final),
        grid=(t // tm,),
        in_specs=[pl.BlockSpec((tm, d), lambda i: (i, 0)),
                  _const_spec((1, d)), _const_spec((d, D_FF)), _const_spec((d, D_FF)),
                  _const_spec((D_FF, d)), _const_spec((1, d))],
        out_specs=pl.BlockSpec((tm, d), lambda i: (i, 0)),
        out_shape=jax.ShapeDtypeStruct((t, d), F32),
        compiler_params=_params(("parallel",)),
        name="ffn_final" if final else "ffn",
    )(x, g, wg, wu, wd, fg)


INPROJ_TM = 512
INPROJ_COLS = 256


def _inproj_kernel(x_ref, g_ref, wrw_ref, wml_ref, wif_ref, mu_ref, convw_ref, convb_ref,
                   zrw_ref, zml_ref, zif_ref, carry_rw_ref, carry_ml_ref, buf_a_ref, buf_b_ref):
    @pl.when(pl.program_id(1) == 0)
    def _():
        carry_rw_ref[...] = jnp.zeros(carry_rw_ref.shape, F32)
        carry_ml_ref[...] = jnp.zeros(carry_ml_ref.shape, F32)

    rows = x_ref.shape[0]
    h = _rmsnorm(x_ref[...], g_ref[...]).astype(BF16)

    def conv_stage(cols, scale):
        def finish(u):
            carry = carry_ml_ref[:, cols]
            w0, w1, w2, w3 = (convw_ref[j:j + 1, cols] for j in range(ML_CONV))
            u1 = _shift_rows_via(buf_a_ref, u, carry, 1)
            pair = w2 * u + w3 * u1
            pair_carry = w2 * carry + w3 * pltpu.roll(carry, 1, axis=0)
            conv = convb_ref[:, cols] + w0 * u + w1 * u1 + _shift_rows_via(buf_b_ref, pair, pair_carry, 2)
            carry_ml_ref[:, cols] = u[rows - SUBLANES:]
            zml_ref[:, cols] = _silu(conv) * scale if scale != 1.0 else _silu(conv)
        return wml_ref, cols, finish

    def plain_stage(cols, fn):
        def finish(u):
            zml_ref[:, cols] = fn(u)
        return wml_ref, cols, finish

    def lerp_stage(cols):
        def finish(z):
            prev = _shift_rows_via(buf_a_ref, z, carry_rw_ref[:, cols], 1)
            carry_rw_ref[:, cols] = z[rows - SUBLANES:]
            zrw_ref[:, cols] = z + (prev - z) * mu_ref[:, cols]
        return wrw_ref, cols, finish

    blk = lambda c: slice(c * INPROJ_COLS, (c + 1) * INPROJ_COLS)
    n_head_blocks = ML_WIDTH // INPROJ_COLS
    heavy, light = [], []
    for c in range(ML_QKVO // INPROJ_COLS):
        group = c // n_head_blocks
        if group < 2:
            heavy.append(conv_stage(blk(c), ML_HEAD_DIM ** -0.5 if group == 0 else 1.0))
        else:
            light.append(plain_stage(blk(c), _sigmoid if group == 3 else (lambda u: u)))
    light += [lerp_stage(blk(c)) for c in range(RW_COLS // INPROJ_COLS)]

    per_heavy = -(-len(light) // len(heavy))
    stages = []
    for i, stage in enumerate(heavy):
        stages.append(stage)
        stages += light[i * per_heavy:(i + 1) * per_heavy]
    pending = None
    for w_ref, cols, finish in stages:
        acc = jnp.dot(h, w_ref[:, cols], preferred_element_type=F32)
        if pending is not None:
            pending[0](pending[1])
        pending = (finish, acc)
    zif_ref[...] = jnp.dot(h, wif_ref[...], preferred_element_type=F32)
    pending[0](pending[1])


def _inproj_call(x, g, wrw, wml, wif, mu, convw, convb, bsz, seq):
    t, d = x.shape
    tm = min(INPROJ_TM, seq)
    nt = seq // tm
    row = lambda b, i: (b * nt + i, 0)
    return pl.pallas_call(
        _inproj_kernel,
        grid=(bsz, nt),
        in_specs=[pl.BlockSpec((tm, d), row),
                  _const_spec((1, d)), _const_spec((d, RW_COLS)), _const_spec((d, ML_QKVO)),
                  _const_spec((d, IF_PAD)), _const_spec((1, RW_COLS)),
                  _const_spec(convw.shape), _const_spec(convb.shape)],
        out_specs=[pl.BlockSpec((tm, RW_COLS), row), pl.BlockSpec((tm, ML_QKVO), row),
                   pl.BlockSpec((tm, IF_PAD), row)],
        out_shape=[jax.ShapeDtypeStruct((t, RW_COLS), F32), jax.ShapeDtypeStruct((t, ML_QKVO), F32),
                   jax.ShapeDtypeStruct((t, IF_PAD), F32)],
        scratch_shapes=[pltpu.VMEM((SUBLANES, RW_COLS), F32), pltpu.VMEM((SUBLANES, 2 * ML_WIDTH), F32),
                        pltpu.VMEM((SUBLANES + tm, INPROJ_COLS), F32),
                        pltpu.VMEM((SUBLANES + tm, INPROJ_COLS), F32)],
        compiler_params=_params(("parallel", "arbitrary")),
        name="inproj",
    )(x, g, wrw, wml, wif, mu, convw, convb)


RW1_CHUNKS = 8
RW1_GROUP = 16
INV_LEVELS = 6


def _rw_masks():
    t = np.arange(CHUNK)[:, None]
    s = np.tile(np.arange(CHUNK), 2)[None, :]
    levels = []
    for k in range(INV_LEVELS):
        size = 1 << k
        levels.append(((t // (2 * size)) == (s // (2 * size))) & ((t // size) % 2 == 1) & ((s // size) % 2 == 0))
    m = np.stack([s < t, s <= t, s == t] + levels).astype(np.float32)
    tri = (np.arange(CHUNK)[:, None] >= np.arange(CHUNK)[None, :]).astype(np.float32)
    return m, tri


def _group_ones(width, group):
    i = np.arange(width)
    return (i[:, None] // group == i[None, :] // group).astype(np.float32)


def _rw1_kernel(z_ref, w2_ref, a2_ref, g2_ref, vec_ref, gsum_ref, tri_ref, mask_ref,
                g_ref, yi_ref, bonus_ref, gate_ref, m_ref, c_ref):
    z = z_ref[...]
    rows = z.shape[0]
    r = z[:, 0:RW_WIDTH]
    k = z[:, RW_WIDTH:2 * RW_WIDTH]
    v = z[:, 2 * RW_WIDTH:3 * RW_WIDTH]
    lora_in = z[:, 3 * RW_WIDTH:3 * RW_WIDTH + DECAY_LORA + AAA_LORA]
    g_d = z[:, 3 * RW_WIDTH + DECAY_LORA + AAA_LORA:]
    w0 = vec_ref[0:1, :]
    a0 = vec_ref[1:2, :]
    k_k = vec_ref[2:3, :]
    k_a = vec_ref[3:4, :]
    r_k = vec_ref[4:5, :]

    w_raw = w0 + _bdot(jnp.tanh(lora_in), w2_ref[...])
    logw = -math.exp(-0.5) * _sigmoid(w_raw)
    a = _sigmoid(a0 + _bdot(lora_in, a2_ref[...]))
    gate_ref[...] = _bdot(_sigmoid(g_d), g2_ref[...])

    gsum = gsum_ref[...]
    kk = k * k_k
    kk = kk / jnp.maximum(jnp.sqrt(_per_slab(_bdot, kk * kk, gsum)), 1e-12)
    k_mod = k * (1.0 + (a - 1.0) * k_a)
    b = kk * a
    bonus_ref[...] = _per_slab(_dot_split2_rhs, r * k_mod * r_k, gsum) * v

    lane = lax.broadcasted_iota(jnp.int32, (CHUNK, PAIR), 1)
    head0 = lane < RW_HEAD_DIM
    strict = mask_ref[0]
    incl = mask_ref[1]
    eye = mask_ref[2]
    tri = tri_ref[...]

    def stack(x):
        return jnp.concatenate([jnp.where(head0, x, 0.0), jnp.where(head0, 0.0, x)], axis=0).astype(BF16)

    ops = {}
    for c in range(rows // CHUNK):
        rs = slice(c * CHUNK, (c + 1) * CHUNK)
        lw = logw[rs]
        cum = _dot_exact_lhs(tri, lw)
        cum_last = cum[CHUNK - 1:CHUNK]
        w_incl = jnp.exp(cum)
        w_inv = jnp.exp(-cum)
        w_prev = jnp.exp(cum - lw)
        w_end = jnp.exp(cum_last - cum)
        w_last = jnp.exp(cum_last)
        full = dict(at=-kk[rs] * w_prev, rt=r[rs] * w_incl, bt=b[rs] * w_inv, kt=k_mod[rs] * w_inv,
                    be=b[rs] * w_end, ke=k_mod[rs] * w_end, v2=v[rs])
        for p in range(RW_PAIRS):
            ls = slice(p * PAIR, (p + 1) * PAIR)
            item = {name: val[:, ls] for name, val in full.items()}
            item["w_last"] = w_last[:, ls]
            ops[(c, p)] = item

    keys = list(ops)
    for g0 in range(0, len(keys), RW1_GROUP):
        group = keys[g0:g0 + RW1_GROUP]
        n_ab, a_ak, a_rb, a_rk, t_inv, sv = {}, {}, {}, {}, {}, {}
        for key in group:
            o = ops[key]
            prod = _bdot_nt(jnp.concatenate([o["at"], o["rt"]], axis=0),
                            jnp.concatenate([stack(o["bt"]), stack(o["kt"])], axis=0))
            n_ab[key] = prod[:CHUNK, :PAIR] * strict
            a_ak[key] = prod[:CHUNK, PAIR:] * strict
            a_rb[key] = prod[CHUNK:, :PAIR] * incl
            a_rk[key] = prod[CHUNK:, PAIR:] * incl
            sv[key] = stack(o["v2"])
            t_inv[key] = eye + n_ab[key] * mask_ref[3]
        for lvl in range(1, INV_LEVELS):
            half = {key: _bdot(t_inv[key], stack(n_ab[key] * mask_ref[3 + lvl])) for key in group}
            t_inv = {key: t_inv[key] + _bdot(half[key], stack(t_inv[key])) for key in group}
        akv = {key: _bdot(a_ak[key], sv[key]) for key in group}
        pq = {key: _bdot(t_inv[key], jnp.concatenate([stack(ops[key]["at"]), stack(akv[key])], axis=1))
              for key in group}
        gy = {key: _bdot(a_rb[key], jnp.concatenate([stack(pq[key][:, :PAIR]), stack(pq[key][:, PAIR:])], axis=1))
              for key in group}
        ark_v = {key: _bdot(a_rk[key], sv[key]) for key in group}
        mc = {key: _bdot_tn(ops[key]["be"], pq[key]) for key in group}
        ke_v = {key: _bdot_tn(ops[key]["ke"], ops[key]["v2"]) for key in group}
        for key in group:
            c, p = key
            rs = slice(c * CHUNK, (c + 1) * CHUNK)
            ls = slice(p * PAIR, (p + 1) * PAIR)
            g_ref[rs, ls] = ops[key]["rt"] + gy[key][:, :PAIR]
            yi_ref[rs, ls] = gy[key][:, PAIR:] + ark_v[key]
            cv = mc[key][:, PAIR:] + ke_v[key]
            m_ref[c, p] = eye * ops[key]["w_last"] + jnp.where(head0, mc[key][:CHUNK, :PAIR], mc[key][CHUNK:, :PAIR])
            c_ref[c, p] = jnp.where(head0, cv[:CHUNK], cv[CHUNK:])


def _rw1_call(zrw, w2p, a2p, g2, vec, gsum, tri, masks):
    t = zrw.shape[0]
    rows = RW1_CHUNKS * CHUNK
    nchunks = t // CHUNK
    tok = lambda i: (i, 0)
    blk = lambda i: (i, 0, 0, 0)
    tok_spec = pl.BlockSpec((rows, RW_WIDTH), tok)
    mc_spec = pl.BlockSpec((RW1_CHUNKS, RW_PAIRS, CHUNK, PAIR), blk)
    tok_shape = jax.ShapeDtypeStruct((t, RW_WIDTH), F32)
    mc_shape = jax.ShapeDtypeStruct((nchunks, RW_PAIRS, CHUNK, PAIR), F32)
    return pl.pallas_call(
        _rw1_kernel,
        grid=(t // rows,),
        in_specs=[pl.BlockSpec((rows, RW_COLS), tok),
                  _const_spec(w2p.shape), _const_spec(a2p.shape), _const_spec(g2.shape),
                  _const_spec(vec.shape), _const_spec(gsum.shape), _const_spec(tri.shape),
                  _const_spec(masks.shape)],
        out_specs=[tok_spec, tok_spec, tok_spec, tok_spec, mc_spec, mc_spec],
        out_shape=[tok_shape, tok_shape, tok_shape, tok_shape, mc_shape, mc_shape],
        compiler_params=_params(("parallel",)),
        name="rwkv_chunk_local",
    )(zrw, w2p, a2p, g2, vec, gsum, tri, masks)


RW2_CHUNKS = 8
RW2_BATCH = 2


def _rw2_kernel(g_ref, yi_ref, bonus_ref, gate_ref, m_ref, c_ref, gmean_ref, gn_ref, y_ref, h_ref):
    @pl.when(pl.program_id(1) == 0)
    def _():
        h_ref[...] = jnp.zeros(h_ref.shape, F32)

    lane = lax.broadcasted_iota(jnp.int32, (CHUNK, PAIR), 1)
    head0 = lane < RW_HEAD_DIM

    def block_diag(x):
        return jnp.concatenate([jnp.where(head0, x, 0.0), jnp.where(head0, 0.0, x)], axis=0)

    nb = g_ref.shape[0]
    chains = [(b, p) for b in range(nb) for p in range(RW_PAIRS)]
    lsl = [slice(p * PAIR, (p + 1) * PAIR) for p in range(RW_PAIRS)]
    state = {(b, p): h_ref[b, p] for b, p in chains}
    ys = {}
    for c in range(RW2_CHUNKS):
        rs = slice(c * CHUNK, (c + 1) * CHUNK)
        for b, p in chains:
            ys[b, c, p] = _bdot(g_ref[b, rs, lsl[p]], state[b, p]) + yi_ref[b, rs, lsl[p]]
        state = {(b, p): _dot_f32(block_diag(m_ref[b, c, p]), state[b, p]) + block_diag(c_ref[b, c, p])
                 for b, p in chains}
    for b, p in chains:
        h_ref[b, p] = state[b, p]
    gmean = gmean_ref[...]
    for b in range(nb):
        y = jnp.concatenate([jnp.concatenate([ys[b, c, p] for p in range(RW_PAIRS)], axis=1)
                             for c in range(RW2_CHUNKS)], axis=0)
        mu = _per_slab(_dot_split2_rhs, y, gmean)
        d = y - mu
        var = _per_slab(_bdot, d * d, gmean)
        y = d * lax.rsqrt(var + GN_EPS) * gn_ref[0:1, :] + gn_ref[1:2, :]
        y_ref[b] = ((y + bonus_ref[b]) * gate_ref[b]).astype(BF16)


def _rw2_call(g, yi, bonus, gate, m, c, gmean, gn, bsz, seq):
    t = g.shape[0]
    rows = RW2_CHUNKS * CHUNK
    nt = seq // rows
    nb = RW2_BATCH if bsz % RW2_BATCH == 0 else 1
    per_batch = lambda a: a.reshape((bsz, a.shape[0] // bsz) + a.shape[1:])
    tok_spec = pl.BlockSpec((nb, rows, RW_WIDTH), lambda b, i: (b, i, 0))
    mc_spec = pl.BlockSpec((nb, RW2_CHUNKS, RW_PAIRS, CHUNK, PAIR), lambda b, i: (b, i, 0, 0, 0))
    y = pl.pallas_call(
        _rw2_kernel,
        grid=(bsz // nb, nt),
        in_specs=[tok_spec, tok_spec, tok_spec, tok_spec, mc_spec, mc_spec,
                  _const_spec(gmean.shape), _const_spec(gn.shape)],
        out_specs=tok_spec,
        out_shape=jax.ShapeDtypeStruct((bsz, seq, RW_WIDTH), BF16),
        scratch_shapes=[pltpu.VMEM((nb, RW_PAIRS, PAIR, PAIR), F32)],
        compiler_params=_params(("parallel", "arbitrary")),
        name="rwkv_scan",
    )(per_batch(g), per_batch(yi), per_batch(bonus), per_batch(gate), per_batch(m), per_batch(c), gmean, gn)
    return y.reshape(t, RW_WIDTH)


ML_CHUNK = 128
ML_STEP_CHUNKS = 4
ML_BATCH = 1


def _log_sigmoid(x):
    return jnp.minimum(x, 0.0) - jnp.log(1.0 + jnp.exp(-jnp.abs(x)))


def _ml_kernel(z_ref, zif_ref, ifb_ref, norm_ref, tri_ref, y_ref, state_ref, m_ref):
    @pl.when(pl.program_id(1) == 0)
    def _():
        state_ref[...] = jnp.zeros(state_ref.shape, F32)
        m_ref[...] = jnp.zeros(m_ref.shape, F32)

    nb, rows = z_ref.shape[0], z_ref.shape[1]
    q_all = [z_ref[b, :, 0:ML_WIDTH] for b in range(nb)]
    k_all = [z_ref[b, :, ML_WIDTH:2 * ML_WIDTH] for b in range(nb)]
    v_all = [z_ref[b, :, 2 * ML_WIDTH:3 * ML_WIDTH] for b in range(nb)]
    ogate_all = [z_ref[b, :, 3 * ML_WIDTH:4 * ML_WIDTH] for b in range(nb)]

    pre = [zif_ref[b] + ifb_ref[...] for b in range(nb)]
    logf = [_log_sigmoid(pre[b]) for b in range(nb)]
    tri = tri_ref[...]
    lane = lax.broadcasted_iota(jnp.int32, (ML_CHUNK, LANES), 1)
    tq = lax.broadcasted_iota(jnp.int32, (ML_CHUNK, ML_CHUNK), 0)
    tk = lax.broadcasted_iota(jnp.int32, (ML_CHUNK, ML_CHUNK), 1)
    causal = tk <= tq
    ones = jnp.ones((ML_CHUNK, ML_HEAD_DIM), F32)
    dense = (ML_CHUNK, ML_CHUNK)

    chains = [(b, h) for b in range(nb) for h in range(ML_HEADS)]
    lsl = [slice(h * ML_HEAD_DIM, (h + 1) * ML_HEAD_DIM) for h in range(ML_HEADS)]
    m_run = {ch: m_ref[ch] for ch in chains}
    state_run = {ch: state_ref[ch] for ch in chains}
    ys = {b: [] for b in range(nb)}
    for c in range(rows // ML_CHUNK):
        rs = slice(c * ML_CHUNK, (c + 1) * ML_CHUNK)
        z, z_t = [], []
        for b in range(nb):
            bcum = _dot_exact_lhs(tri, logf[b][rs])
            z.append(jnp.where(lane < ML_HEADS, pre[b][rs], bcum))
            z_t.append(z[b].T)
        b_col = {(b, h): jnp.broadcast_to(z[b][:, ML_HEADS + h:ML_HEADS + h + 1], dense)
                 for b, h in chains}
        ib_col = {(b, h): jnp.broadcast_to(z[b][:, h:h + 1] - z[b][:, ML_HEADS + h:ML_HEADS + h + 1], dense)
                  for b, h in chains}
        ib_row = {(b, h): z_t[b][h:h + 1, :] - z_t[b][ML_HEADS + h:ML_HEADS + h + 1, :]
                  for b, h in chains}
        b_end = {ch: b_col[ch][ML_CHUNK - 1:ML_CHUNK] for ch in chains}
        m_prev = m_run
        state = state_run
        q = {(b, h): q_all[b][rs, lsl[h]] for b, h in chains}
        k = {(b, h): k_all[b][rs, lsl[h]] for b, h in chains}
        v = {(b, h): v_all[b][rs, lsl[h]] for b, h in chains}
        s_qk = {ch: _bdot_nt(q[ch], k[ch]) for ch in chains}
        inter = {ch: _bdot(q[ch], state[ch]) for ch in chains}

        log_d = {ch: jnp.where(causal, b_col[ch] + ib_row[ch], NEG_BIG) for ch in chains}
        log_inter = {ch: b_col[ch] + m_prev[ch] for ch in chains}
        m_row = {ch: jnp.maximum(log_inter[ch], jnp.max(log_d[ch], axis=-1, keepdims=True)) for ch in chains}
        p = {ch: jnp.exp(log_d[ch] - m_row[ch]) * s_qk[ch] for ch in chains}
        intra = {ch: _bdot(p[ch], jnp.concatenate([v[ch], ones], axis=1)) for ch in chains}

        log_w_end = {ch: ib_col[ch] + b_end[ch] for ch in chains}
        m_chunk = {ch: jnp.max(log_w_end[ch], axis=0, keepdims=True) for ch in chains}
        w_end = {ch: jnp.exp(log_w_end[ch] - m_chunk[ch]) for ch in chains}
        kv = {ch: _bdot_tn(k[ch], jnp.concatenate([w_end[ch] * v[ch], w_end[ch]], axis=1)) for ch in chains}
        m_run, state_run = {}, {}
        for ch in chains:
            m_new = jnp.maximum(b_end[ch] + m_prev[ch], m_chunk[ch])
            dec = jnp.exp(b_end[ch] + m_prev[ch] - m_new)
            sc = jnp.exp(m_chunk[ch] - m_new)
            state_run[ch] = (jnp.concatenate([dec, dec], axis=1) * state[ch]
                             + jnp.concatenate([sc, sc], axis=1) * kv[ch])
            m_run[ch] = m_new

        outs = {}
        for ch in chains:
            b, h = ch
            w_inter = jnp.exp(log_inter[ch] - m_row[ch])
            num = intra[ch][:, :ML_HEAD_DIM] + w_inter * inter[ch][:, :ML_HEAD_DIM]
            den = intra[ch][:, ML_HEAD_DIM:] + w_inter * inter[ch][:, ML_HEAD_DIM:]
            hh = num / jnp.maximum(jnp.abs(den), jnp.exp(-m_row[ch]))
            mu = jnp.mean(hh, axis=-1, keepdims=True)
            d = hh - mu
            var = jnp.mean(d * d, axis=-1, keepdims=True)
            hn = d * lax.rsqrt(var + ML_NORM_EPS)
            outs[ch] = hn * norm_ref[:, lsl[h]] * ogate_all[b][rs, lsl[h]]
        for b in range(nb):
            ys[b].append(jnp.concatenate([outs[b, h] for h in range(ML_HEADS)], axis=1).astype(BF16))
    for b in range(nb):
        y_ref[b] = jnp.concatenate(ys[b], axis=0)
    for ch in chains:
        state_ref[ch] = state_run[ch]
        m_ref[ch] = m_run[ch]


def _ml_call(zml, zif, ifb, norm, tri, bsz, seq):
    t = zml.shape[0]
    rows = ML_STEP_CHUNKS * ML_CHUNK
    nt = seq // rows
    nb = ML_BATCH if bsz % ML_BATCH == 0 else 1
    tok = lambda b, i: (b, i, 0)
    y = pl.pallas_call(
        _ml_kernel,
        grid=(bsz // nb, nt),
        in_specs=[pl.BlockSpec((nb, rows, ML_QKVO), tok), pl.BlockSpec((nb, rows, IF_PAD), tok),
                  _const_spec(ifb.shape), _const_spec(norm.shape), _const_spec(tri.shape)],
        out_specs=pl.BlockSpec((nb, rows, ML_WIDTH), tok),
        out_shape=jax.ShapeDtypeStruct((bsz, seq, ML_WIDTH), BF16),
        scratch_shapes=[pltpu.VMEM((nb, ML_HEADS, ML_HEAD_DIM, 2 * ML_HEAD_DIM), F32),
                        pltpu.VMEM((nb, ML_HEADS, 1, LANES), F32)],
        compiler_params=_params(("parallel", "arbitrary")),
        name="mlstm",
    )(zml.reshape(bsz, seq, ML_QKVO), zif.reshape(bsz, seq, IF_PAD), ifb, norm, tri)
    return y.reshape(t, ML_WIDTH)


MERGE_TM = 512


def _merge_kernel(x_ref, g_ref, wgt_ref, yrw_ref, yml_ref, wur_ref, wum_ref, wo_ref, o_ref):
    x = x_ref[...]
    d = x.shape[1]
    h = _rmsnorm(x, g_ref[...]).astype(BF16)
    gates = _sigmoid(jnp.dot(h, wgt_ref[...], preferred_element_type=F32))
    up_rw = jnp.dot(yrw_ref[...], wur_ref[...], preferred_element_type=F32)
    up_ml = jnp.dot(yml_ref[...], wum_ref[...], preferred_element_type=F32)
    merged = gates[:, 0:d] * up_rw + gates[:, d:] * up_ml
    o_ref[...] = x + _bdot(merged, wo_ref[...])


def _merge_call(x, g, wgt, yrw, yml, wur, wum, wo):
    t, d = x.shape
    tm = min(MERGE_TM, t)
    row = lambda i: (i, 0)
    return pl.pallas_call(
        _merge_kernel,
        grid=(t // tm,),
        in_specs=[pl.BlockSpec((tm, d), row), _const_spec(g.shape), _const_spec(wgt.shape),
                  pl.BlockSpec((tm, RW_WIDTH), row), pl.BlockSpec((tm, ML_WIDTH), row),
                  _const_spec(wur.shape), _const_spec(wum.shape), _const_spec(wo.shape)],
        out_specs=pl.BlockSpec((tm, d), row),
        out_shape=jax.ShapeDtypeStruct((t, d), F32),
        compiler_params=_params(("parallel",)),
        name="merge",
    )(x, g, wgt, yrw, yml, wur, wum, wo)


def _memkv_kernel(mem_ref, g_ref, wkv_ref, k_ref, v_ref):
    d = mem_ref.shape[1]
    hm = _rmsnorm(mem_ref[...], g_ref[...])
    kv = _bdot(hm, wkv_ref[...])
    k_ref[...] = kv[:, 0:d].astype(BF16)
    v_ref[...] = kv[:, d:].astype(BF16)


def _memkv_call(mem2d, g, wkv):
    n, d = mem2d.shape
    row = lambda i: (i, 0)
    return pl.pallas_call(
        _memkv_kernel,
        grid=(n // MEM_LEN,),
        in_specs=[pl.BlockSpec((MEM_LEN, d), row), _const_spec(g.shape), _const_spec(wkv.shape)],
        out_specs=[pl.BlockSpec((MEM_LEN, d), row), pl.BlockSpec((MEM_LEN, d), row)],
        out_shape=[jax.ShapeDtypeStruct((n, d), BF16), jax.ShapeDtypeStruct((n, d), BF16)],
        compiler_params=_params(("parallel",)),
        name="mem_kv",
    )(mem2d, g, wkv)


XA_TM = 512


def _xattn_kernel(x_ref, g_ref, wq_ref, k_ref, v_ref, wo_ref, o_ref):
    x = x_ref[...]
    hq = _rmsnorm(x, g_ref[...])
    q = _bdot(hq, wq_ref[...]) * (XA_HEAD_DIM ** -0.5)
    lsl = [slice(h * XA_HEAD_DIM, (h + 1) * XA_HEAD_DIM) for h in range(XA_HEADS)]

    def softmax(s):
        e = jnp.exp(s - jnp.max(s, axis=-1, keepdims=True))
        return e / jnp.sum(e, axis=-1, keepdims=True)

    outs = []
    s_prev = _bdot_nt(q[:, lsl[0]], k_ref[:, lsl[0]])
    for h in range(XA_HEADS):
        s_next = _bdot_nt(q[:, lsl[h + 1]], k_ref[:, lsl[h + 1]]) if h + 1 < XA_HEADS else None
        outs.append(_bdot(softmax(s_prev), v_ref[:, lsl[h]]))
        s_prev = s_next
    o = jnp.concatenate(outs, axis=1)
    o_ref[...] = x + _bdot(o, wo_ref[...])


def _xattn_call(x, g, wq, kmem, vmem, wo, bsz, seq):
    t, d = x.shape
    tm = min(XA_TM, seq)
    nt = seq // tm
    row = lambda b, i: (b * nt + i, 0)
    mem = lambda b, i: (b, 0)
    return pl.pallas_call(
        _xattn_kernel,
        grid=(bsz, nt),
        in_specs=[pl.BlockSpec((tm, d), row), _const_spec(g.shape), _const_spec(wq.shape),
                  pl.BlockSpec((MEM_LEN, d), mem), pl.BlockSpec((MEM_LEN, d), mem),
                  _const_spec(wo.shape)],
        out_specs=pl.BlockSpec((tm, d), row),
        out_shape=jax.ShapeDtypeStruct((t, d), F32),
        compiler_params=_params(("parallel", "parallel")),
        name="mem_xattn",
    )(x, g, wq, kmem, vmem, wo)


def _row(v):
    return v.reshape(1, -1).astype(F32)


def _layer(x, mem2d, bsz, seq, p):
    d = D_MODEL
    bf = lambda w: w.astype(BF16)
    masks_np, tri_np = _rw_masks()
    masks = jnp.asarray(masks_np)
    tri = jnp.asarray(tri_np, dtype=BF16)
    gsum = jnp.asarray(_group_ones(PAIR, RW_HEAD_DIM), dtype=BF16)
    gmean = jnp.asarray(_group_ones(PAIR, RW_HEAD_DIM) / RW_HEAD_DIM, dtype=BF16)

    x = _ffn_call(x, _row(p["ffn1_norm"]), bf(p["ffn1_w_gate"]), bf(p["ffn1_w_up"]), bf(p["ffn1_w_down"]),
                  _row(p["ffn1_norm"]), final=False)

    w_in = p["w_in"]
    c0 = RW_COLS
    c1 = c0 + ML_QKVO
    c2 = c1 + 2 * ML_HEADS
    w_if = jnp.pad(w_in[:, c1:c2], ((0, 0), (0, IF_PAD - 2 * ML_HEADS)))
    convw = jnp.concatenate([p["ml_conv_w"].astype(F32), jnp.zeros((SUBLANES - ML_CONV, 2 * ML_WIDTH), F32)], axis=0)
    zrw, zml, zif = _inproj_call(x, _row(p["mix_norm"]), bf(w_in[:, :c0]), bf(w_in[:, c0:c1]), bf(w_if),
                                 _row(p["rw_mu"]), convw, _row(p["ml_conv_b"]), bsz, seq)

    zpad = jnp.zeros((DECAY_LORA, RW_WIDTH), F32)
    w2p = bf(jnp.concatenate([p["rw_w2"], zpad], axis=0))
    a2p = bf(jnp.concatenate([zpad, p["rw_a2"]], axis=0))
    vec = jnp.stack([p["rw_w0"], p["rw_a0"], p["rw_k_k"], p["rw_k_a"], p["rw_r_k"].reshape(-1),
                     jnp.zeros_like(p["rw_w0"]), jnp.zeros_like(p["rw_w0"]), jnp.zeros_like(p["rw_w0"])]).astype(F32)
    g, yi, bonus, gate, m, c = _rw1_call(zrw, w2p, a2p, bf(p["rw_g2"]), vec, gsum, tri, masks)
    gn = jnp.concatenate([jnp.stack([p["rw_gn_w"], p["rw_gn_b"]]).astype(F32),
                          jnp.zeros((SUBLANES - 2, RW_WIDTH), F32)], axis=0)
    y_rw = _rw2_call(g, yi, bonus, gate, m, c, gmean, gn, bsz, seq)

    ifb = jnp.pad(jnp.concatenate([p["ml_i_b"], p["ml_f_b"]]).astype(F32), (0, IF_PAD - 2 * ML_HEADS)).reshape(1, -1)
    tri_ml = jnp.asarray(np.tril(np.ones((ML_CHUNK, ML_CHUNK), np.float32)), dtype=BF16)
    y_ml = _ml_call(zml, zif, ifb, _row(p["ml_norm"]), tri_ml, bsz, seq)

    x = _merge_call(x, _row(p["mix_norm"]), bf(w_in[:, c2:]), y_rw, y_ml,
                    bf(p["w_up_rw"]), bf(p["w_up_ml"]), bf(p["w_out"]))

    kmem, vmem = _memkv_call(mem2d, _row(p["mem_norm"]), bf(p["xa_wkv"]))
    x = _xattn_call(x, _row(p["xa_norm"]), bf(p["xa_wq"]), kmem, vmem, bf(p["xa_wo"]), bsz, seq)
    return x


def kernel(x, mem, ffn1_norm, ffn1_w_gate, ffn1_w_up, ffn1_w_down, mix_norm, w_in, rw_mu, rw_w0, rw_w2, rw_a0,
           rw_a2, rw_g2, rw_k_k, rw_k_a, rw_r_k, rw_gn_w, rw_gn_b, ml_conv_w, ml_conv_b, ml_i_b, ml_f_b, ml_norm,
           w_up_rw, w_up_ml, w_out, xa_norm, mem_norm, xa_wq, xa_wkv, xa_wo, ffn2_norm, ffn2_w_gate, ffn2_w_up,
           ffn2_w_down, final_norm):
    bsz, seq, d = x.shape
    depth = ffn1_norm.shape[0]
    stacked = dict(ffn1_norm=ffn1_norm, ffn1_w_gate=ffn1_w_gate, ffn1_w_up=ffn1_w_up, ffn1_w_down=ffn1_w_down,
                   mix_norm=mix_norm, w_in=w_in, rw_mu=rw_mu, rw_w0=rw_w0, rw_w2=rw_w2, rw_a0=rw_a0, rw_a2=rw_a2,
                   rw_g2=rw_g2, rw_k_k=rw_k_k, rw_k_a=rw_k_a, rw_r_k=rw_r_k, rw_gn_w=rw_gn_w, rw_gn_b=rw_gn_b,
                   ml_conv_w=ml_conv_w, ml_conv_b=ml_conv_b, ml_i_b=ml_i_b, ml_f_b=ml_f_b, ml_norm=ml_norm,
                   w_up_rw=w_up_rw, w_up_ml=w_up_ml, w_out=w_out, xa_norm=xa_norm, mem_norm=mem_norm,
                   xa_wq=xa_wq, xa_wkv=xa_wkv, xa_wo=xa_wo, ffn2_norm=ffn2_norm, ffn2_w_gate=ffn2_w_gate,
                   ffn2_w_up=ffn2_w_up, ffn2_w_down=ffn2_w_down)
    h = x.reshape(bsz * seq, d)
    mem2d = mem.reshape(bsz * mem.shape[1], d)
    bf = lambda w: w.astype(BF16)
    for l in range(depth):
        p = {name: w[l] for name, w in stacked.items()}
        h = _layer(h, mem2d, bsz, seq, p)
        last = l == depth - 1
        h = _ffn_call(h, _row(p["ffn2_norm"]), bf(p["ffn2_w_gate"]), bf(p["ffn2_w_up"]), bf(p["ffn2_w_down"]),
                      _row(final_norm), final=last)
    return h.reshape(bsz, seq, d)
```

```python
import functools
import math

import jax
import jax.numpy as jnp
import numpy as np
from jax import lax
from jax.experimental import pallas as pl
from jax.experimental.pallas import tpu as pltpu

F32 = jnp.float32
BF16 = jnp.bfloat16

D_MODEL = 1024
D_FF = 2816
FFN_HALF = 0.5
NORM_EPS = 1e-5
MEM_LEN = 256

RW_HEADS = 8
RW_HEAD_DIM = 64
RW_WIDTH = 512
DECAY_LORA = 64
AAA_LORA = 64
GATE_LORA = 128
RW_COLS = 3 * RW_WIDTH + DECAY_LORA + AAA_LORA + GATE_LORA
GN_EPS = 64e-5
RW_PAIRS = RW_HEADS // 2
PAIR = 2 * RW_HEAD_DIM

ML_HEADS = 4
ML_HEAD_DIM = 128
ML_WIDTH = 512
ML_CONV = 4
ML_NORM_EPS = 1e-6
ML_QKVO = 4 * ML_WIDTH
IF_PAD = 128

XA_HEADS = 4
XA_HEAD_DIM = 256

CHUNK = 64
LANES = 128
SUBLANES = 8
NEG_BIG = -1e30

VMEM_LIMIT = 56 * 1024 * 1024


def _bdot(a, b):
    return jnp.dot(a.astype(BF16), b.astype(BF16), preferred_element_type=F32)


def _bdot_nt(a, b):
    return lax.dot_general(a.astype(BF16), b.astype(BF16), (((1,), (1,)), ((), ())),
                           preferred_element_type=F32)


def _bdot_tn(a, b):
    return lax.dot_general(a.astype(BF16), b.astype(BF16), (((0,), (0,)), ((), ())),
                           preferred_element_type=F32)


def _split3(x):
    hi = x.astype(BF16)
    r1 = x - hi.astype(F32)
    mid = r1.astype(BF16)
    lo = (r1 - mid.astype(F32)).astype(BF16)
    return hi, mid, lo


def _dot_exact_lhs(c_bf16, x):
    hi, mid, lo = _split3(x)
    f = lambda p: jnp.dot(c_bf16, p, preferred_element_type=F32)
    return f(hi) + f(mid) + f(lo)


def _dot_split2_rhs(x, c_bf16):
    hi = x.astype(BF16)
    lo = (x - hi.astype(F32)).astype(BF16)
    f = lambda p: jnp.dot(p, c_bf16, preferred_element_type=F32)
    return f(hi) + f(lo)


def _per_slab(fn, x, c_bf16):
    n = x.shape[1] // LANES
    return jnp.concatenate([fn(x[:, i * LANES:(i + 1) * LANES], c_bf16) for i in range(n)], axis=1)


def _dot_f32(a, b):
    a_hi = a.astype(BF16)
    a_lo = (a - a_hi.astype(F32)).astype(BF16)
    b_hi = b.astype(BF16)
    b_lo = (b - b_hi.astype(F32)).astype(BF16)
    f = lambda p, q: jnp.dot(p, q, preferred_element_type=F32)
    return f(a_hi, b_hi) + f(a_hi, b_lo) + f(a_lo, b_hi)


def _sigmoid(x):
    return 1.0 / (1.0 + jnp.exp(-x))


def _silu(x):
    return x * _sigmoid(x)


def _rmsnorm(x, g):
    ms = jnp.mean(x * x, axis=-1, keepdims=True)
    return x * lax.rsqrt(ms + NORM_EPS) * g


def _shift_rows(u, carry8, j):
    rolled = pltpu.roll(u, j, axis=0)
    rolled_c = pltpu.roll(carry8, j, axis=0)
    row = lax.broadcasted_iota(jnp.int32, (SUBLANES, u.shape[1]), 0)
    top = jnp.where(row < j, rolled_c, rolled[:SUBLANES])
    return jnp.concatenate([top, rolled[SUBLANES:]], axis=0)


def _shift_rows_via(buf_ref, u, carry8, j):
    rows = u.shape[0]
    buf_ref[0:SUBLANES, :] = carry8
    buf_ref[SUBLANES:, :] = u
    return buf_ref[SUBLANES - j:SUBLANES - j + rows, :]


def _const_spec(shape):
    nd = len(shape)
    return pl.BlockSpec(shape, lambda *_: (0,) * nd, pipeline_mode=pl.Buffered(1))


def _params(sem):
    return pltpu.CompilerParams(dimension_semantics=sem, vmem_limit_bytes=VMEM_LIMIT)


FF_CHUNK = 256
FFN_TM = 1024


def _ffn_kernel(x_ref, g_ref, wg_ref, wu_ref, wd_ref, fg_ref, o_ref, *, final):
    x = x_ref[...]
    h = _rmsnorm(x, g_ref[...]).astype(BF16)
    acc = jnp.zeros(x.shape, F32)
    for c in range(D_FF // FF_CHUNK):
        sl = slice(c * FF_CHUNK, (c + 1) * FF_CHUNK)
        gate = jnp.dot(h, wg_ref[:, sl], preferred_element_type=F32)
        up = jnp.dot(h, wu_ref[:, sl], preferred_element_type=F32)
        act = (_silu(gate) * up).astype(BF16)
        acc = acc + jnp.dot(act, wd_ref[sl, :], preferred_element_type=F32)
    y = x + FFN_HALF * acc
    if final:
        y = _rmsnorm(y, fg_ref[...])
    o_ref[...] = y


def _ffn_call(x, g, wg, wu, wd, fg, final):
    t, d = x.shape
    tm = min(FFN_TM, t)
    return pl.pallas_call(
        functools.partial(_ffn_kernel, final=final),
        grid=(t // tm,),
        in_specs=[pl.BlockSpec((tm, d), lambda i: (i, 0)),
                  _const_spec((1, d)), _const_spec((d, D_FF)), _const_spec((d, D_FF)),
                  _const_spec((D_FF, d)), _const_spec((1, d))],
        out_specs=pl.BlockSpec((tm, d), lambda i: (i, 0)),
        out_shape=jax.ShapeDtypeStruct((t, d), F32),
        compiler_params=_params(("parallel",)),
        name="ffn_final" if final else "ffn",
    )(x, g, wg, wu, wd, fg)


INPROJ_TM = 512
INPROJ_COLS = 256


def _inproj_kernel(x_ref, g_ref, wrw_ref, wml_ref, wif_ref, mu_ref, convw_ref, convb_ref,
                   zrw_ref, zml_ref, zif_ref, carry_rw_ref, carry_ml_ref, buf_a_ref, buf_b_ref):
    @pl.when(pl.program_id(1) == 0)
    def _():
        carry_rw_ref[...] = jnp.zeros(carry_rw_ref.shape, F32)
        carry_ml_ref[...] = jnp.zeros(carry_ml_ref.shape, F32)

    rows = x_ref.shape[0]
    h = _rmsnorm(x_ref[...], g_ref[...]).astype(BF16)

    def conv_stage(cols, scale):
        def finish(u):
            carry = carry_ml_ref[:, cols]
            w0, w1, w2, w3 = (convw_ref[j:j + 1, cols] for j in range(ML_CONV))
            u1 = _shift_rows_via(buf_a_ref, u, carry, 1)
            pair = w2 * u + w3 * u1
            pair_carry = w2 * carry + w3 * pltpu.roll(carry, 1, axis=0)
            conv = convb_ref[:, cols] + w0 * u + w1 * u1 + _shift_rows_via(buf_b_ref, pair, pair_carry, 2)
            carry_ml_ref[:, cols] = u[rows - SUBLANES:]
            zml_ref[:, cols] = _silu(conv) * scale if scale != 1.0 else _silu(conv)
        return wml_ref, cols, finish

    def plain_stage(cols, fn):
        def finish(u):
            zml_ref[:, cols] = fn(u)
        return wml_ref, cols, finish

    def lerp_stage(cols):
        def finish(z):
            prev = _shift_rows_via(buf_a_ref, z, carry_rw_ref[:, cols], 1)
            carry_rw_ref[:, cols] = z[rows - SUBLANES:]
            zrw_ref[:, cols] = z + (prev - z) * mu_ref[:, cols]
        return wrw_ref, cols, finish

    blk = lambda c: slice(c * INPROJ_COLS, (c + 1) * INPROJ_COLS)
    n_head_blocks = ML_WIDTH // INPROJ_COLS
    heavy, light = [], []
    for c in range(ML_QKVO // INPROJ_COLS):
        group = c // n_head_blocks
        if group < 2:
            heavy.append(conv_stage(blk(c), ML_HEAD_DIM ** -0.5 if group == 0 else 1.0))
        else:
            light.append(plain_stage(blk(c), _sigmoid if group == 3 else (lambda u: u)))
    light += [lerp_stage(blk(c)) for c in range(RW_COLS // INPROJ_COLS)]

    per_heavy = -(-len(light) // len(heavy))
    stages = []
    for i, stage in enumerate(heavy):
        stages.append(stage)
        stages += light[i * per_heavy:(i + 1) * per_heavy]
    pending = None
    for w_ref, cols, finish in stages:
        acc = jnp.dot(h, w_ref[:, cols], preferred_element_type=F32)
        if pending is not None:
            pending[0](pending[1])
        pending = (finish, acc)
    zif_ref[...] = jnp.dot(h, wif_ref[...], preferred_element_type=F32)
    pending[0](pending[1])


def _inproj_call(x, g, wrw, wml, wif, mu, convw, convb, bsz, seq):
    t, d = x.shape
    tm = min(INPROJ_TM, seq)
    nt = seq // tm
    row = lambda b, i: (b * nt + i, 0)
    return pl.pallas_call(
        _inproj_kernel,
        grid=(bsz, nt),
        in_specs=[pl.BlockSpec((tm, d), row),
                  _const_spec((1, d)), _const_spec((d, RW_COLS)), _const_spec((d, ML_QKVO)),
                  _const_spec((d, IF_PAD)), _const_spec((1, RW_COLS)),
                  _const_spec(convw.shape), _const_spec(convb.shape)],
        out_specs=[pl.BlockSpec((tm, RW_COLS), row), pl.BlockSpec((tm, ML_QKVO), row),
                   pl.BlockSpec((tm, IF_PAD), row)],
        out_shape=[jax.ShapeDtypeStruct((t, RW_COLS), F32), jax.ShapeDtypeStruct((t, ML_QKVO), F32),
                   jax.ShapeDtypeStruct((t, IF_PAD), F32)],
        scratch_shapes=[pltpu.VMEM((SUBLANES, RW_COLS), F32), pltpu.VMEM((SUBLANES, 2 * ML_WIDTH), F32),
                        pltpu.VMEM((SUBLANES + tm, INPROJ_COLS), F32),
                        pltpu.VMEM((SUBLANES + tm, INPROJ_COLS), F32)],
        compiler_params=_params(("parallel", "arbitrary")),
        name="inproj",
    )(x, g, wrw, wml, wif, mu, convw, convb)


RW1_CHUNKS = 8
RW1_GROUP = 16
INV_LEVELS = 6


def _rw_masks():
    t = np.arange(CHUNK)[:, None]
    s = np.tile(np.arange(CHUNK), 2)[None, :]
    levels = []
    for k in range(INV_LEVELS):
        size = 1 << k
        levels.append(((t // (2 * size)) == (s // (2 * size))) & ((t // size) % 2 == 1) & ((s // size) % 2 == 0))
    m = np.stack([s < t, s <= t, s == t] + levels).astype(np.float32)
    tri = (np.arange(CHUNK)[:, None] >= np.arange(CHUNK)[None, :]).astype(np.float32)
    return m, tri


def _group_ones(width, group):
    i = np.arange(width)
    return (i[:, None] // group == i[None, :] // group).astype(np.float32)


def _rw1_kernel(z_ref, w2_ref, a2_ref, g2_ref, vec_ref, gsum_ref, tri_ref, mask_ref,
                g_ref, yi_ref, bonus_ref, gate_ref, m_ref, c_ref):
    z = z_ref[...]
    rows = z.shape[0]
    r = z[:, 0:RW_WIDTH]
    k = z[:, RW_WIDTH:2 * RW_WIDTH]
    v = z[:, 2 * RW_WIDTH:3 * RW_WIDTH]
    lora_in = z[:, 3 * RW_WIDTH:3 * RW_WIDTH + DECAY_LORA + AAA_LORA]
    g_d = z[:, 3 * RW_WIDTH + DECAY_LORA + AAA_LORA:]
    w0 = vec_ref[0:1, :]
    a0 = vec_ref[1:2, :]
    k_k = vec_ref[2:3, :]
    k_a = vec_ref[3:4, :]
    r_k = vec_ref[4:5, :]

    w_raw = w0 + _bdot(jnp.tanh(lora_in), w2_ref[...])
    logw = -math.exp(-0.5) * _sigmoid(w_raw)
    a = _sigmoid(a0 + _bdot(lora_in, a2_ref[...]))
    gate_ref[...] = _bdot(_sigmoid(g_d), g2_ref[...])

    gsum = gsum_ref[...]
    kk = k * k_k
    kk = kk / jnp.maximum(jnp.sqrt(_per_slab(_bdot, kk * kk, gsum)), 1e-12)
    k_mod = k * (1.0 + (a - 1.0) * k_a)
    b = kk * a
    bonus_ref[...] = _per_slab(_dot_split2_rhs, r * k_mod * r_k, gsum) * v

    lane = lax.broadcasted_iota(jnp.int32, (CHUNK, PAIR), 1)
    head0 = lane < RW_HEAD_DIM
    strict = mask_ref[0]
    incl = mask_ref[1]
    eye = mask_ref[2]
    tri = tri_ref[...]

    def stack(x):
        return jnp.concatenate([jnp.where(head0, x, 0.0), jnp.where(head0, 0.0, x)], axis=0).astype(BF16)

    ops = {}
    for c in range(rows // CHUNK):
        rs = slice(c * CHUNK, (c + 1) * CHUNK)
        lw = logw[rs]
        cum = _dot_exact_lhs(tri, lw)
        cum_last = cum[CHUNK - 1:CHUNK]
        w_incl = jnp.exp(cum)
        w_inv = 1.0 / w_incl
        w_prev = jnp.exp(cum - lw)
        w_last = jnp.exp(cum_last)
        bt = b[rs] * w_inv
        kt = k_mod[rs] * w_inv
        full = dict(at=-kk[rs] * w_prev, rt=r[rs] * w_incl, bt=bt, kt=kt, be=bt * w_last, ke=kt * w_last, v2=v[rs])
        for p in range(RW_PAIRS):
            ls = slice(p * PAIR, (p + 1) * PAIR)
            item = {name: val[:, ls] for name, val in full.items()}
            item["w_last"] = w_last[:, ls]
            ops[(c, p)] = item

    keys = list(ops)
    for g0 in range(0, len(keys), RW1_GROUP):
        group = keys[g0:g0 + RW1_GROUP]
        n_ab, a_ak, a_rb, a_rk, t_inv, sv = {}, {}, {}, {}, {}, {}
        for key in group:
            o = ops[key]
            prod = _bdot_nt(jnp.concatenate([o["at"], o["rt"]], axis=0),
                            jnp.concatenate([stack(o["bt"]), stack(o["kt"])], axis=0))
            n_ab[key] = prod[:CHUNK, :PAIR] * strict
            a_ak[key] = prod[:CHUNK, PAIR:] * strict
            a_rb[key] = prod[CHUNK:, :PAIR] * incl
            a_rk[key] = prod[CHUNK:, PAIR:] * incl
            sv[key] = stack(o["v2"])
            t_inv[key] = eye + n_ab[key] * mask_ref[3]
        for lvl in range(1, INV_LEVELS):
            half = {key: _bdot(t_inv[key], stack(n_ab[key] * mask_ref[3 + lvl])) for key in group}
            t_inv = {key: t_inv[key] + _bdot(half[key], stack(t_inv[key])) for key in group}
        akv = {key: _bdot(a_ak[key], sv[key]) for key in group}
        pq = {key: _bdot(t_inv[key], jnp.concatenate([stack(ops[key]["at"]), stack(akv[key])], axis=1))
              for key in group}
        gy = {key: _bdot(a_rb[key], jnp.concatenate([stack(pq[key][:, :PAIR]), stack(pq[key][:, PAIR:])], axis=1))
              for key in group}
        ark_v = {key: _bdot(a_rk[key], sv[key]) for key in group}
        mc = {key: _bdot_tn(ops[key]["be"], pq[key]) for key in group}
        ke_v = {key: _bdot_tn(ops[key]["ke"], ops[key]["v2"]) for key in group}
        for key in group:
            c, p = key
            rs = slice(c * CHUNK, (c + 1) * CHUNK)
            ls = slice(p * PAIR, (p + 1) * PAIR)
            g_ref[rs, ls] = ops[key]["rt"] + gy[key][:, :PAIR]
            yi_ref[rs, ls] = gy[key][:, PAIR:] + ark_v[key]
            cv = mc[key][:, PAIR:] + ke_v[key]
            m_ref[c, p] = eye * ops[key]["w_last"] + jnp.where(head0, mc[key][:CHUNK, :PAIR], mc[key][CHUNK:, :PAIR])
            c_ref[c, p] = jnp.where(head0, cv[:CHUNK], cv[CHUNK:])


def _rw1_call(zrw, w2p, a2p, g2, vec, gsum, tri, masks):
    t = zrw.shape[0]
    rows = RW1_CHUNKS * CHUNK
    nchunks = t // CHUNK
    tok = lambda i: (i, 0)
    blk = lambda i: (i, 0, 0, 0)
    tok_spec = pl.BlockSpec((rows, RW_WIDTH), tok)
    mc_spec = pl.BlockSpec((RW1_CHUNKS, RW_PAIRS, CHUNK, PAIR), blk)
    tok_shape = jax.ShapeDtypeStruct((t, RW_WIDTH), F32)
    mc_shape = jax.ShapeDtypeStruct((nchunks, RW_PAIRS, CHUNK, PAIR), F32)
    return pl.pallas_call(
        _rw1_kernel,
        grid=(t // rows,),
        in_specs=[pl.BlockSpec((rows, RW_COLS), tok),
                  _const_spec(w2p.shape), _const_spec(a2p.shape), _const_spec(g2.shape),
                  _const_spec(vec.shape), _const_spec(gsum.shape), _const_spec(tri.shape),
                  _const_spec(masks.shape)],
        out_specs=[tok_spec, tok_spec, tok_spec, tok_spec, mc_spec, mc_spec],
        out_shape=[tok_shape, tok_shape, tok_shape, tok_shape, mc_shape, mc_shape],
        compiler_params=_params(("parallel",)),
        name="rwkv_chunk_local",
    )(zrw, w2p, a2p, g2, vec, gsum, tri, masks)


RW2_CHUNKS = 8
RW2_BATCH = 2


def _rw2_kernel(g_ref, yi_ref, bonus_ref, gate_ref, m_ref, c_ref, gmean_ref, gn_ref, y_ref, h_ref):
    @pl.when(pl.program_id(1) == 0)
    def _():
        h_ref[...] = jnp.zeros(h_ref.shape, F32)

    lane = lax.broadcasted_iota(jnp.int32, (CHUNK, PAIR), 1)
    head0 = lane < RW_HEAD_DIM

    def block_diag(x):
        return jnp.concatenate([jnp.where(head0, x, 0.0), jnp.where(head0, 0.0, x)], axis=0)

    nb = g_ref.shape[0]
    chains = [(b, p) for b in range(nb) for p in range(RW_PAIRS)]
    lsl = [slice(p * PAIR, (p + 1) * PAIR) for p in range(RW_PAIRS)]
    state = {(b, p): h_ref[b, p] for b, p in chains}
    ys = {}
    for c in range(RW2_CHUNKS):
        rs = slice(c * CHUNK, (c + 1) * CHUNK)
        for b, p in chains:
            ys[b, c, p] = _bdot(g_ref[b, rs, lsl[p]], state[b, p]) + yi_ref[b, rs, lsl[p]]
        state = {(b, p): _dot_f32(block_diag(m_ref[b, c, p]), state[b, p]) + block_diag(c_ref[b, c, p])
                 for b, p in chains}
    for b, p in chains:
        h_ref[b, p] = state[b, p]
    gmean = gmean_ref[...]
    for b in range(nb):
        y = jnp.concatenate([jnp.concatenate([ys[b, c, p] for p in range(RW_PAIRS)], axis=1)
                             for c in range(RW2_CHUNKS)], axis=0)
        mu = _per_slab(_dot_split2_rhs, y, gmean)
        d = y - mu
        var = _per_slab(_bdot, d * d, gmean)
        y = d * lax.rsqrt(var + GN_EPS) * gn_ref[0:1, :] + gn_ref[1:2, :]
        y_ref[b] = ((y + bonus_ref[b]) * gate_ref[b]).astype(BF16)


def _rw2_call(g, yi, bonus, gate, m, c, gmean, gn, bsz, seq):
    t = g.shape[0]
    rows = RW2_CHUNKS * CHUNK
    nt = seq // rows
    nb = RW2_BATCH if bsz % RW2_BATCH == 0 else 1
    per_batch = lambda a: a.reshape((bsz, a.shape[0] // bsz) + a.shape[1:])
    tok_spec = pl.BlockSpec((nb, rows, RW_WIDTH), lambda b, i: (b, i, 0))
    mc_spec = pl.BlockSpec((nb, RW2_CHUNKS, RW_PAIRS, CHUNK, PAIR), lambda b, i: (b, i, 0, 0, 0))
    y = pl.pallas_call(
        _rw2_kernel,
        grid=(bsz // nb, nt),
        in_specs=[tok_spec, tok_spec, tok_spec, tok_spec, mc_spec, mc_spec,
                  _const_spec(gmean.shape), _const_spec(gn.shape)],
        out_specs=tok_spec,
        out_shape=jax.ShapeDtypeStruct((bsz, seq, RW_WIDTH), BF16),
        scratch_shapes=[pltpu.VMEM((nb, RW_PAIRS, PAIR, PAIR), F32)],
        compiler_params=_params(("parallel", "arbitrary")),
        name="rwkv_scan",
    )(per_batch(g), per_batch(yi), per_batch(bonus), per_batch(gate), per_batch(m), per_batch(c), gmean, gn)
    return y.reshape(t, RW_WIDTH)


ML_CHUNK = 128
ML_STEP_CHUNKS = 4
ML_BATCH = 1


def _log_sigmoid(x):
    return jnp.minimum(x, 0.0) - jnp.log(1.0 + jnp.exp(-jnp.abs(x)))


def _ml_kernel(z_ref, zif_ref, ifb_ref, norm_ref, tri_ref, y_ref, state_ref, m_ref):
    @pl.when(pl.program_id(1) == 0)
    def _():
        state_ref[...] = jnp.zeros(state_ref.shape, F32)
        m_ref[...] = jnp.zeros(m_ref.shape, F32)

    nb, rows = z_ref.shape[0], z_ref.shape[1]
    q_all = [z_ref[b, :, 0:ML_WIDTH] for b in range(nb)]
    k_all = [z_ref[b, :, ML_WIDTH:2 * ML_WIDTH] for b in range(nb)]
    v_all = [z_ref[b, :, 2 * ML_WIDTH:3 * ML_WIDTH] for b in range(nb)]
    ogate_all = [z_ref[b, :, 3 * ML_WIDTH:4 * ML_WIDTH] for b in range(nb)]

    pre = [zif_ref[b] + ifb_ref[...] for b in range(nb)]
    logf = [_log_sigmoid(pre[b]) for b in range(nb)]
    tri = tri_ref[...]
    lane = lax.broadcasted_iota(jnp.int32, (ML_CHUNK, LANES), 1)
    tq = lax.broadcasted_iota(jnp.int32, (ML_CHUNK, ML_CHUNK), 0)
    tk = lax.broadcasted_iota(jnp.int32, (ML_CHUNK, ML_CHUNK), 1)
    causal = tk <= tq
    ones = jnp.ones((ML_CHUNK, ML_HEAD_DIM), F32)
    dense = (ML_CHUNK, ML_CHUNK)

    chains = [(b, h) for b in range(nb) for h in range(ML_HEADS)]
    lsl = [slice(h * ML_HEAD_DIM, (h + 1) * ML_HEAD_DIM) for h in range(ML_HEADS)]
    m_run = {ch: m_ref[ch] for ch in chains}
    state_run = {ch: state_ref[ch] for ch in chains}
    ys = {b: [] for b in range(nb)}
    for c in range(rows // ML_CHUNK):
        rs = slice(c * ML_CHUNK, (c + 1) * ML_CHUNK)
        z, z_t = [], []
        for b in range(nb):
            bcum = _dot_exact_lhs(tri, logf[b][rs])
            z.append(jnp.where(lane < ML_HEADS, pre[b][rs], bcum))
            z_t.append(z[b].T)
        b_col = {(b, h): jnp.broadcast_to(z[b][:, ML_HEADS + h:ML_HEADS + h + 1], dense)
                 for b, h in chains}
        ib_col = {(b, h): jnp.broadcast_to(z[b][:, h:h + 1] - z[b][:, ML_HEADS + h:ML_HEADS + h + 1], dense)
                  for b, h in chains}
        ib_row = {(b, h): z_t[b][h:h + 1, :] - z_t[b][ML_HEADS + h:ML_HEADS + h + 1, :]
                  for b, h in chains}
        b_end = {ch: b_col[ch][ML_CHUNK - 1:ML_CHUNK] for ch in chains}
        m_prev = m_run
        state = state_run
        q = {(b, h): q_all[b][rs, lsl[h]] for b, h in chains}
        k = {(b, h): k_all[b][rs, lsl[h]] for b, h in chains}
        v = {(b, h): v_all[b][rs, lsl[h]] for b, h in chains}
        s_qk = {ch: _bdot_nt(q[ch], k[ch]) for ch in chains}
        inter = {ch: _bdot(q[ch], state[ch]) for ch in chains}

        log_d = {ch: jnp.where(causal, b_col[ch] + ib_row[ch], NEG_BIG) for ch in chains}
        log_inter = {ch: b_col[ch] + m_prev[ch] for ch in chains}
        m_row = {ch: jnp.maximum(log_inter[ch], jnp.max(log_d[ch], axis=-1, keepdims=True)) for ch in chains}
        p = {ch: jnp.exp(log_d[ch] - m_row[ch]) * s_qk[ch] for ch in chains}
        intra = {ch: _bdot(p[ch], jnp.concatenate([v[ch], ones], axis=1)) for ch in chains}

        log_w_end = {ch: ib_col[ch] + b_end[ch] for ch in chains}
        m_chunk = {ch: jnp.max(log_w_end[ch], axis=0, keepdims=True) for ch in chains}
        w_end = {ch: jnp.exp(log_w_end[ch] - m_chunk[ch]) for ch in chains}
        kv = {ch: _bdot_tn(k[ch], jnp.concatenate([w_end[ch] * v[ch], w_end[ch]], axis=1)) for ch in chains}
        m_run, state_run = {}, {}
        for ch in chains:
            m_new = jnp.maximum(b_end[ch] + m_prev[ch], m_chunk[ch])
            dec = jnp.exp(b_end[ch] + m_prev[ch] - m_new)
            sc = jnp.exp(m_chunk[ch] - m_new)
            state_run[ch] = (jnp.concatenate([dec, dec], axis=1) * state[ch]
                             + jnp.concatenate([sc, sc], axis=1) * kv[ch])
            m_run[ch] = m_new

        outs = {}
        for ch in chains:
            b, h = ch
            w_inter = jnp.exp(log_inter[ch] - m_row[ch])
            num = intra[ch][:, :ML_HEAD_DIM] + w_inter * inter[ch][:, :ML_HEAD_DIM]
            den = intra[ch][:, ML_HEAD_DIM:] + w_inter * inter[ch][:, ML_HEAD_DIM:]
            hh = num / jnp.maximum(jnp.abs(den), jnp.exp(-m_row[ch]))
            mu = jnp.mean(hh, axis=-1, keepdims=True)
            d = hh - mu
            var = jnp.mean(d * d, axis=-1, keepdims=True)
            hn = d * lax.rsqrt(var + ML_NORM_EPS)
            outs[ch] = hn * norm_ref[:, lsl[h]] * ogate_all[b][rs, lsl[h]]
        for b in range(nb):
            ys[b].append(jnp.concatenate([outs[b, h] for h in range(ML_HEADS)], axis=1).astype(BF16))
    for b in range(nb):
        y_ref[b] = jnp.concatenate(ys[b], axis=0)
    for ch in chains:
        state_ref[ch] = state_run[ch]
        m_ref[ch] = m_run[ch]


def _ml_call(zml, zif, ifb, norm, tri, bsz, seq):
    t = zml.shape[0]
    rows = ML_STEP_CHUNKS * ML_CHUNK
    nt = seq // rows
    nb = ML_BATCH if bsz % ML_BATCH == 0 else 1
    tok = lambda b, i: (b, i, 0)
    y = pl.pallas_call(
        _ml_kernel,
        grid=(bsz // nb, nt),
        in_specs=[pl.BlockSpec((nb, rows, ML_QKVO), tok), pl.BlockSpec((nb, rows, IF_PAD), tok),
                  _const_spec(ifb.shape), _const_spec(norm.shape), _const_spec(tri.shape)],
        out_specs=pl.BlockSpec((nb, rows, ML_WIDTH), tok),
        out_shape=jax.ShapeDtypeStruct((bsz, seq, ML_WIDTH), BF16),
        scratch_shapes=[pltpu.VMEM((nb, ML_HEADS, ML_HEAD_DIM, 2 * ML_HEAD_DIM), F32),
                        pltpu.VMEM((nb, ML_HEADS, 1, LANES), F32)],
        compiler_params=_params(("parallel", "arbitrary")),
        name="mlstm",
    )(zml.reshape(bsz, seq, ML_QKVO), zif.reshape(bsz, seq, IF_PAD), ifb, norm, tri)
    return y.reshape(t, ML_WIDTH)


MERGE_TM = 512


def _merge_kernel(x_ref, g_ref, wgt_ref, yrw_ref, yml_ref, wur_ref, wum_ref, wo_ref, o_ref):
    x = x_ref[...]
    d = x.shape[1]
    h = _rmsnorm(x, g_ref[...]).astype(BF16)
    gates = _sigmoid(jnp.dot(h, wgt_ref[...], preferred_element_type=F32))
    up_rw = jnp.dot(yrw_ref[...], wur_ref[...], preferred_element_type=F32)
    up_ml = jnp.dot(yml_ref[...], wum_ref[...], preferred_element_type=F32)
    merged = gates[:, 0:d] * up_rw + gates[:, d:] * up_ml
    o_ref[...] = x + _bdot(merged, wo_ref[...])


def _merge_call(x, g, wgt, yrw, yml, wur, wum, wo):
    t, d = x.shape
    tm = min(MERGE_TM, t)
    row = lambda i: (i, 0)
    return pl.pallas_call(
        _merge_kernel,
        grid=(t // tm,),
        in_specs=[pl.BlockSpec((tm, d), row), _const_spec(g.shape), _const_spec(wgt.shape),
                  pl.BlockSpec((tm, RW_WIDTH), row), pl.BlockSpec((tm, ML_WIDTH), row),
                  _const_spec(wur.shape), _const_spec(wum.shape), _const_spec(wo.shape)],
        out_specs=pl.BlockSpec((tm, d), row),
        out_shape=jax.ShapeDtypeStruct((t, d), F32),
        compiler_params=_params(("parallel",)),
        name="merge",
    )(x, g, wgt, yrw, yml, wur, wum, wo)


def _memkv_kernel(mem_ref, g_ref, wkv_ref, k_ref, v_ref):
    d = mem_ref.shape[1]
    hm = _rmsnorm(mem_ref[...], g_ref[...])
    kv = _bdot(hm, wkv_ref[...])
    k_ref[...] = kv[:, 0:d].astype(BF16)
    v_ref[...] = kv[:, d:].astype(BF16)


def _memkv_call(mem2d, g, wkv):
    n, d = mem2d.shape
    row = lambda i: (i, 0)
    return pl.pallas_call(
        _memkv_kernel,
        grid=(n // MEM_LEN,),
        in_specs=[pl.BlockSpec((MEM_LEN, d), row), _const_spec(g.shape), _const_spec(wkv.shape)],
        out_specs=[pl.BlockSpec((MEM_LEN, d), row), pl.BlockSpec((MEM_LEN, d), row)],
        out_shape=[jax.ShapeDtypeStruct((n, d), BF16), jax.ShapeDtypeStruct((n, d), BF16)],
        compiler_params=_params(("parallel",)),
        name="mem_kv",
    )(mem2d, g, wkv)


XA_TM = 512


def _xattn_kernel(x_ref, g_ref, wq_ref, k_ref, v_ref, wo_ref, o_ref):
    x = x_ref[...]
    hq = _rmsnorm(x, g_ref[...])
    q = _bdot(hq, wq_ref[...])
    lsl = [slice(h * XA_HEAD_DIM, (h + 1) * XA_HEAD_DIM) for h in range(XA_HEADS)]

    def softmax(s):
        e = jnp.exp(s - jnp.max(s, axis=-1, keepdims=True))
        return e / jnp.sum(e, axis=-1, keepdims=True)

    outs = []
    s_prev = _bdot_nt(q[:, lsl[0]], k_ref[:, lsl[0]])
    for h in range(XA_HEADS):
        s_next = _bdot_nt(q[:, lsl[h + 1]], k_ref[:, lsl[h + 1]]) if h + 1 < XA_HEADS else None
        outs.append(_bdot(softmax(s_prev), v_ref[:, lsl[h]]))
        s_prev = s_next
    o = jnp.concatenate(outs, axis=1)
    o_ref[...] = x + _bdot(o, wo_ref[...])


def _xattn_call(x, g, wq, kmem, vmem, wo, bsz, seq):
    t, d = x.shape
    tm = min(XA_TM, seq)
    nt = seq // tm
    row = lambda b, i: (b * nt + i, 0)
    mem = lambda b, i: (b, 0)
    return pl.pallas_call(
        _xattn_kernel,
        grid=(bsz, nt),
        in_specs=[pl.BlockSpec((tm, d), row), _const_spec(g.shape), _const_spec(wq.shape),
                  pl.BlockSpec((MEM_LEN, d), mem), pl.BlockSpec((MEM_LEN, d), mem),
                  _const_spec(wo.shape)],
        out_specs=pl.BlockSpec((tm, d), row),
        out_shape=jax.ShapeDtypeStruct((t, d), F32),
        compiler_params=_params(("parallel", "parallel")),
        name="mem_xattn",
    )(x, g, wq, kmem, vmem, wo)


def _row(v):
    return v.reshape(1, -1).astype(F32)


def _layer(x, mem2d, bsz, seq, p):
    d = D_MODEL
    bf = lambda w: w.astype(BF16)
    masks_np, tri_np = _rw_masks()
    masks = jnp.asarray(masks_np)
    tri = jnp.asarray(tri_np, dtype=BF16)
    gsum = jnp.asarray(_group_ones(PAIR, RW_HEAD_DIM), dtype=BF16)
    gmean = jnp.asarray(_group_ones(PAIR, RW_HEAD_DIM) / RW_HEAD_DIM, dtype=BF16)

    x = _ffn_call(x, _row(p["ffn1_norm"]), bf(p["ffn1_w_gate"]), bf(p["ffn1_w_up"]), bf(p["ffn1_w_down"]),
                  _row(p["ffn1_norm"]), final=False)

    w_in = p["w_in"]
    c0 = RW_COLS
    c1 = c0 + ML_QKVO
    c2 = c1 + 2 * ML_HEADS
    w_if = jnp.pad(w_in[:, c1:c2], ((0, 0), (0, IF_PAD - 2 * ML_HEADS)))
    convw = jnp.concatenate([p["ml_conv_w"].astype(F32), jnp.zeros((SUBLANES - ML_CONV, 2 * ML_WIDTH), F32)], axis=0)
    zrw, zml, zif = _inproj_call(x, _row(p["mix_norm"]), bf(w_in[:, :c0]), bf(w_in[:, c0:c1]), bf(w_if),
                                 _row(p["rw_mu"]), convw, _row(p["ml_conv_b"]), bsz, seq)

    zpad = jnp.zeros((DECAY_LORA, RW_WIDTH), F32)
    w2p = bf(jnp.concatenate([p["rw_w2"], zpad], axis=0))
    a2p = bf(jnp.concatenate([zpad, p["rw_a2"]], axis=0))
    vec = jnp.stack([p["rw_w0"], p["rw_a0"], p["rw_k_k"], p["rw_k_a"], p["rw_r_k"].reshape(-1),
                     jnp.zeros_like(p["rw_w0"]), jnp.zeros_like(p["rw_w0"]), jnp.zeros_like(p["rw_w0"])]).astype(F32)
    g, yi, bonus, gate, m, c = _rw1_call(zrw, w2p, a2p, bf(p["rw_g2"]), vec, gsum, tri, masks)
    gn = jnp.concatenate([jnp.stack([p["rw_gn_w"], p["rw_gn_b"]]).astype(F32),
                          jnp.zeros((SUBLANES - 2, RW_WIDTH), F32)], axis=0)
    y_rw = _rw2_call(g, yi, bonus, gate, m, c, gmean, gn, bsz, seq)

    ifb = jnp.pad(jnp.concatenate([p["ml_i_b"], p["ml_f_b"]]).astype(F32), (0, IF_PAD - 2 * ML_HEADS)).reshape(1, -1)
    tri_ml = jnp.asarray(np.tril(np.ones((ML_CHUNK, ML_CHUNK), np.float32)), dtype=BF16)
    y_ml = _ml_call(zml, zif, ifb, _row(p["ml_norm"]), tri_ml, bsz, seq)

    x = _merge_call(x, _row(p["mix_norm"]), bf(w_in[:, c2:]), y_rw, y_ml,
                    bf(p["w_up_rw"]), bf(p["w_up_ml"]), bf(p["w_out"]))

    kmem, vmem = _memkv_call(mem2d, _row(p["mem_norm"]), bf(p["xa_wkv"]))
    xa_scale = XA_HEAD_DIM ** -0.5
    assert math.frexp(xa_scale)[0] == 0.5, "folding the scale into bf16 weights is exact only for powers of two"
    x = _xattn_call(x, _row(p["xa_norm"]), bf(p["xa_wq"] * xa_scale), kmem, vmem, bf(p["xa_wo"]), bsz, seq)
    return x


def kernel(x, mem, ffn1_norm, ffn1_w_gate, ffn1_w_up, ffn1_w_down, mix_norm, w_in, rw_mu, rw_w0, rw_w2, rw_a0,
           rw_a2, rw_g2, rw_k_k, rw_k_a, rw_r_k, rw_gn_w, rw_gn_b, ml_conv_w, ml_conv_b, ml_i_b, ml_f_b, ml_norm,
           w_up_rw, w_up_ml, w_out, xa_norm, mem_norm, xa_wq, xa_wkv, xa_wo, ffn2_norm, ffn2_w_gate, ffn2_w_up,
           ffn2_w_down, final_norm):
    bsz, seq, d = x.shape
    depth = ffn1_norm.shape[0]
    stacked = dict(ffn1_norm=ffn1_norm, ffn1_w_gate=ffn1_w_gate, ffn1_w_up=ffn1_w_up, ffn1_w_down=ffn1_w_down,
                   mix_norm=mix_norm, w_in=w_in, rw_mu=rw_mu, rw_w0=rw_w0, rw_w2=rw_w2, rw_a0=rw_a0, rw_a2=rw_a2,
                   rw_g2=rw_g2, rw_k_k=rw_k_k, rw_k_a=rw_k_a, rw_r_k=rw_r_k, rw_gn_w=rw_gn_w, rw_gn_b=rw_gn_b,
                   ml_conv_w=ml_conv_w, ml_conv_b=ml_conv_b, ml_i_b=ml_i_b, ml_f_b=ml_f_b, ml_norm=ml_norm,
                   w_up_rw=w_up_rw, w_up_ml=w_up_ml, w_out=w_out, xa_norm=xa_norm, mem_norm=mem_norm,
                   xa_wq=xa_wq, xa_wkv=xa_wkv, xa_wo=xa_wo, ffn2_norm=ffn2_norm, ffn2_w_gate=ffn2_w_gate,
                   ffn2_w_up=ffn2_w_up, ffn2_w_down=ffn2_w_down)
    h = x.reshape(bsz * seq, d)
    mem2d = mem.reshape(bsz * mem.shape[1], d)
    bf = lambda w: w.astype(BF16)
    for l in range(depth):
        p = {name: w[l] for name, w in stacked.items()}
        h = _layer(h, mem2d, bsz, seq, p)
        last = l == depth - 1
        h = _ffn_call(h, _row(p["ffn2_norm"]), bf(p["ffn2_w_gate"]), bf(p["ffn2_w_up"]), bf(p["ffn2_w_down"]),
                      _row(final_norm), final=last)
    return h.reshape(bsz, seq, d)
```

```python
import functools
import math

import jax
import jax.numpy as jnp
import numpy as np
from jax import lax
from jax.experimental import pallas as pl
from jax.experimental.pallas import tpu as pltpu

F32 = jnp.float32
BF16 = jnp.bfloat16

D_MODEL = 1024
D_FF = 2816
FFN_HALF = 0.5
NORM_EPS = 1e-5
MEM_LEN = 256

RW_HEADS = 8
RW_HEAD_DIM = 64
RW_WIDTH = 512
DECAY_LORA = 64
AAA_LORA = 64
GATE_LORA = 128
RW_COLS = 3 * RW_WIDTH + DECAY_LORA + AAA_LORA + GATE_LORA
GN_EPS = 64e-5
RW_PAIRS = RW_HEADS // 2
PAIR = 2 * RW_HEAD_DIM

ML_HEADS = 4
ML_HEAD_DIM = 128
ML_WIDTH = 512
ML_CONV = 4
ML_NORM_EPS = 1e-6
ML_QKVO = 4 * ML_WIDTH
IF_PAD = 128

XA_HEADS = 4
XA_HEAD_DIM = 256

CHUNK = 64
LANES = 128
SUBLANES = 8
NEG_BIG = -1e30

VMEM_LIMIT = 56 * 1024 * 1024


def _bdot(a, b):
    return jnp.dot(a.astype(BF16), b.astype(BF16), preferred_element_type=F32)


def _bdot_nt(a, b):
    return lax.dot_general(a.astype(BF16), b.astype(BF16), (((1,), (1,)), ((), ())),
                           preferred_element_type=F32)


def _bdot_tn(a, b):
    return lax.dot_general(a.astype(BF16), b.astype(BF16), (((0,), (0,)), ((), ())),
                           preferred_element_type=F32)


def _split3(x):
    hi = x.astype(BF16)
    r1 = x - hi.astype(F32)
    mid = r1.astype(BF16)
    lo = (r1 - mid.astype(F32)).astype(BF16)
    return hi, mid, lo


def _dot_exact_lhs(c_bf16, x):
    hi, mid, lo = _split3(x)
    f = lambda p: jnp.dot(c_bf16, p, preferred_element_type=F32)
    return f(hi) + f(mid) + f(lo)


def _dot_split2_rhs(x, c_bf16):
    hi = x.astype(BF16)
    lo = (x - hi.astype(F32)).astype(BF16)
    f = lambda p: jnp.dot(p, c_bf16, preferred_element_type=F32)
    return f(hi) + f(lo)


def _per_slab(fn, x, c_bf16):
    n = x.shape[1] // LANES
    return jnp.concatenate([fn(x[:, i * LANES:(i + 1) * LANES], c_bf16) for i in range(n)], axis=1)


def _dot_f32(a, b):
    a_hi = a.astype(BF16)
    a_lo = (a - a_hi.astype(F32)).astype(BF16)
    b_hi = b.astype(BF16)
    b_lo = (b - b_hi.astype(F32)).astype(BF16)
    f = lambda p, q: jnp.dot(p, q, preferred_element_type=F32)
    return f(a_hi, b_hi) + f(a_hi, b_lo) + f(a_lo, b_hi)


def _sigmoid(x):
    return 1.0 / (1.0 + jnp.exp(-x))


def _silu(x):
    return x * _sigmoid(x)


def _rmsnorm(x, g):
    ms = jnp.mean(x * x, axis=-1, keepdims=True)
    return x * lax.rsqrt(ms + NORM_EPS) * g


def _shift_rows(u, carry8, j):
    rolled = pltpu.roll(u, j, axis=0)
    rolled_c = pltpu.roll(carry8, j, axis=0)
    row = lax.broadcasted_iota(jnp.int32, (SUBLANES, u.shape[1]), 0)
    top = jnp.where(row < j, rolled_c, rolled[:SUBLANES])
    return jnp.concatenate([top, rolled[SUBLANES:]], axis=0)


def _shift_rows_via(buf_ref, u, carry8, j):
    rows = u.shape[0]
    buf_ref[0:SUBLANES, :] = carry8
    buf_ref[SUBLANES:, :] = u
    return buf_ref[SUBLANES - j:SUBLANES - j + rows, :]


def _const_spec(shape):
    nd = len(shape)
    return pl.BlockSpec(shape, lambda *_: (0,) * nd, pipeline_mode=pl.Buffered(1))


def _params(sem):
    return pltpu.CompilerParams(dimension_semantics=sem, vmem_limit_bytes=VMEM_LIMIT)


FF_CHUNK = 256
FFN_TM = 1024


def _ffn_kernel(x_ref, g_ref, wg_ref, wu_ref, wd_ref, fg_ref, o_ref, *, final):
    x = x_ref[...]
    h = _rmsnorm(x, g_ref[...]).astype(BF16)
    acc = jnp.zeros(x.shape, F32)
    for c in range(D_FF // FF_CHUNK):
        sl = slice(c * FF_CHUNK, (c + 1) * FF_CHUNK)
        gate = jnp.dot(h, wg_ref[:, sl], preferred_element_type=F32)
        up = jnp.dot(h, wu_ref[:, sl], preferred_element_type=F32)
        act = (_silu(gate) * up).astype(BF16)
        acc = acc + jnp.dot(act, wd_ref[sl, :], preferred_element_type=F32)
    y = x + FFN_HALF * acc
    if final:
        y = _rmsnorm(y, fg_ref[...])
    o_ref[...] = y


def _ffn_call(x, g, wg, wu, wd, fg, final):
    t, d = x.shape
    tm = min(FFN_TM, t)
    return pl.pallas_call(
        functools.partial(_ffn_kernel, final=final),
        grid=(t // tm,),
        in_specs=[pl.BlockSpec((tm, d), lambda i: (i, 0)),
                  _const_spec((1, d)), _const_spec((d, D_FF)), _const_spec((d, D_FF)),
                  _const_spec((D_FF, d)), _const_spec((1, d))],
        out_specs=pl.BlockSpec((tm, d), lambda i: (i, 0)),
        out_shape=jax.ShapeDtypeStruct((t, d), F32),
        compiler_params=_params(("parallel",)),
        name="ffn_final" if final else "ffn",
    )(x, g, wg, wu, wd, fg)


INPROJ_TM = 512
INPROJ_COLS = 256


def _inproj_kernel(x_ref, g_ref, wrw_ref, wml_ref, wif_ref, mu_ref, convw_ref, convb_ref,
                   zrw_ref, zml_ref, zif_ref, carry_rw_ref, carry_ml_ref, buf_a_ref, buf_b_ref):
    @pl.when(pl.program_id(1) == 0)
    def _():
        carry_rw_ref[...] = jnp.zeros(carry_rw_ref.shape, F32)
        carry_ml_ref[...] = jnp.zeros(carry_ml_ref.shape, F32)

    rows = x_ref.shape[0]
    h = _rmsnorm(x_ref[...], g_ref[...]).astype(BF16)

    def conv_stage(cols, scale):
        def finish(u):
            carry = carry_ml_ref[:, cols]
            w0, w1, w2, w3 = (convw_ref[j:j + 1, cols] for j in range(ML_CONV))
            u1 = _shift_rows_via(buf_a_ref, u, carry, 1)
            pair = w2 * u + w3 * u1
            pair_carry = w2 * carry + w3 * pltpu.roll(carry, 1, axis=0)
            conv = convb_ref[:, cols] + w0 * u + w1 * u1 + _shift_rows_via(buf_b_ref, pair, pair_carry, 2)
            carry_ml_ref[:, cols] = u[rows - SUBLANES:]
            zml_ref[:, cols] = _silu(conv) * scale if scale != 1.0 else _silu(conv)
        return wml_ref, cols, finish

    def plain_stage(cols, fn):
        def finish(u):
            zml_ref[:, cols] = fn(u)
        return wml_ref, cols, finish

    def lerp_stage(cols):
        def finish(z):
            prev = _shift_rows_via(buf_a_ref, z, carry_rw_ref[:, cols], 1)
            carry_rw_ref[:, cols] = z[rows - SUBLANES:]
            zrw_ref[:, cols] = z + (prev - z) * mu_ref[:, cols]
        return wrw_ref, cols, finish

    blk = lambda c: slice(c * INPROJ_COLS, (c + 1) * INPROJ_COLS)
    n_head_blocks = ML_WIDTH // INPROJ_COLS
    heavy, light = [], []
    for c in range(ML_QKVO // INPROJ_COLS):
        group = c // n_head_blocks
        if group < 2:
            heavy.append(conv_stage(blk(c), ML_HEAD_DIM ** -0.5 if group == 0 else 1.0))
        else:
            light.append(plain_stage(blk(c), _sigmoid if group == 3 else (lambda u: u)))
    light += [lerp_stage(blk(c)) for c in range(RW_COLS // INPROJ_COLS)]

    per_heavy = -(-len(light) // len(heavy))
    stages = []
    for i, stage in enumerate(heavy):
        stages.append(stage)
        stages += light[i * per_heavy:(i + 1) * per_heavy]
    pending = None
    for w_ref, cols, finish in stages:
        acc = jnp.dot(h, w_ref[:, cols], preferred_element_type=F32)
        if pending is not None:
            pending[0](pending[1])
        pending = (finish, acc)
    zif_ref[...] = jnp.dot(h, wif_ref[...], preferred_element_type=F32)
    pending[0](pending[1])


def _inproj_call(x, g, wrw, wml, wif, mu, convw, convb, bsz, seq):
    t, d = x.shape
    tm = min(INPROJ_TM, seq)
    nt = seq // tm
    row = lambda b, i: (b * nt + i, 0)
    return pl.pallas_call(
        _inproj_kernel,
        grid=(bsz, nt),
        in_specs=[pl.BlockSpec((tm, d), row),
                  _const_spec((1, d)), _const_spec((d, RW_COLS)), _const_spec((d, ML_QKVO)),
                  _const_spec((d, IF_PAD)), _const_spec((1, RW_COLS)),
                  _const_spec(convw.shape), _const_spec(convb.shape)],
        out_specs=[pl.BlockSpec((tm, RW_COLS), row), pl.BlockSpec((tm, ML_QKVO), row),
                   pl.BlockSpec((tm, IF_PAD), row)],
        out_shape=[jax.ShapeDtypeStruct((t, RW_COLS), F32), jax.ShapeDtypeStruct((t, ML_QKVO), F32),
                   jax.ShapeDtypeStruct((t, IF_PAD), F32)],
        scratch_shapes=[pltpu.VMEM((SUBLANES, RW_COLS), F32), pltpu.VMEM((SUBLANES, 2 * ML_WIDTH), F32),
                        pltpu.VMEM((SUBLANES + tm, INPROJ_COLS), F32),
                        pltpu.VMEM((SUBLANES + tm, INPROJ_COLS), F32)],
        compiler_params=_params(("parallel", "arbitrary")),
        name="inproj",
    )(x, g, wrw, wml, wif, mu, convw, convb)


RW1_CHUNKS = 8
RW1_GROUP = 16
INV_LEVELS = 6


def _rw_masks():
    t = np.arange(CHUNK)[:, None]
    s = np.tile(np.arange(CHUNK), 2)[None, :]
    levels = []
    for k in range(INV_LEVELS):
        size = 1 << k
        levels.append(((t // (2 * size)) == (s // (2 * size))) & ((t // size) % 2 == 1) & ((s // size) % 2 == 0))
    m = np.stack([s < t, s <= t, s == t] + levels).astype(np.float32)
    tri = (np.arange(CHUNK)[:, None] >= np.arange(CHUNK)[None, :]).astype(np.float32)
    return m, tri


def _group_ones(width, group):
    i = np.arange(width)
    return (i[:, None] // group == i[None, :] // group).astype(np.float32)


def _rw1_kernel(z_ref, w2_ref, a2_ref, g2_ref, vec_ref, gsum_ref, tri_ref, mask_ref,
                g_ref, yi_ref, bonus_ref, gate_ref, m_ref, c_ref):
    z = z_ref[...]
    rows = z.shape[0]
    r = z[:, 0:RW_WIDTH]
    k = z[:, RW_WIDTH:2 * RW_WIDTH]
    v = z[:, 2 * RW_WIDTH:3 * RW_WIDTH]
    lora_in = z[:, 3 * RW_WIDTH:3 * RW_WIDTH + DECAY_LORA + AAA_LORA]
    g_d = z[:, 3 * RW_WIDTH + DECAY_LORA + AAA_LORA:]
    w0 = vec_ref[0:1, :]
    a0 = vec_ref[1:2, :]
    k_k = vec_ref[2:3, :]
    k_a = vec_ref[3:4, :]
    r_k = vec_ref[4:5, :]

    w_raw = w0 + _bdot(jnp.tanh(lora_in), w2_ref[...])
    logw = -math.exp(-0.5) * _sigmoid(w_raw)
    a = _sigmoid(a0 + _bdot(lora_in, a2_ref[...]))
    gate_ref[...] = _bdot(_sigmoid(g_d), g2_ref[...])

    gsum = gsum_ref[...]
    kk = k * k_k
    kk = kk / jnp.maximum(jnp.sqrt(_per_slab(_bdot, kk * kk, gsum)), 1e-12)
    k_mod = k * (1.0 + (a - 1.0) * k_a)
    b = kk * a
    bonus_ref[...] = _per_slab(_dot_split2_rhs, r * k_mod * r_k, gsum) * v

    lane = lax.broadcasted_iota(jnp.int32, (CHUNK, PAIR), 1)
    head0 = lane < RW_HEAD_DIM
    strict = mask_ref[0]
    incl = mask_ref[1]
    eye = mask_ref[2]
    tri = tri_ref[...]

    def stack(x):
        return jnp.concatenate([jnp.where(head0, x, 0.0), jnp.where(head0, 0.0, x)], axis=0).astype(BF16)

    ops = {}
    for c in range(rows // CHUNK):
        rs = slice(c * CHUNK, (c + 1) * CHUNK)
        lw = logw[rs]
        cum = _dot_exact_lhs(tri, lw)
        cum_last = cum[CHUNK - 1:CHUNK]
        w_incl = jnp.exp(cum)
        w_inv = 1.0 / w_incl
        w_prev = jnp.exp(cum - lw)
        w_last = jnp.exp(cum_last)
        bt = b[rs] * w_inv
        kt = k_mod[rs] * w_inv
        full = dict(at=-kk[rs] * w_prev, rt=r[rs] * w_incl, bt=bt, kt=kt, be=bt * w_last, ke=kt * w_last, v2=v[rs])
        for p in range(RW_PAIRS):
            ls = slice(p * PAIR, (p + 1) * PAIR)
            item = {name: val[:, ls] for name, val in full.items()}
            item["w_last"] = w_last[:, ls]
            ops[(c, p)] = item

    keys = list(ops)
    for g0 in range(0, len(keys), RW1_GROUP):
        group = keys[g0:g0 + RW1_GROUP]
        n_ab, a_ak, a_rb, a_rk, t_inv, sv = {}, {}, {}, {}, {}, {}
        for key in group:
            o = ops[key]
            prod = _bdot_nt(jnp.concatenate([o["at"], o["rt"]], axis=0),
                            jnp.concatenate([stack(o["bt"]), stack(o["kt"])], axis=0))
            n_ab[key] = prod[:CHUNK, :PAIR] * strict
            a_ak[key] = prod[:CHUNK, PAIR:] * strict
            a_rb[key] = prod[CHUNK:, :PAIR] * incl
            a_rk[key] = prod[CHUNK:, PAIR:] * incl
            sv[key] = stack(o["v2"])
            t_inv[key] = eye + n_ab[key] * mask_ref[3]
        for lvl in range(1, INV_LEVELS):
            half = {key: _bdot(t_inv[key], stack(n_ab[key] * mask_ref[3 + lvl])) for key in group}
            t_inv = {key: t_inv[key] + _bdot(half[key], stack(t_inv[key])) for key in group}
        akv = {key: _bdot(a_ak[key], sv[key]) for key in group}
        pq = {key: _bdot(t_inv[key], jnp.concatenate([stack(ops[key]["at"]), stack(akv[key])], axis=1))
              for key in group}
        gy = {key: _bdot(a_rb[key], jnp.concatenate([stack(pq[key][:, :PAIR]), stack(pq[key][:, PAIR:])], axis=1))
              for key in group}
        ark_v = {key: _bdot(a_rk[key], sv[key]) for key in group}
        mc = {key: _bdot_tn(ops[key]["be"], pq[key]) for key in group}
        ke_v = {key: _bdot_tn(ops[key]["ke"], ops[key]["v2"]) for key in group}
        for key in group:
            c, p = key
            rs = slice(c * CHUNK, (c + 1) * CHUNK)
            ls = slice(p * PAIR, (p + 1) * PAIR)
            g_ref[rs, ls] = (ops[key]["rt"] + gy[key][:, :PAIR]).astype(BF16)
            yi_ref[rs, ls] = gy[key][:, PAIR:] + ark_v[key]
            cv = mc[key][:, PAIR:] + ke_v[key]
            m_ref[c, p] = eye * ops[key]["w_last"] + jnp.where(head0, mc[key][:CHUNK, :PAIR], mc[key][CHUNK:, :PAIR])
            c_ref[c, p] = jnp.where(head0, cv[:CHUNK], cv[CHUNK:])


def _rw1_call(zrw, w2p, a2p, g2, vec, gsum, tri, masks):
    t = zrw.shape[0]
    rows = RW1_CHUNKS * CHUNK
    nchunks = t // CHUNK
    tok = lambda i: (i, 0)
    blk = lambda i: (i, 0, 0, 0)
    tok_spec = pl.BlockSpec((rows, RW_WIDTH), tok)
    mc_spec = pl.BlockSpec((RW1_CHUNKS, RW_PAIRS, CHUNK, PAIR), blk)
    tok_shape = jax.ShapeDtypeStruct((t, RW_WIDTH), F32)
    mc_shape = jax.ShapeDtypeStruct((nchunks, RW_PAIRS, CHUNK, PAIR), F32)
    return pl.pallas_call(
        _rw1_kernel,
        grid=(t // rows,),
        in_specs=[pl.BlockSpec((rows, RW_COLS), tok),
                  _const_spec(w2p.shape), _const_spec(a2p.shape), _const_spec(g2.shape),
                  _const_spec(vec.shape), _const_spec(gsum.shape), _const_spec(tri.shape),
                  _const_spec(masks.shape)],
        out_specs=[tok_spec, tok_spec, tok_spec, tok_spec, mc_spec, mc_spec],
        out_shape=[jax.ShapeDtypeStruct((t, RW_WIDTH), BF16), tok_shape, tok_shape, tok_shape, mc_shape, mc_shape],
        compiler_params=_params(("parallel",)),
        name="rwkv_chunk_local",
    )(zrw, w2p, a2p, g2, vec, gsum, tri, masks)


RW2_CHUNKS = 4
RW2_BATCH = 4


def _rw2_kernel(g_ref, yi_ref, bonus_ref, gate_ref, m_ref, c_ref, gmean_ref, gn_ref, y_ref, h_ref):
    @pl.when(pl.program_id(1) == 0)
    def _():
        h_ref[...] = jnp.zeros(h_ref.shape, F32)

    lane = lax.broadcasted_iota(jnp.int32, (CHUNK, PAIR), 1)
    head0 = lane < RW_HEAD_DIM

    def block_diag(x):
        return jnp.concatenate([jnp.where(head0, x, 0.0), jnp.where(head0, 0.0, x)], axis=0)

    nb = g_ref.shape[0]
    chains = [(b, p) for b in range(nb) for p in range(RW_PAIRS)]
    lsl = [slice(p * PAIR, (p + 1) * PAIR) for p in range(RW_PAIRS)]
    state = {(b, p): h_ref[b, p] for b, p in chains}
    ys = {}
    for c in range(RW2_CHUNKS):
        rs = slice(c * CHUNK, (c + 1) * CHUNK)
        for b, p in chains:
            ys[b, c, p] = _bdot(g_ref[b, rs, lsl[p]], state[b, p]) + yi_ref[b, rs, lsl[p]]
        state = {(b, p): _dot_f32(block_diag(m_ref[b, c, p]), state[b, p]) + block_diag(c_ref[b, c, p])
                 for b, p in chains}
    for b, p in chains:
        h_ref[b, p] = state[b, p]
    gmean = gmean_ref[...]
    for b in range(nb):
        y = jnp.concatenate([jnp.concatenate([ys[b, c, p] for p in range(RW_PAIRS)], axis=1)
                             for c in range(RW2_CHUNKS)], axis=0)
        mu = _per_slab(_dot_split2_rhs, y, gmean)
        d = y - mu
        var = _per_slab(_bdot, d * d, gmean)
        y = d * lax.rsqrt(var + GN_EPS) * gn_ref[0:1, :] + gn_ref[1:2, :]
        y_ref[b] = ((y + bonus_ref[b]) * gate_ref[b]).astype(BF16)


def _rw2_call(g, yi, bonus, gate, m, c, gmean, gn, bsz, seq):
    t = g.shape[0]
    rows = RW2_CHUNKS * CHUNK
    nt = seq // rows
    nb = RW2_BATCH if bsz % RW2_BATCH == 0 else 1
    per_batch = lambda a: a.reshape((bsz, a.shape[0] // bsz) + a.shape[1:])
    tok_spec = pl.BlockSpec((nb, rows, RW_WIDTH), lambda b, i: (b, i, 0))
    mc_spec = pl.BlockSpec((nb, RW2_CHUNKS, RW_PAIRS, CHUNK, PAIR), lambda b, i: (b, i, 0, 0, 0))
    y = pl.pallas_call(
        _rw2_kernel,
        grid=(bsz // nb, nt),
        in_specs=[tok_spec, tok_spec, tok_spec, tok_spec, mc_spec, mc_spec,
                  _const_spec(gmean.shape), _const_spec(gn.shape)],
        out_specs=tok_spec,
        out_shape=jax.ShapeDtypeStruct((bsz, seq, RW_WIDTH), BF16),
        scratch_shapes=[pltpu.VMEM((nb, RW_PAIRS, PAIR, PAIR), F32)],
        compiler_params=_params(("parallel", "arbitrary")),
        name="rwkv_scan",
    )(per_batch(g), per_batch(yi), per_batch(bonus), per_batch(gate), per_batch(m), per_batch(c), gmean, gn)
    return y.reshape(t, RW_WIDTH)


ML_CHUNK = 128
ML_STEP_CHUNKS = 4
ML_BATCH = 1


def _log_sigmoid(x):
    return jnp.minimum(x, 0.0) - jnp.log(1.0 + jnp.exp(-jnp.abs(x)))


def _ml_kernel(z_ref, zif_ref, ifb_ref, norm_ref, tri_ref, y_ref, state_ref, m_ref):
    @pl.when(pl.program_id(1) == 0)
    def _():
        state_ref[...] = jnp.zeros(state_ref.shape, F32)
        m_ref[...] = jnp.zeros(m_ref.shape, F32)

    nb, rows = z_ref.shape[0], z_ref.shape[1]
    q_all = [z_ref[b, :, 0:ML_WIDTH] for b in range(nb)]
    k_all = [z_ref[b, :, ML_WIDTH:2 * ML_WIDTH] for b in range(nb)]
    v_all = [z_ref[b, :, 2 * ML_WIDTH:3 * ML_WIDTH] for b in range(nb)]
    ogate_all = [z_ref[b, :, 3 * ML_WIDTH:4 * ML_WIDTH] for b in range(nb)]

    pre = [zif_ref[b] + ifb_ref[...] for b in range(nb)]
    logf = [_log_sigmoid(pre[b]) for b in range(nb)]
    tri = tri_ref[...]
    lane = lax.broadcasted_iota(jnp.int32, (ML_CHUNK, LANES), 1)
    tq = lax.broadcasted_iota(jnp.int32, (ML_CHUNK, ML_CHUNK), 0)
    tk = lax.broadcasted_iota(jnp.int32, (ML_CHUNK, ML_CHUNK), 1)
    causal = tk <= tq
    ones = jnp.ones((ML_CHUNK, ML_HEAD_DIM), F32)
    dense = (ML_CHUNK, ML_CHUNK)

    chains = [(b, h) for b in range(nb) for h in range(ML_HEADS)]
    lsl = [slice(h * ML_HEAD_DIM, (h + 1) * ML_HEAD_DIM) for h in range(ML_HEADS)]
    m_run = {ch: m_ref[ch] for ch in chains}
    state_run = {ch: state_ref[ch] for ch in chains}
    ys = {b: [] for b in range(nb)}
    for c in range(rows // ML_CHUNK):
        rs = slice(c * ML_CHUNK, (c + 1) * ML_CHUNK)
        z, z_t = [], []
        for b in range(nb):
            bcum = _dot_exact_lhs(tri, logf[b][rs])
            z.append(jnp.where(lane < ML_HEADS, pre[b][rs], bcum))
            z_t.append(z[b].T)
        b_col = {(b, h): jnp.broadcast_to(z[b][:, ML_HEADS + h:ML_HEADS + h + 1], dense)
                 for b, h in chains}
        ib_col = {(b, h): jnp.broadcast_to(z[b][:, h:h + 1] - z[b][:, ML_HEADS + h:ML_HEADS + h + 1], dense)
                  for b, h in chains}
        ib_row = {(b, h): z_t[b][h:h + 1, :] - z_t[b][ML_HEADS + h:ML_HEADS + h + 1, :]
                  for b, h in chains}
        b_end = {ch: b_col[ch][ML_CHUNK - 1:ML_CHUNK] for ch in chains}
        m_prev = m_run
        state = state_run
        q = {(b, h): q_all[b][rs, lsl[h]] for b, h in chains}
        k = {(b, h): k_all[b][rs, lsl[h]] for b, h in chains}
        v = {(b, h): v_all[b][rs, lsl[h]] for b, h in chains}
        s_qk = {ch: _bdot_nt(q[ch], k[ch]) for ch in chains}
        inter = {ch: _bdot(q[ch], state[ch]) for ch in chains}

        log_d = {ch: jnp.where(causal, b_col[ch] + ib_row[ch], NEG_BIG) for ch in chains}
        log_inter = {ch: b_col[ch] + m_prev[ch] for ch in chains}
        m_row = {ch: jnp.maximum(log_inter[ch], jnp.max(log_d[ch], axis=-1, keepdims=True)) for ch in chains}
        p = {ch: jnp.exp(log_d[ch] - m_row[ch]) * s_qk[ch] for ch in chains}
        intra = {ch: _bdot(p[ch], jnp.concatenate([v[ch], ones], axis=1)) for ch in chains}

        log_w_end = {ch: ib_col[ch] + b_end[ch] for ch in chains}
        m_chunk = {ch: jnp.max(log_w_end[ch], axis=0, keepdims=True) for ch in chains}
        w_end = {ch: jnp.exp(log_w_end[ch] - m_chunk[ch]) for ch in chains}
        kv = {ch: _bdot_tn(k[ch], jnp.concatenate([w_end[ch] * v[ch], w_end[ch]], axis=1)) for ch in chains}
        m_run, state_run = {}, {}
        for ch in chains:
            m_new = jnp.maximum(b_end[ch] + m_prev[ch], m_chunk[ch])
            dec = jnp.exp(b_end[ch] + m_prev[ch] - m_new)
            sc = jnp.exp(m_chunk[ch] - m_new)
            state_run[ch] = (jnp.concatenate([dec, dec], axis=1) * state[ch]
                             + jnp.concatenate([sc, sc], axis=1) * kv[ch])
            m_run[ch] = m_new

        outs = {}
        for ch in chains:
            b, h = ch
            w_inter = jnp.exp(log_inter[ch] - m_row[ch])
            num = intra[ch][:, :ML_HEAD_DIM] + w_inter * inter[ch][:, :ML_HEAD_DIM]
            den = intra[ch][:, ML_HEAD_DIM:] + w_inter * inter[ch][:, ML_HEAD_DIM:]
            hh = num / jnp.maximum(jnp.abs(den), jnp.exp(-m_row[ch]))
            mu = jnp.mean(hh, axis=-1, keepdims=True)
            d = hh - mu
            var = jnp.mean(d * d, axis=-1, keepdims=True)
            hn = d * lax.rsqrt(var + ML_NORM_EPS)
            outs[ch] = hn * norm_ref[:, lsl[h]] * ogate_all[b][rs, lsl[h]]
        for b in range(nb):
            ys[b].append(jnp.concatenate([outs[b, h] for h in range(ML_HEADS)], axis=1).astype(BF16))
    for b in range(nb):
        y_ref[b] = jnp.concatenate(ys[b], axis=0)
    for ch in chains:
        state_ref[ch] = state_run[ch]
        m_ref[ch] = m_run[ch]


def _ml_call(zml, zif, ifb, norm, tri, bsz, seq):
    t = zml.shape[0]
    rows = ML_STEP_CHUNKS * ML_CHUNK
    nt = seq // rows
    nb = ML_BATCH if bsz % ML_BATCH == 0 else 1
    tok = lambda b, i: (b, i, 0)
    y = pl.pallas_call(
        _ml_kernel,
        grid=(bsz // nb, nt),
        in_specs=[pl.BlockSpec((nb, rows, ML_QKVO), tok), pl.BlockSpec((nb, rows, IF_PAD), tok),
                  _const_spec(ifb.shape), _const_spec(norm.shape), _const_spec(tri.shape)],
        out_specs=pl.BlockSpec((nb, rows, ML_WIDTH), tok),
        out_shape=jax.ShapeDtypeStruct((bsz, seq, ML_WIDTH), BF16),
        scratch_shapes=[pltpu.VMEM((nb, ML_HEADS, ML_HEAD_DIM, 2 * ML_HEAD_DIM), F32),
                        pltpu.VMEM((nb, ML_HEADS, 1, LANES), F32)],
        compiler_params=_params(("parallel", "arbitrary")),
        name="mlstm",
    )(zml.reshape(bsz, seq, ML_QKVO), zif.reshape(bsz, seq, IF_PAD), ifb, norm, tri)
    return y.reshape(t, ML_WIDTH)


MERGE_TM = 512


def _merge_kernel(x_ref, g_ref, wgt_ref, yrw_ref, yml_ref, wur_ref, wum_ref, wo_ref, o_ref):
    x = x_ref[...]
    d = x.shape[1]
    h = _rmsnorm(x, g_ref[...]).astype(BF16)
    gates = _sigmoid(jnp.dot(h, wgt_ref[...], preferred_element_type=F32))
    up_rw = jnp.dot(yrw_ref[...], wur_ref[...], preferred_element_type=F32)
    up_ml = jnp.dot(yml_ref[...], wum_ref[...], preferred_element_type=F32)
    merged = gates[:, 0:d] * up_rw + gates[:, d:] * up_ml
    o_ref[...] = x + _bdot(merged, wo_ref[...])


def _merge_call(x, g, wgt, yrw, yml, wur, wum, wo):
    t, d = x.shape
    tm = min(MERGE_TM, t)
    row = lambda i: (i, 0)
    return pl.pallas_call(
        _merge_kernel,
        grid=(t // tm,),
        in_specs=[pl.BlockSpec((tm, d), row), _const_spec(g.shape), _const_spec(wgt.shape),
                  pl.BlockSpec((tm, RW_WIDTH), row), pl.BlockSpec((tm, ML_WIDTH), row),
                  _const_spec(wur.shape), _const_spec(wum.shape), _const_spec(wo.shape)],
        out_specs=pl.BlockSpec((tm, d), row),
        out_shape=jax.ShapeDtypeStruct((t, d), F32),
        compiler_params=_params(("parallel",)),
        name="merge",
    )(x, g, wgt, yrw, yml, wur, wum, wo)


def _memkv_kernel(mem_ref, g_ref, wkv_ref, k_ref, v_ref):
    d = mem_ref.shape[1]
    hm = _rmsnorm(mem_ref[...], g_ref[...])
    kv = jnp.dot(hm, wkv_ref[...], preferred_element_type=F32)
    k_ref[...] = kv[:, 0:d].astype(BF16)
    v_ref[...] = kv[:, d:].astype(BF16)


def _memkv_call(mem2d, g, wkv):
    n, d = mem2d.shape
    row = lambda i: (i, 0)
    return pl.pallas_call(
        _memkv_kernel,
        grid=(n // MEM_LEN,),
        in_specs=[pl.BlockSpec((MEM_LEN, d), row), _const_spec(g.shape), _const_spec(wkv.shape)],
        out_specs=[pl.BlockSpec((MEM_LEN, d), row), pl.BlockSpec((MEM_LEN, d), row)],
        out_shape=[jax.ShapeDtypeStruct((n, d), BF16), jax.ShapeDtypeStruct((n, d), BF16)],
        compiler_params=_params(("parallel",)),
        name="mem_kv",
    )(mem2d, g, wkv)


XA_TM = 512


def _xattn_kernel(x_ref, g_ref, wq_ref, k_ref, v_ref, wo_ref, o_ref):
    x = x_ref[...]
    hq = _rmsnorm(x, g_ref[...])
    q = _bdot(hq, wq_ref[...])
    lsl = [slice(h * XA_HEAD_DIM, (h + 1) * XA_HEAD_DIM) for h in range(XA_HEADS)]

    def softmax(s):
        e = jnp.exp(s - jnp.max(s, axis=-1, keepdims=True))
        return e / jnp.sum(e, axis=-1, keepdims=True)

    outs = []
    s_prev = _bdot_nt(q[:, lsl[0]], k_ref[:, lsl[0]])
    for h in range(XA_HEADS):
        s_next = _bdot_nt(q[:, lsl[h + 1]], k_ref[:, lsl[h + 1]]) if h + 1 < XA_HEADS else None
        outs.append(_bdot(softmax(s_prev), v_ref[:, lsl[h]]))
        s_prev = s_next
    o = jnp.concatenate(outs, axis=1)
    o_ref[...] = x + _bdot(o, wo_ref[...])


def _xattn_call(x, g, wq, kmem, vmem, wo, bsz, seq):
    t, d = x.shape
    tm = min(XA_TM, seq)
    nt = seq // tm
    row = lambda b, i: (b * nt + i, 0)
    mem = lambda b, i: (b, 0)
    return pl.pallas_call(
        _xattn_kernel,
        grid=(bsz, nt),
        in_specs=[pl.BlockSpec((tm, d), row), _const_spec(g.shape), _const_spec(wq.shape),
                  pl.BlockSpec((MEM_LEN, d), mem), pl.BlockSpec((MEM_LEN, d), mem),
                  _const_spec(wo.shape)],
        out_specs=pl.BlockSpec((tm, d), row),
        out_shape=jax.ShapeDtypeStruct((t, d), F32),
        compiler_params=_params(("parallel", "parallel")),
        name="mem_xattn",
    )(x, g, wq, kmem, vmem, wo)


def _row(v):
    return v.reshape(1, -1).astype(F32)


def _layer(x, mem2d, bsz, seq, p):
    d = D_MODEL
    bf = lambda w: w.astype(BF16)
    masks_np, tri_np = _rw_masks()
    masks = jnp.asarray(masks_np)
    tri = jnp.asarray(tri_np, dtype=BF16)
    gsum = jnp.asarray(_group_ones(PAIR, RW_HEAD_DIM), dtype=BF16)
    gmean = jnp.asarray(_group_ones(PAIR, RW_HEAD_DIM) / RW_HEAD_DIM, dtype=BF16)

    x = _ffn_call(x, _row(p["ffn1_norm"]), bf(p["ffn1_w_gate"]), bf(p["ffn1_w_up"]), bf(p["ffn1_w_down"]),
                  _row(p["ffn1_norm"]), final=False)

    w_in = p["w_in"]
    c0 = RW_COLS
    c1 = c0 + ML_QKVO
    c2 = c1 + 2 * ML_HEADS
    w_if = jnp.pad(w_in[:, c1:c2], ((0, 0), (0, IF_PAD - 2 * ML_HEADS)))
    convw = jnp.concatenate([p["ml_conv_w"].astype(F32), jnp.zeros((SUBLANES - ML_CONV, 2 * ML_WIDTH), F32)], axis=0)
    zrw, zml, zif = _inproj_call(x, _row(p["mix_norm"]), bf(w_in[:, :c0]), bf(w_in[:, c0:c1]), bf(w_if),
                                 _row(p["rw_mu"]), convw, _row(p["ml_conv_b"]), bsz, seq)

    zpad = jnp.zeros((DECAY_LORA, RW_WIDTH), F32)
    w2p = bf(jnp.concatenate([p["rw_w2"], zpad], axis=0))
    a2p = bf(jnp.concatenate([zpad, p["rw_a2"]], axis=0))
    vec = jnp.stack([p["rw_w0"], p["rw_a0"], p["rw_k_k"], p["rw_k_a"], p["rw_r_k"].reshape(-1),
                     jnp.zeros_like(p["rw_w0"]), jnp.zeros_like(p["rw_w0"]), jnp.zeros_like(p["rw_w0"])]).astype(F32)
    g, yi, bonus, gate, m, c = _rw1_call(zrw, w2p, a2p, bf(p["rw_g2"]), vec, gsum, tri, masks)
    gn = jnp.concatenate([jnp.stack([p["rw_gn_w"], p["rw_gn_b"]]).astype(F32),
                          jnp.zeros((SUBLANES - 2, RW_WIDTH), F32)], axis=0)
    y_rw = _rw2_call(g, yi, bonus, gate, m, c, gmean, gn, bsz, seq)

    ifb = jnp.pad(jnp.concatenate([p["ml_i_b"], p["ml_f_b"]]).astype(F32), (0, IF_PAD - 2 * ML_HEADS)).reshape(1, -1)
    tri_ml = jnp.asarray(np.tril(np.ones((ML_CHUNK, ML_CHUNK), np.float32)), dtype=BF16)
    y_ml = _ml_call(zml, zif, ifb, _row(p["ml_norm"]), tri_ml, bsz, seq)

    x = _merge_call(x, _row(p["mix_norm"]), bf(w_in[:, c2:]), y_rw, y_ml,
                    bf(p["w_up_rw"]), bf(p["w_up_ml"]), bf(p["w_out"]))

    kmem, vmem = _memkv_call(mem2d, _row(p["mem_norm"]), p["xa_wkv"])
    xa_scale = XA_HEAD_DIM ** -0.5
    assert math.frexp(xa_scale)[0] == 0.5, "folding the scale into bf16 weights is exact only for powers of two"
    x = _xattn_call(x, _row(p["xa_norm"]), bf(p["xa_wq"] * xa_scale), kmem, vmem, bf(p["xa_wo"]), bsz, seq)
    return x


def kernel(x, mem, ffn1_norm, ffn1_w_gate, ffn1_w_up, ffn1_w_down, mix_norm, w_in, rw_mu, rw_w0, rw_w2, rw_a0,
           rw_a2, rw_g2, rw_k_k, rw_k_a, rw_r_k, rw_gn_w, rw_gn_b, ml_conv_w, ml_conv_b, ml_i_b, ml_f_b, ml_norm,
           w_up_rw, w_up_ml, w_out, xa_norm, mem_norm, xa_wq, xa_wkv, xa_wo, ffn2_norm, ffn2_w_gate, ffn2_w_up,
           ffn2_w_down, final_norm):
    bsz, seq, d = x.shape
    depth = ffn1_norm.shape[0]
    stacked = dict(ffn1_norm=ffn1_norm, ffn1_w_gate=ffn1_w_gate, ffn1_w_up=ffn1_w_up, ffn1_w_down=ffn1_w_down,
                   mix_norm=mix_norm, w_in=w_in, rw_mu=rw_mu, rw_w0=rw_w0, rw_w2=rw_w2, rw_a0=rw_a0, rw_a2=rw_a2,
                   rw_g2=rw_g2, rw_k_k=rw_k_k, rw_k_a=rw_k_a, rw_r_k=rw_r_k, rw_gn_w=rw_gn_w, rw_gn_b=rw_gn_b,
                   ml_conv_w=ml_conv_w, ml_conv_b=ml_conv_b, ml_i_b=ml_i_b, ml_f_b=ml_f_b, ml_norm=ml_norm,
                   w_up_rw=w_up_rw, w_up_ml=w_up_ml, w_out=w_out, xa_norm=xa_norm, mem_norm=mem_norm,
                   xa_wq=xa_wq, xa_wkv=xa_wkv, xa_wo=xa_wo, ffn2_norm=ffn2_norm, ffn2_w_gate=ffn2_w_gate,
                   ffn2_w_up=ffn2_w_up, ffn2_w_down=ffn2_w_down)
    h = x.reshape(bsz * seq, d)
    mem2d = mem.reshape(bsz * mem.shape[1], d)
    bf = lambda w: w.astype(BF16)
    for l in range(depth):
        p = {name: w[l] for name, w in stacked.items()}
        h = _layer(h, mem2d, bsz, seq, p)
        last = l == depth - 1
        h = _ffn_call(h, _row(p["ffn2_norm"]), bf(p["ffn2_w_gate"]), bf(p["ffn2_w_up"]), bf(p["ffn2_w_down"]),
                      _row(final_norm), final=last)
    return h.reshape(bsz, seq, d)
```

```python
import functools
import math

import jax
import jax.numpy as jnp
import numpy as np
from jax import lax
from jax.experimental import pallas as pl
from jax.experimental.pallas import tpu as pltpu

F32 = jnp.float32
BF16 = jnp.bfloat16

D_MODEL = 1024
D_FF = 2816
FFN_HALF = 0.5
NORM_EPS = 1e-5
MEM_LEN = 256

RW_HEADS = 8
RW_HEAD_DIM = 64
RW_WIDTH = 512
DECAY_LORA = 64
AAA_LORA = 64
GATE_LORA = 128
RW_COLS = 3 * RW_WIDTH + DECAY_LORA + AAA_LORA + GATE_LORA
GN_EPS = 64e-5
RW_PAIRS = RW_HEADS // 2
PAIR = 2 * RW_HEAD_DIM

ML_HEADS = 4
ML_HEAD_DIM = 128
ML_WIDTH = 512
ML_CONV = 4
ML_NORM_EPS = 1e-6
ML_QKVO = 4 * ML_WIDTH
IF_PAD = 128

XA_HEADS = 4
XA_HEAD_DIM = 256

CHUNK = 64
LANES = 128
SUBLANES = 8
NEG_BIG = -1e30

VMEM_LIMIT = 56 * 1024 * 1024


def _bdot(a, b):
    return jnp.dot(a.astype(BF16), b.astype(BF16), preferred_element_type=F32)


def _bdot_nt(a, b):
    return lax.dot_general(a.astype(BF16), b.astype(BF16), (((1,), (1,)), ((), ())),
                           preferred_element_type=F32)


def _bdot_tn(a, b):
    return lax.dot_general(a.astype(BF16), b.astype(BF16), (((0,), (0,)), ((), ())),
                           preferred_element_type=F32)


def _split3(x):
    hi = x.astype(BF16)
    r1 = x - hi.astype(F32)
    mid = r1.astype(BF16)
    lo = (r1 - mid.astype(F32)).astype(BF16)
    return hi, mid, lo


def _dot_exact_lhs(c_bf16, x):
    hi, mid, lo = _split3(x)
    f = lambda p: jnp.dot(c_bf16, p, preferred_element_type=F32)
    return f(hi) + f(mid) + f(lo)


def _dot_split2_rhs(x, c_bf16):
    hi = x.astype(BF16)
    lo = (x - hi.astype(F32)).astype(BF16)
    f = lambda p: jnp.dot(p, c_bf16, preferred_element_type=F32)
    return f(hi) + f(lo)


def _per_slab(fn, x, c_bf16):
    n = x.shape[1] // LANES
    return jnp.concatenate([fn(x[:, i * LANES:(i + 1) * LANES], c_bf16) for i in range(n)], axis=1)


def _dot_f32(a, b):
    a_hi = a.astype(BF16)
    a_lo = (a - a_hi.astype(F32)).astype(BF16)
    b_hi = b.astype(BF16)
    b_lo = (b - b_hi.astype(F32)).astype(BF16)
    f = lambda p, q: jnp.dot(p, q, preferred_element_type=F32)
    return f(a_hi, b_hi) + f(a_hi, b_lo) + f(a_lo, b_hi)


def _sigmoid(x):
    return 1.0 / (1.0 + jnp.exp(-x))


def _silu(x):
    return x * _sigmoid(x)


def _rmsnorm(x, g):
    ms = jnp.mean(x * x, axis=-1, keepdims=True)
    return x * lax.rsqrt(ms + NORM_EPS) * g


def _shift_rows(u, carry8, j):
    rolled = pltpu.roll(u, j, axis=0)
    rolled_c = pltpu.roll(carry8, j, axis=0)
    row = lax.broadcasted_iota(jnp.int32, (SUBLANES, u.shape[1]), 0)
    top = jnp.where(row < j, rolled_c, rolled[:SUBLANES])
    return jnp.concatenate([top, rolled[SUBLANES:]], axis=0)


def _shift_rows_via(buf_ref, u, carry8, j):
    rows = u.shape[0]
    buf_ref[0:SUBLANES, :] = carry8
    buf_ref[SUBLANES:, :] = u
    return buf_ref[SUBLANES - j:SUBLANES - j + rows, :]


def _const_spec(shape):
    nd = len(shape)
    return pl.BlockSpec(shape, lambda *_: (0,) * nd, pipeline_mode=pl.Buffered(1))


def _params(sem):
    return pltpu.CompilerParams(dimension_semantics=sem, vmem_limit_bytes=VMEM_LIMIT)


FF_CHUNK = 256
FFN_TM = 512


def _ffn_kernel(x_ref, g_ref, wg_ref, wu_ref, wd_ref, fg_ref, o_ref, *, final):
    x = x_ref[...]
    h = _rmsnorm(x, g_ref[...]).astype(BF16)
    acc = jnp.zeros(x.shape, F32)
    for c in range(D_FF // FF_CHUNK):
        sl = slice(c * FF_CHUNK, (c + 1) * FF_CHUNK)
        gate = jnp.dot(h, wg_ref[:, sl].astype(BF16), preferred_element_type=F32)
        up = jnp.dot(h, wu_ref[:, sl].astype(BF16), preferred_element_type=F32)
        act = (_silu(gate) * up).astype(BF16)
        acc = acc + jnp.dot(act, wd_ref[sl, :].astype(BF16), preferred_element_type=F32)
    y = x + FFN_HALF * acc
    if final:
        y = _rmsnorm(y, fg_ref[...])
    o_ref[...] = y


def _ffn_call(x, g, wg, wu, wd, fg, final):
    t, d = x.shape
    tm = min(FFN_TM, t)
    return pl.pallas_call(
        functools.partial(_ffn_kernel, final=final),
        grid=(t // tm,),
        in_specs=[pl.BlockSpec((tm, d), lambda i: (i, 0)),
                  _const_spec((1, d)), _const_spec((d, D_FF)), _const_spec((d, D_FF)),
                  _const_spec((D_FF, d)), _const_spec((1, d))],
        out_specs=pl.BlockSpec((tm, d), lambda i: (i, 0)),
        out_shape=jax.ShapeDtypeStruct((t, d), F32),
        compiler_params=_params(("parallel",)),
        name="ffn_final" if final else "ffn",
    )(x, g, wg, wu, wd, fg)


INPROJ_TM = 512
INPROJ_COLS = 256


def _inproj_kernel(x_ref, g_ref, wrw_ref, wml_ref, wif_ref, mu_ref, convw_ref, convb_ref,
                   zrw_ref, zml_ref, zif_ref, carry_rw_ref, carry_ml_ref, buf_a_ref, buf_b_ref):
    @pl.when(pl.program_id(1) == 0)
    def _():
        carry_rw_ref[...] = jnp.zeros(carry_rw_ref.shape, F32)
        carry_ml_ref[...] = jnp.zeros(carry_ml_ref.shape, F32)

    rows = x_ref.shape[0]
    h = _rmsnorm(x_ref[...], g_ref[...]).astype(BF16)

    def conv_stage(cols, scale):
        def finish(u):
            carry = carry_ml_ref[:, cols]
            w0, w1, w2, w3 = (convw_ref[j:j + 1, cols] for j in range(ML_CONV))
            u1 = _shift_rows_via(buf_a_ref, u, carry, 1)
            pair = w2 * u + w3 * u1
            pair_carry = w2 * carry + w3 * pltpu.roll(carry, 1, axis=0)
            conv = convb_ref[:, cols] + w0 * u + w1 * u1 + _shift_rows_via(buf_b_ref, pair, pair_carry, 2)
            carry_ml_ref[:, cols] = u[rows - SUBLANES:]
            zml_ref[:, cols] = _silu(conv) * scale if scale != 1.0 else _silu(conv)
        return wml_ref, cols, finish

    def plain_stage(cols, fn):
        def finish(u):
            zml_ref[:, cols] = fn(u)
        return wml_ref, cols, finish

    def lerp_stage(cols):
        def finish(z):
            prev = _shift_rows_via(buf_a_ref, z, carry_rw_ref[:, cols], 1)
            carry_rw_ref[:, cols] = z[rows - SUBLANES:]
            zrw_ref[:, cols] = z + (prev - z) * mu_ref[:, cols]
        return wrw_ref, cols, finish

    blk = lambda c: slice(c * INPROJ_COLS, (c + 1) * INPROJ_COLS)
    n_head_blocks = ML_WIDTH // INPROJ_COLS
    heavy, light = [], []
    for c in range(ML_QKVO // INPROJ_COLS):
        group = c // n_head_blocks
        if group < 2:
            heavy.append(conv_stage(blk(c), ML_HEAD_DIM ** -0.5 if group == 0 else 1.0))
        else:
            light.append(plain_stage(blk(c), _sigmoid if group == 3 else (lambda u: u)))
    light += [lerp_stage(blk(c)) for c in range(RW_COLS // INPROJ_COLS)]

    per_heavy = -(-len(light) // len(heavy))
    stages = []
    for i, stage in enumerate(heavy):
        stages.append(stage)
        stages += light[i * per_heavy:(i + 1) * per_heavy]
    pending = None
    for w_ref, cols, finish in stages:
        acc = jnp.dot(h, w_ref[:, cols], preferred_element_type=F32)
        if pending is not None:
            pending[0](pending[1])
        pending = (finish, acc)
    zif_ref[...] = jnp.dot(h, wif_ref[...], preferred_element_type=F32)
    pending[0](pending[1])


def _inproj_call(x, g, wrw, wml, wif, mu, convw, convb, bsz, seq):
    t, d = x.shape
    tm = min(INPROJ_TM, seq)
    nt = seq // tm
    row = lambda b, i: (b * nt + i, 0)
    return pl.pallas_call(
        _inproj_kernel,
        grid=(bsz, nt),
        in_specs=[pl.BlockSpec((tm, d), row),
                  _const_spec((1, d)), _const_spec((d, RW_COLS)), _const_spec((d, ML_QKVO)),
                  _const_spec((d, IF_PAD)), _const_spec((1, RW_COLS)),
                  _const_spec(convw.shape), _const_spec(convb.shape)],
        out_specs=[pl.BlockSpec((tm, RW_COLS), row), pl.BlockSpec((tm, ML_QKVO), row),
                   pl.BlockSpec((tm, IF_PAD), row)],
        out_shape=[jax.ShapeDtypeStruct((t, RW_COLS), F32), jax.ShapeDtypeStruct((t, ML_QKVO), F32),
                   jax.ShapeDtypeStruct((t, IF_PAD), F32)],
        scratch_shapes=[pltpu.VMEM((SUBLANES, RW_COLS), F32), pltpu.VMEM((SUBLANES, 2 * ML_WIDTH), F32),
                        pltpu.VMEM((SUBLANES + tm, INPROJ_COLS), F32),
                        pltpu.VMEM((SUBLANES + tm, INPROJ_COLS), F32)],
        compiler_params=_params(("parallel", "arbitrary")),
        name="inproj",
    )(x, g, wrw, wml, wif, mu, convw, convb)


RW1_CHUNKS = 8
RW1_GROUP = 16
INV_LEVELS = 6


def _rw_masks():
    t = np.arange(CHUNK)[:, None]
    s = np.tile(np.arange(CHUNK), 2)[None, :]
    levels = []
    for k in range(INV_LEVELS):
        size = 1 << k
        levels.append(((t // (2 * size)) == (s // (2 * size))) & ((t // size) % 2 == 1) & ((s // size) % 2 == 0))
    m = np.stack([s < t, s <= t, s == t] + levels).astype(np.float32)
    tri = (np.arange(CHUNK)[:, None] >= np.arange(CHUNK)[None, :]).astype(np.float32)
    return m, tri


def _group_ones(width, group):
    i = np.arange(width)
    return (i[:, None] // group == i[None, :] // group).astype(np.float32)


def _rw1_kernel(z_ref, w2_ref, a2_ref, g2_ref, vec_ref, gsum_ref, tri_ref, mask_ref,
                g_ref, yi_ref, bonus_ref, gate_ref, m_ref, c_ref):
    z = z_ref[...]
    rows = z.shape[0]
    r = z[:, 0:RW_WIDTH]
    k = z[:, RW_WIDTH:2 * RW_WIDTH]
    v = z[:, 2 * RW_WIDTH:3 * RW_WIDTH]
    lora_in = z[:, 3 * RW_WIDTH:3 * RW_WIDTH + DECAY_LORA + AAA_LORA]
    g_d = z[:, 3 * RW_WIDTH + DECAY_LORA + AAA_LORA:]
    w0 = vec_ref[0:1, :]
    a0 = vec_ref[1:2, :]
    k_k = vec_ref[2:3, :]
    k_a = vec_ref[3:4, :]
    r_k = vec_ref[4:5, :]

    w_raw = w0 + _bdot(jnp.tanh(lora_in), w2_ref[...])
    logw = -math.exp(-0.5) * _sigmoid(w_raw)
    a = _sigmoid(a0 + _bdot(lora_in, a2_ref[...]))
    gate_ref[...] = _bdot(_sigmoid(g_d), g2_ref[...])

    gsum = gsum_ref[...]
    kk = k * k_k
    kk = kk / jnp.maximum(jnp.sqrt(_per_slab(_bdot, kk * kk, gsum)), 1e-12)
    k_mod = k * (1.0 + (a - 1.0) * k_a)
    b = kk * a
    bonus_ref[...] = _per_slab(_dot_split2_rhs, r * k_mod * r_k, gsum) * v

    lane = lax.broadcasted_iota(jnp.int32, (CHUNK, PAIR), 1)
    head0 = lane < RW_HEAD_DIM
    strict = mask_ref[0]
    incl = mask_ref[1]
    eye = mask_ref[2]
    tri = tri_ref[...]

    def stack(x):
        return jnp.concatenate([jnp.where(head0, x, 0.0), jnp.where(head0, 0.0, x)], axis=0).astype(BF16)

    ops = {}
    for c in range(rows // CHUNK):
        rs = slice(c * CHUNK, (c + 1) * CHUNK)
        lw = logw[rs]
        cum = _dot_exact_lhs(tri, lw)
        cum_last = cum[CHUNK - 1:CHUNK]
        w_incl = jnp.exp(cum)
        w_inv = 1.0 / w_incl
        w_prev = jnp.exp(cum - lw)
        w_last = jnp.exp(cum_last)
        bt = b[rs] * w_inv
        kt = k_mod[rs] * w_inv
        full = dict(at=-kk[rs] * w_prev, rt=r[rs] * w_incl, bt=bt, kt=kt, be=bt * w_last, ke=kt * w_last, v2=v[rs])
        for p in range(RW_PAIRS):
            ls = slice(p * PAIR, (p + 1) * PAIR)
            item = {name: val[:, ls] for name, val in full.items()}
            item["w_last"] = w_last[:, ls]
            ops[(c, p)] = item

    keys = list(ops)
    for g0 in range(0, len(keys), RW1_GROUP):
        group = keys[g0:g0 + RW1_GROUP]
        n_ab, a_ak, a_rb, a_rk, t_inv, sv = {}, {}, {}, {}, {}, {}
        for key in group:
            o = ops[key]
            prod = _bdot_nt(jnp.concatenate([o["at"], o["rt"]], axis=0),
                            jnp.concatenate([stack(o["bt"]), stack(o["kt"])], axis=0))
            n_ab[key] = prod[:CHUNK, :PAIR] * strict
            a_ak[key] = prod[:CHUNK, PAIR:] * strict
            a_rb[key] = prod[CHUNK:, :PAIR] * incl
            a_rk[key] = prod[CHUNK:, PAIR:] * incl
            sv[key] = stack(o["v2"])
            t_inv[key] = eye + n_ab[key] * mask_ref[3]
        for lvl in range(1, INV_LEVELS):
            half = {key: _bdot(t_inv[key], stack(n_ab[key] * mask_ref[3 + lvl])) for key in group}
            t_inv = {key: t_inv[key] + _bdot(half[key], stack(t_inv[key])) for key in group}
        akv = {key: _bdot(a_ak[key], sv[key]) for key in group}
        pq = {key: _bdot(t_inv[key], jnp.concatenate([stack(ops[key]["at"]), stack(akv[key])], axis=1))
              for key in group}
        gy = {key: _bdot(a_rb[key], jnp.concatenate([stack(pq[key][:, :PAIR]), stack(pq[key][:, PAIR:])], axis=1))
              for key in group}
        ark_v = {key: _bdot(a_rk[key], sv[key]) for key in group}
        mc = {key: _bdot_tn(ops[key]["be"], pq[key]) for key in group}
        ke_v = {key: _bdot_tn(ops[key]["ke"], ops[key]["v2"]) for key in group}
        for key in group:
            c, p = key
            rs = slice(c * CHUNK, (c + 1) * CHUNK)
            ls = slice(p * PAIR, (p + 1) * PAIR)
            g_ref[rs, ls] = (ops[key]["rt"] + gy[key][:, :PAIR]).astype(BF16)
            yi_ref[rs, ls] = gy[key][:, PAIR:] + ark_v[key]
            cv = mc[key][:, PAIR:] + ke_v[key]
            m_ref[c, p] = eye * ops[key]["w_last"] + jnp.where(head0, mc[key][:CHUNK, :PAIR], mc[key][CHUNK:, :PAIR])
            c_ref[c, p] = jnp.where(head0, cv[:CHUNK], cv[CHUNK:])


def _rw1_call(zrw, w2p, a2p, g2, vec, gsum, tri, masks):
    t = zrw.shape[0]
    rows = RW1_CHUNKS * CHUNK
    nchunks = t // CHUNK
    tok = lambda i: (i, 0)
    blk = lambda i: (i, 0, 0, 0)
    tok_spec = pl.BlockSpec((rows, RW_WIDTH), tok)
    mc_spec = pl.BlockSpec((RW1_CHUNKS, RW_PAIRS, CHUNK, PAIR), blk)
    tok_shape = jax.ShapeDtypeStruct((t, RW_WIDTH), F32)
    mc_shape = jax.ShapeDtypeStruct((nchunks, RW_PAIRS, CHUNK, PAIR), F32)
    return pl.pallas_call(
        _rw1_kernel,
        grid=(t // rows,),
        in_specs=[pl.BlockSpec((rows, RW_COLS), tok),
                  _const_spec(w2p.shape), _const_spec(a2p.shape), _const_spec(g2.shape),
                  _const_spec(vec.shape), _const_spec(gsum.shape), _const_spec(tri.shape),
                  _const_spec(masks.shape)],
        out_specs=[tok_spec, tok_spec, tok_spec, tok_spec, mc_spec, mc_spec],
        out_shape=[jax.ShapeDtypeStruct((t, RW_WIDTH), BF16), tok_shape, tok_shape, tok_shape, mc_shape, mc_shape],
        compiler_params=_params(("parallel",)),
        name="rwkv_chunk_local",
    )(zrw, w2p, a2p, g2, vec, gsum, tri, masks)


RW2_CHUNKS = 4
RW2_BATCH = 4


def _rw2_kernel(g_ref, yi_ref, bonus_ref, gate_ref, m_ref, c_ref, gmean_ref, gn_ref, y_ref, h_ref):
    @pl.when(pl.program_id(1) == 0)
    def _():
        h_ref[...] = jnp.zeros(h_ref.shape, F32)

    lane = lax.broadcasted_iota(jnp.int32, (CHUNK, PAIR), 1)
    head0 = lane < RW_HEAD_DIM

    def block_diag(x):
        return jnp.concatenate([jnp.where(head0, x, 0.0), jnp.where(head0, 0.0, x)], axis=0)

    nb = g_ref.shape[0]
    chains = [(b, p) for b in range(nb) for p in range(RW_PAIRS)]
    lsl = [slice(p * PAIR, (p + 1) * PAIR) for p in range(RW_PAIRS)]
    state = {(b, p): h_ref[b, p] for b, p in chains}
    ys = {}
    for c in range(RW2_CHUNKS):
        rs = slice(c * CHUNK, (c + 1) * CHUNK)
        for b, p in chains:
            ys[b, c, p] = _bdot(g_ref[b, rs, lsl[p]], state[b, p]) + yi_ref[b, rs, lsl[p]]
        state = {(b, p): _dot_f32(block_diag(m_ref[b, c, p]), state[b, p]) + block_diag(c_ref[b, c, p])
                 for b, p in chains}
    for b, p in chains:
        h_ref[b, p] = state[b, p]
    gmean = gmean_ref[...]
    for b in range(nb):
        y = jnp.concatenate([jnp.concatenate([ys[b, c, p] for p in range(RW_PAIRS)], axis=1)
                             for c in range(RW2_CHUNKS)], axis=0)
        mu = _per_slab(_dot_split2_rhs, y, gmean)
        d = y - mu
        var = _per_slab(_bdot, d * d, gmean)
        y = d * lax.rsqrt(var + GN_EPS) * gn_ref[0:1, :] + gn_ref[1:2, :]
        y_ref[b] = ((y + bonus_ref[b]) * gate_ref[b]).astype(BF16)


def _rw2_call(g, yi, bonus, gate, m, c, gmean, gn, bsz, seq):
    t = g.shape[0]
    rows = RW2_CHUNKS * CHUNK
    nt = seq // rows
    nb = RW2_BATCH if bsz % RW2_BATCH == 0 else 1
    per_batch = lambda a: a.reshape((bsz, a.shape[0] // bsz) + a.shape[1:])
    tok_spec = pl.BlockSpec((nb, rows, RW_WIDTH), lambda b, i: (b, i, 0))
    mc_spec = pl.BlockSpec((nb, RW2_CHUNKS, RW_PAIRS, CHUNK, PAIR), lambda b, i: (b, i, 0, 0, 0))
    y = pl.pallas_call(
        _rw2_kernel,
        grid=(bsz // nb, nt),
        in_specs=[tok_spec, tok_spec, tok_spec, tok_spec, mc_spec, mc_spec,
                  _const_spec(gmean.shape), _const_spec(gn.shape)],
        out_specs=tok_spec,
        out_shape=jax.ShapeDtypeStruct((bsz, seq, RW_WIDTH), BF16),
        scratch_shapes=[pltpu.VMEM((nb, RW_PAIRS, PAIR, PAIR), F32)],
        compiler_params=_params(("parallel", "arbitrary")),
        name="rwkv_scan",
    )(per_batch(g), per_batch(yi), per_batch(bonus), per_batch(gate), per_batch(m), per_batch(c), gmean, gn)
    return y.reshape(t, RW_WIDTH)


ML_CHUNK = 128
ML_STEP_CHUNKS = 4
ML_BATCH = 1


def _log_sigmoid(x):
    return jnp.minimum(x, 0.0) - jnp.log(1.0 + jnp.exp(-jnp.abs(x)))


def _ml_kernel(z_ref, zif_ref, ifb_ref, norm_ref, tri_ref, y_ref, state_ref, m_ref):
    @pl.when(pl.program_id(1) == 0)
    def _():
        state_ref[...] = jnp.zeros(state_ref.shape, F32)
        m_ref[...] = jnp.zeros(m_ref.shape, F32)

    nb, rows = z_ref.shape[0], z_ref.shape[1]
    q_all = [z_ref[b, :, 0:ML_WIDTH] for b in range(nb)]
    k_all = [z_ref[b, :, ML_WIDTH:2 * ML_WIDTH] for b in range(nb)]
    v_all = [z_ref[b, :, 2 * ML_WIDTH:3 * ML_WIDTH] for b in range(nb)]
    ogate_all = [z_ref[b, :, 3 * ML_WIDTH:4 * ML_WIDTH] for b in range(nb)]

    pre = [zif_ref[b] + ifb_ref[...] for b in range(nb)]
    logf = [_log_sigmoid(pre[b]) for b in range(nb)]
    tri = tri_ref[...]
    lane = lax.broadcasted_iota(jnp.int32, (ML_CHUNK, LANES), 1)
    tq = lax.broadcasted_iota(jnp.int32, (ML_CHUNK, ML_CHUNK), 0)
    tk = lax.broadcasted_iota(jnp.int32, (ML_CHUNK, ML_CHUNK), 1)
    causal = tk <= tq
    ones = jnp.ones((ML_CHUNK, ML_HEAD_DIM), F32)
    dense = (ML_CHUNK, ML_CHUNK)

    chains = [(b, h) for b in range(nb) for h in range(ML_HEADS)]
    lsl = [slice(h * ML_HEAD_DIM, (h + 1) * ML_HEAD_DIM) for h in range(ML_HEADS)]
    m_run = {ch: m_ref[ch] for ch in chains}
    state_run = {ch: state_ref[ch] for ch in chains}
    ys = {b: [] for b in range(nb)}
    for c in range(rows // ML_CHUNK):
        rs = slice(c * ML_CHUNK, (c + 1) * ML_CHUNK)
        z, z_t = [], []
        for b in range(nb):
            bcum = _dot_exact_lhs(tri, logf[b][rs])
            z.append(jnp.where(lane < ML_HEADS, pre[b][rs], bcum))
            z_t.append(z[b].T)
        b_col = {(b, h): jnp.broadcast_to(z[b][:, ML_HEADS + h:ML_HEADS + h + 1], dense)
                 for b, h in chains}
        ib_col = {(b, h): jnp.broadcast_to(z[b][:, h:h + 1] - z[b][:, ML_HEADS + h:ML_HEADS + h + 1], dense)
                  for b, h in chains}
        ib_row = {(b, h): z_t[b][h:h + 1, :] - z_t[b][ML_HEADS + h:ML_HEADS + h + 1, :]
                  for b, h in chains}
        b_end = {ch: b_col[ch][ML_CHUNK - 1:ML_CHUNK] for ch in chains}
        m_prev = m_run
        state = state_run
        q = {(b, h): q_all[b][rs, lsl[h]] for b, h in chains}
        k = {(b, h): k_all[b][rs, lsl[h]] for b, h in chains}
        v = {(b, h): v_all[b][rs, lsl[h]] for b, h in chains}
        s_qk = {ch: _bdot_nt(q[ch], k[ch]) for ch in chains}
        inter = {ch: _bdot(q[ch], state[ch]) for ch in chains}

        log_d = {ch: jnp.where(causal, b_col[ch] + ib_row[ch], NEG_BIG) for ch in chains}
        log_inter = {ch: b_col[ch] + m_prev[ch] for ch in chains}
        m_row = {ch: jnp.maximum(log_inter[ch], jnp.max(log_d[ch], axis=-1, keepdims=True)) for ch in chains}
        p = {ch: jnp.exp(log_d[ch] - m_row[ch]) * s_qk[ch] for ch in chains}
        intra = {ch: _bdot(p[ch], jnp.concatenate([v[ch], ones], axis=1)) for ch in chains}

        log_w_end = {ch: ib_col[ch] + b_end[ch] for ch in chains}
        m_chunk = {ch: jnp.max(log_w_end[ch], axis=0, keepdims=True) for ch in chains}
        w_end = {ch: jnp.exp(log_w_end[ch] - m_chunk[ch]) for ch in chains}
        kv = {ch: _bdot_tn(k[ch], jnp.concatenate([w_end[ch] * v[ch], w_end[ch]], axis=1)) for ch in chains}
        m_run, state_run = {}, {}
        for ch in chains:
            m_new = jnp.maximum(b_end[ch] + m_prev[ch], m_chunk[ch])
            dec = jnp.exp(b_end[ch] + m_prev[ch] - m_new)
            sc = jnp.exp(m_chunk[ch] - m_new)
            state_run[ch] = (jnp.concatenate([dec, dec], axis=1) * state[ch]
                             + jnp.concatenate([sc, sc], axis=1) * kv[ch])
            m_run[ch] = m_new

        outs = {}
        for ch in chains:
            b, h = ch
            w_inter = jnp.exp(log_inter[ch] - m_row[ch])
            num = intra[ch][:, :ML_HEAD_DIM] + w_inter * inter[ch][:, :ML_HEAD_DIM]
            den = intra[ch][:, ML_HEAD_DIM:] + w_inter * inter[ch][:, ML_HEAD_DIM:]
            hh = num / jnp.maximum(jnp.abs(den), jnp.exp(-m_row[ch]))
            mu = jnp.mean(hh, axis=-1, keepdims=True)
            d = hh - mu
            var = jnp.mean(d * d, axis=-1, keepdims=True)
            hn = d * lax.rsqrt(var + ML_NORM_EPS)
            outs[ch] = hn * norm_ref[:, lsl[h]] * ogate_all[b][rs, lsl[h]]
        for b in range(nb):
            ys[b].append(jnp.concatenate([outs[b, h] for h in range(ML_HEADS)], axis=1).astype(BF16))
    for b in range(nb):
        y_ref[b] = jnp.concatenate(ys[b], axis=0)
    for ch in chains:
        state_ref[ch] = state_run[ch]
        m_ref[ch] = m_run[ch]


def _ml_call(zml, zif, ifb, norm, tri, bsz, seq):
    t = zml.shape[0]
    rows = ML_STEP_CHUNKS * ML_CHUNK
    nt = seq // rows
    nb = ML_BATCH if bsz % ML_BATCH == 0 else 1
    tok = lambda b, i: (b, i, 0)
    y = pl.pallas_call(
        _ml_kernel,
        grid=(bsz // nb, nt),
        in_specs=[pl.BlockSpec((nb, rows, ML_QKVO), tok), pl.BlockSpec((nb, rows, IF_PAD), tok),
                  _const_spec(ifb.shape), _const_spec(norm.shape), _const_spec(tri.shape)],
        out_specs=pl.BlockSpec((nb, rows, ML_WIDTH), tok),
        out_shape=jax.ShapeDtypeStruct((bsz, seq, ML_WIDTH), BF16),
        scratch_shapes=[pltpu.VMEM((nb, ML_HEADS, ML_HEAD_DIM, 2 * ML_HEAD_DIM), F32),
                        pltpu.VMEM((nb, ML_HEADS, 1, LANES), F32)],
        compiler_params=_params(("parallel", "arbitrary")),
        name="mlstm",
    )(zml.reshape(bsz, seq, ML_QKVO), zif.reshape(bsz, seq, IF_PAD), ifb, norm, tri)
    return y.reshape(t, ML_WIDTH)


MERGE_TM = 512


def _merge_kernel(x_ref, g_ref, wgt_ref, yrw_ref, yml_ref, wur_ref, wum_ref, wo_ref, o_ref):
    x = x_ref[...]
    d = x.shape[1]
    h = _rmsnorm(x, g_ref[...]).astype(BF16)
    gates = _sigmoid(jnp.dot(h, wgt_ref[...], preferred_element_type=F32))
    up_rw = _bdot(yrw_ref[...], wur_ref[...])
    up_ml = _bdot(yml_ref[...], wum_ref[...])
    merged = gates[:, 0:d] * up_rw + gates[:, d:] * up_ml
    o_ref[...] = x + _bdot(merged, wo_ref[...])


def _merge_call(x, g, wgt, yrw, yml, wur, wum, wo):
    t, d = x.shape
    tm = min(MERGE_TM, t)
    row = lambda i: (i, 0)
    return pl.pallas_call(
        _merge_kernel,
        grid=(t // tm,),
        in_specs=[pl.BlockSpec((tm, d), row), _const_spec(g.shape), _const_spec(wgt.shape),
                  pl.BlockSpec((tm, RW_WIDTH), row), pl.BlockSpec((tm, ML_WIDTH), row),
                  _const_spec(wur.shape), _const_spec(wum.shape), _const_spec(wo.shape)],
        out_specs=pl.BlockSpec((tm, d), row),
        out_shape=jax.ShapeDtypeStruct((t, d), F32),
        compiler_params=_params(("parallel",)),
        name="merge",
    )(x, g, wgt, yrw, yml, wur, wum, wo)


def _memkv_kernel(mem_ref, g_ref, wkv_ref, k_ref, v_ref):
    d = mem_ref.shape[1]
    hm = _rmsnorm(mem_ref[...], g_ref[...])
    kv = jnp.dot(hm, wkv_ref[...], preferred_element_type=F32)
    k_ref[...] = kv[:, 0:d].astype(BF16)
    v_ref[...] = kv[:, d:].astype(BF16)


def _memkv_call(mem2d, g, wkv):
    n, d = mem2d.shape
    row = lambda i: (i, 0)
    return pl.pallas_call(
        _memkv_kernel,
        grid=(n // MEM_LEN,),
        in_specs=[pl.BlockSpec((MEM_LEN, d), row), _const_spec(g.shape), _const_spec(wkv.shape)],
        out_specs=[pl.BlockSpec((MEM_LEN, d), row), pl.BlockSpec((MEM_LEN, d), row)],
        out_shape=[jax.ShapeDtypeStruct((n, d), BF16), jax.ShapeDtypeStruct((n, d), BF16)],
        compiler_params=_params(("parallel",)),
        name="mem_kv",
    )(mem2d, g, wkv)


XA_TM = 512


def _xattn_kernel(x_ref, g_ref, wq_ref, k_ref, v_ref, wo_ref, o_ref):
    x = x_ref[...]
    hq = _rmsnorm(x, g_ref[...])
    q = _bdot(hq, wq_ref[...]) * (XA_HEAD_DIM ** -0.5)
    lsl = [slice(h * XA_HEAD_DIM, (h + 1) * XA_HEAD_DIM) for h in range(XA_HEADS)]

    def softmax(s):
        e = jnp.exp(s - jnp.max(s, axis=-1, keepdims=True))
        return e / jnp.sum(e, axis=-1, keepdims=True)

    outs = []
    s_prev = _bdot_nt(q[:, lsl[0]], k_ref[:, lsl[0]])
    for h in range(XA_HEADS):
        s_next = _bdot_nt(q[:, lsl[h + 1]], k_ref[:, lsl[h + 1]]) if h + 1 < XA_HEADS else None
        outs.append(_bdot(softmax(s_prev), v_ref[:, lsl[h]]))
        s_prev = s_next
    o = jnp.concatenate(outs, axis=1)
    o_ref[...] = x + _bdot(o, wo_ref[...])


def _xattn_call(x, g, wq, kmem, vmem, wo, bsz, seq):
    t, d = x.shape
    tm = min(XA_TM, seq)
    nt = seq // tm
    row = lambda b, i: (b * nt + i, 0)
    mem = lambda b, i: (b, 0)
    return pl.pallas_call(
        _xattn_kernel,
        grid=(bsz, nt),
        in_specs=[pl.BlockSpec((tm, d), row), _const_spec(g.shape), _const_spec(wq.shape),
                  pl.BlockSpec((MEM_LEN, d), mem), pl.BlockSpec((MEM_LEN, d), mem),
                  _const_spec(wo.shape)],
        out_specs=pl.BlockSpec((tm, d), row),
        out_shape=jax.ShapeDtypeStruct((t, d), F32),
        compiler_params=_params(("parallel", "parallel")),
        name="mem_xattn",
    )(x, g, wq, kmem, vmem, wo)


def _row(v):
    return v.reshape(1, -1).astype(F32)


def _layer(x, mem2d, bsz, seq, p):
    d = D_MODEL
    bf = lambda w: w.astype(BF16)
    masks_np, tri_np = _rw_masks()
    masks = jnp.asarray(masks_np)
    tri = jnp.asarray(tri_np, dtype=BF16)
    gsum = jnp.asarray(_group_ones(PAIR, RW_HEAD_DIM), dtype=BF16)
    gmean = jnp.asarray(_group_ones(PAIR, RW_HEAD_DIM) / RW_HEAD_DIM, dtype=BF16)

    x = _ffn_call(x, _row(p["ffn1_norm"]), p["ffn1_w_gate"], p["ffn1_w_up"], p["ffn1_w_down"],
                  _row(p["ffn1_norm"]), final=False)

    w_in = p["w_in"]
    c0 = RW_COLS
    c1 = c0 + ML_QKVO
    c2 = c1 + 2 * ML_HEADS
    w_if = jnp.pad(w_in[:, c1:c2], ((0, 0), (0, IF_PAD - 2 * ML_HEADS)))
    convw = jnp.concatenate([p["ml_conv_w"].astype(F32), jnp.zeros((SUBLANES - ML_CONV, 2 * ML_WIDTH), F32)], axis=0)
    zrw, zml, zif = _inproj_call(x, _row(p["mix_norm"]), bf(w_in[:, :c0]), bf(w_in[:, c0:c1]), bf(w_if),
                                 _row(p["rw_mu"]), convw, _row(p["ml_conv_b"]), bsz, seq)

    zpad = jnp.zeros((DECAY_LORA, RW_WIDTH), F32)
    w2p = bf(jnp.concatenate([p["rw_w2"], zpad], axis=0))
    a2p = bf(jnp.concatenate([zpad, p["rw_a2"]], axis=0))
    vec = jnp.stack([p["rw_w0"], p["rw_a0"], p["rw_k_k"], p["rw_k_a"], p["rw_r_k"].reshape(-1),
                     jnp.zeros_like(p["rw_w0"]), jnp.zeros_like(p["rw_w0"]), jnp.zeros_like(p["rw_w0"])]).astype(F32)
    g, yi, bonus, gate, m, c = _rw1_call(zrw, w2p, a2p, bf(p["rw_g2"]), vec, gsum, tri, masks)
    gn = jnp.concatenate([jnp.stack([p["rw_gn_w"], p["rw_gn_b"]]).astype(F32),
                          jnp.zeros((SUBLANES - 2, RW_WIDTH), F32)], axis=0)
    y_rw = _rw2_call(g, yi, bonus, gate, m, c, gmean, gn, bsz, seq)

    ifb = jnp.pad(jnp.concatenate([p["ml_i_b"], p["ml_f_b"]]).astype(F32), (0, IF_PAD - 2 * ML_HEADS)).reshape(1, -1)
    tri_ml = jnp.asarray(np.tril(np.ones((ML_CHUNK, ML_CHUNK), np.float32)), dtype=BF16)
    y_ml = _ml_call(zml, zif, ifb, _row(p["ml_norm"]), tri_ml, bsz, seq)

    x = _merge_call(x, _row(p["mix_norm"]), bf(w_in[:, c2:]), y_rw, y_ml,
                    p["w_up_rw"], p["w_up_ml"], p["w_out"])

    kmem, vmem = _memkv_call(mem2d, _row(p["mem_norm"]), p["xa_wkv"])
    x = _xattn_call(x, _row(p["xa_norm"]), p["xa_wq"], kmem, vmem, p["xa_wo"], bsz, seq)
    return x


def kernel(x, mem, ffn1_norm, ffn1_w_gate, ffn1_w_up, ffn1_w_down, mix_norm, w_in, rw_mu, rw_w0, rw_w2, rw_a0,
           rw_a2, rw_g2, rw_k_k, rw_k_a, rw_r_k, rw_gn_w, rw_gn_b, ml_conv_w, ml_conv_b, ml_i_b, ml_f_b, ml_norm,
           w_up_rw, w_up_ml, w_out, xa_norm, mem_norm, xa_wq, xa_wkv, xa_wo, ffn2_norm, ffn2_w_gate, ffn2_w_up,
           ffn2_w_down, final_norm):
    bsz, seq, d = x.shape
    depth = ffn1_norm.shape[0]
    stacked = dict(ffn1_norm=ffn1_norm, ffn1_w_gate=ffn1_w_gate, ffn1_w_up=ffn1_w_up, ffn1_w_down=ffn1_w_down,
                   mix_norm=mix_norm, w_in=w_in, rw_mu=rw_mu, rw_w0=rw_w0, rw_w2=rw_w2, rw_a0=rw_a0, rw_a2=rw_a2,
                   rw_g2=rw_g2, rw_k_k=rw_k_k, rw_k_a=rw_k_a, rw_r_k=rw_r_k, rw_gn_w=rw_gn_w, rw_gn_b=rw_gn_b,
                   ml_conv_w=ml_conv_w, ml_conv_b=ml_conv_b, ml_i_b=ml_i_b, ml_f_b=ml_f_b, ml_norm=ml_norm,
                   w_up_rw=w_up_rw, w_up_ml=w_up_ml, w_out=w_out, xa_norm=xa_norm, mem_norm=mem_norm,
                   xa_wq=xa_wq, xa_wkv=xa_wkv, xa_wo=xa_wo, ffn2_norm=ffn2_norm, ffn2_w_gate=ffn2_w_gate,
                   ffn2_w_up=ffn2_w_up, ffn2_w_down=ffn2_w_down)
    h = x.reshape(bsz * seq, d)
    mem2d = mem.reshape(bsz * mem.shape[1], d)
    bf = lambda w: w.astype(BF16)
    for l in range(depth):
        p = {name: w[l] for name, w in stacked.items()}
        h = _layer(h, mem2d, bsz, seq, p)
        last = l == depth - 1
        h = _ffn_call(h, _row(p["ffn2_norm"]), p["ffn2_w_gate"], p["ffn2_w_up"], p["ffn2_w_down"],
                      _row(final_norm), final=last)
    return h.reshape(bsz, seq, d)
```

```python
import functools
import math

import jax
import jax.numpy as jnp
import numpy as np
from jax import lax
from jax.experimental import pallas as pl
from jax.experimental.pallas import tpu as pltpu

F32 = jnp.float32
BF16 = jnp.bfloat16

D_MODEL = 1024
D_FF = 2816
FFN_HALF = 0.5
NORM_EPS = 1e-5
MEM_LEN = 256

RW_HEADS = 8
RW_HEAD_DIM = 64
RW_WIDTH = 512
DECAY_LORA = 64
AAA_LORA = 64
GATE_LORA = 128
RW_COLS = 3 * RW_WIDTH + DECAY_LORA + AAA_LORA + GATE_LORA
GN_EPS = 64e-5
RW_PAIRS = RW_HEADS // 2
PAIR = 2 * RW_HEAD_DIM

ML_HEADS = 4
ML_HEAD_DIM = 128
ML_WIDTH = 512
ML_CONV = 4
ML_NORM_EPS = 1e-6
ML_QKVO = 4 * ML_WIDTH
IF_PAD = 128

XA_HEADS = 4
XA_HEAD_DIM = 256

CHUNK = 64
LANES = 128
SUBLANES = 8
NEG_BIG = -1e30

VMEM_LIMIT = 56 * 1024 * 1024


def _bdot(a, b):
    return jnp.dot(a.astype(BF16), b.astype(BF16), preferred_element_type=F32)


def _bdot_nt(a, b):
    return lax.dot_general(a.astype(BF16), b.astype(BF16), (((1,), (1,)), ((), ())),
                           preferred_element_type=F32)


def _bdot_tn(a, b):
    return lax.dot_general(a.astype(BF16), b.astype(BF16), (((0,), (0,)), ((), ())),
                           preferred_element_type=F32)


def _split3(x):
    hi = x.astype(BF16)
    r1 = x - hi.astype(F32)
    mid = r1.astype(BF16)
    lo = (r1 - mid.astype(F32)).astype(BF16)
    return hi, mid, lo


def _dot_exact_lhs(c_bf16, x):
    hi, mid, lo = _split3(x)
    f = lambda p: jnp.dot(c_bf16, p, preferred_element_type=F32)
    return f(hi) + f(mid) + f(lo)


def _dot_split2_rhs(x, c_bf16):
    hi = x.astype(BF16)
    lo = (x - hi.astype(F32)).astype(BF16)
    f = lambda p: jnp.dot(p, c_bf16, preferred_element_type=F32)
    return f(hi) + f(lo)


def _per_slab(fn, x, c_bf16):
    n = x.shape[1] // LANES
    return jnp.concatenate([fn(x[:, i * LANES:(i + 1) * LANES], c_bf16) for i in range(n)], axis=1)


def _dot_f32(a, b):
    a_hi = a.astype(BF16)
    a_lo = (a - a_hi.astype(F32)).astype(BF16)
    b_hi = b.astype(BF16)
    b_lo = (b - b_hi.astype(F32)).astype(BF16)
    f = lambda p, q: jnp.dot(p, q, preferred_element_type=F32)
    return f(a_hi, b_hi) + f(a_hi, b_lo) + f(a_lo, b_hi)


def _sigmoid(x):
    return 1.0 / (1.0 + jnp.exp(-x))


def _silu(x):
    return x * _sigmoid(x)


def _rmsnorm(x, g):
    ms = jnp.mean(x * x, axis=-1, keepdims=True)
    return x * lax.rsqrt(ms + NORM_EPS) * g


def _shift_rows(u, carry8, j):
    rolled = pltpu.roll(u, j, axis=0)
    rolled_c = pltpu.roll(carry8, j, axis=0)
    row = lax.broadcasted_iota(jnp.int32, (SUBLANES, u.shape[1]), 0)
    top = jnp.where(row < j, rolled_c, rolled[:SUBLANES])
    return jnp.concatenate([top, rolled[SUBLANES:]], axis=0)


def _shift_rows_via(buf_ref, u, carry8, j):
    rows = u.shape[0]
    buf_ref[0:SUBLANES, :] = carry8
    buf_ref[SUBLANES:, :] = u
    return buf_ref[SUBLANES - j:SUBLANES - j + rows, :]


def _const_spec(shape):
    nd = len(shape)
    return pl.BlockSpec(shape, lambda *_: (0,) * nd, pipeline_mode=pl.Buffered(1))


def _params(sem):
    return pltpu.CompilerParams(dimension_semantics=sem, vmem_limit_bytes=VMEM_LIMIT)


FF_CHUNK = 256
FFN_TM = 1024


def _ffn_kernel(x_ref, g_ref, wg_ref, wu_ref, wd_ref, fg_ref, o_ref, *, final):
    x = x_ref[...]
    h = _rmsnorm(x, g_ref[...]).astype(BF16)
    acc = jnp.zeros(x.shape, F32)
    for c in range(D_FF // FF_CHUNK):
        sl = slice(c * FF_CHUNK, (c + 1) * FF_CHUNK)
        gate = jnp.dot(h, wg_ref[:, sl], preferred_element_type=F32)
        up = jnp.dot(h, wu_ref[:, sl], preferred_element_type=F32)
        act = (_silu(gate) * up).astype(BF16)
        acc = acc + jnp.dot(act, wd_ref[sl, :], preferred_element_type=F32)
    y = x + FFN_HALF * acc
    if final:
        y = _rmsnorm(y, fg_ref[...])
    o_ref[...] = y


def _ffn_call(x, g, wg, wu, wd, fg, final):
    t, d = x.shape
    tm = min(FFN_TM, t)
    return pl.pallas_call(
        functools.partial(_ffn_kernel, final=final),
        grid=(t // tm,),
        in_specs=[pl.BlockSpec((tm, d), lambda i: (i, 0)),
                  _const_spec((1, d)), _const_spec((d, D_FF)), _const_spec((d, D_FF)),
                  _const_spec((D_FF, d)), _const_spec((1, d))],
        out_specs=pl.BlockSpec((tm, d), lambda i: (i, 0)),
        out_shape=jax.ShapeDtypeStruct((t, d), F32),
        compiler_params=_params(("parallel",)),
        name="ffn_final" if final else "ffn",
    )(x, g, wg, wu, wd, fg)


INPROJ_TM = 512
INPROJ_COLS = 256


def _inproj_kernel(x_ref, g_ref, wrw_ref, wml_ref, wif_ref, mu_ref, convw_ref, convb_ref,
                   zrw_ref, zml_ref, zif_ref, carry_rw_ref, carry_ml_ref, buf_a_ref, buf_b_ref):
    @pl.when(pl.program_id(1) == 0)
    def _():
        carry_rw_ref[...] = jnp.zeros(carry_rw_ref.shape, F32)
        carry_ml_ref[...] = jnp.zeros(carry_ml_ref.shape, F32)

    rows = x_ref.shape[0]
    h = _rmsnorm(x_ref[...], g_ref[...]).astype(BF16)

    def conv_stage(cols, scale):
        def finish(u):
            carry = carry_ml_ref[:, cols]
            w0, w1, w2, w3 = (convw_ref[j:j + 1, cols] for j in range(ML_CONV))
            u1 = _shift_rows_via(buf_a_ref, u, carry, 1)
            pair = w2 * u + w3 * u1
            pair_carry = w2 * carry + w3 * pltpu.roll(carry, 1, axis=0)
            conv = convb_ref[:, cols] + w0 * u + w1 * u1 + _shift_rows_via(buf_b_ref, pair, pair_carry, 2)
            carry_ml_ref[:, cols] = u[rows - SUBLANES:]
            zml_ref[:, cols] = _silu(conv) * scale if scale != 1.0 else _silu(conv)
        return wml_ref, cols, finish

    def plain_stage(cols, fn):
        def finish(u):
            zml_ref[:, cols] = fn(u)
        return wml_ref, cols, finish

    def lerp_stage(cols):
        def finish(z):
            prev = _shift_rows_via(buf_a_ref, z, carry_rw_ref[:, cols], 1)
            carry_rw_ref[:, cols] = z[rows - SUBLANES:]
            zrw_ref[:, cols] = z + (prev - z) * mu_ref[:, cols]
        return wrw_ref, cols, finish

    blk = lambda c: slice(c * INPROJ_COLS, (c + 1) * INPROJ_COLS)
    n_head_blocks = ML_WIDTH // INPROJ_COLS
    heavy, light = [], []
    for c in range(ML_QKVO // INPROJ_COLS):
        group = c // n_head_blocks
        if group < 2:
            heavy.append(conv_stage(blk(c), ML_HEAD_DIM ** -0.5 if group == 0 else 1.0))
        else:
            light.append(plain_stage(blk(c), _sigmoid if group == 3 else (lambda u: u)))
    light += [lerp_stage(blk(c)) for c in range(RW_COLS // INPROJ_COLS)]

    per_heavy = -(-len(light) // len(heavy))
    stages = []
    for i, stage in enumerate(heavy):
        stages.append(stage)
        stages += light[i * per_heavy:(i + 1) * per_heavy]
    pending = None
    for w_ref, cols, finish in stages:
        acc = jnp.dot(h, w_ref[:, cols], preferred_element_type=F32)
        if pending is not None:
            pending[0](pending[1])
        pending = (finish, acc)
    zif_ref[...] = jnp.dot(h, wif_ref[...], preferred_element_type=F32)
    pending[0](pending[1])


def _inproj_call(x, g, wrw, wml, wif, mu, convw, convb, bsz, seq):
    t, d = x.shape
    tm = min(INPROJ_TM, seq)
    nt = seq // tm
    row = lambda b, i: (b * nt + i, 0)
    return pl.pallas_call(
        _inproj_kernel,
        grid=(bsz, nt),
        in_specs=[pl.BlockSpec((tm, d), row),
                  _const_spec((1, d)), _const_spec((d, RW_COLS)), _const_spec((d, ML_QKVO)),
                  _const_spec((d, IF_PAD)), _const_spec((1, RW_COLS)),
                  _const_spec(convw.shape), _const_spec(convb.shape)],
        out_specs=[pl.BlockSpec((tm, RW_COLS), row), pl.BlockSpec((tm, ML_QKVO), row),
                   pl.BlockSpec((tm, IF_PAD), row)],
        out_shape=[jax.ShapeDtypeStruct((t, RW_COLS), F32), jax.ShapeDtypeStruct((t, ML_QKVO), F32),
                   jax.ShapeDtypeStruct((t, IF_PAD), F32)],
        scratch_shapes=[pltpu.VMEM((SUBLANES, RW_COLS), F32), pltpu.VMEM((SUBLANES, 2 * ML_WIDTH), F32),
                        pltpu.VMEM((SUBLANES + tm, INPROJ_COLS), F32),
                        pltpu.VMEM((SUBLANES + tm, INPROJ_COLS), F32)],
        compiler_params=_params(("parallel", "arbitrary")),
        name="inproj",
    )(x, g, wrw, wml, wif, mu, convw, convb)


RW1_CHUNKS = 8
RW1_GROUP = 16
INV_LEVELS = 6


def _rw_masks():
    t = np.arange(CHUNK)[:, None]
    s = np.tile(np.arange(CHUNK), 2)[None, :]
    levels = []
    for k in range(INV_LEVELS):
        size = 1 << k
        levels.append(((t // (2 * size)) == (s // (2 * size))) & ((t // size) % 2 == 1) & ((s // size) % 2 == 0))
    m = np.stack([s < t, s <= t, s == t] + levels).astype(np.float32)
    tri = (np.arange(CHUNK)[:, None] >= np.arange(CHUNK)[None, :]).astype(np.float32)
    return m, tri


def _group_ones(width, group):
    i = np.arange(width)
    return (i[:, None] // group == i[None, :] // group).astype(np.float32)


def _rw1_kernel(z_ref, w2_ref, a2_ref, g2_ref, vec_ref, gsum_ref, tri_ref, mask_ref,
                g_ref, yi_ref, bonus_ref, gate_ref, m_ref, c_ref):
    z = z_ref[...]
    rows = z.shape[0]
    r = z[:, 0:RW_WIDTH]
    k = z[:, RW_WIDTH:2 * RW_WIDTH]
    v = z[:, 2 * RW_WIDTH:3 * RW_WIDTH]
    lora_in = z[:, 3 * RW_WIDTH:3 * RW_WIDTH + DECAY_LORA + AAA_LORA]
    g_d = z[:, 3 * RW_WIDTH + DECAY_LORA + AAA_LORA:]
    w0 = vec_ref[0:1, :]
    a0 = vec_ref[1:2, :]
    k_k = vec_ref[2:3, :]
    k_a = vec_ref[3:4, :]
    r_k = vec_ref[4:5, :]

    w_raw = w0 + _bdot(jnp.tanh(lora_in), w2_ref[...])
    logw = -math.exp(-0.5) * _sigmoid(w_raw)
    a = _sigmoid(a0 + _bdot(lora_in, a2_ref[...]))
    gate_ref[...] = _bdot(_sigmoid(g_d), g2_ref[...])

    gsum = gsum_ref[...]
    kk = k * k_k
    kk = kk / jnp.maximum(jnp.sqrt(_per_slab(_bdot, kk * kk, gsum)), 1e-12)
    k_mod = k * (1.0 + (a - 1.0) * k_a)
    b = kk * a
    bonus_ref[...] = _per_slab(_dot_split2_rhs, r * k_mod * r_k, gsum) * v

    lane = lax.broadcasted_iota(jnp.int32, (CHUNK, PAIR), 1)
    head0 = lane < RW_HEAD_DIM
    strict = mask_ref[0]
    incl = mask_ref[1]
    eye = mask_ref[2]
    tri = tri_ref[...]

    def stack(x):
        return jnp.concatenate([jnp.where(head0, x, 0.0), jnp.where(head0, 0.0, x)], axis=0).astype(BF16)

    ops = {}
    for c in range(rows // CHUNK):
        rs = slice(c * CHUNK, (c + 1) * CHUNK)
        lw = logw[rs]
        cum = _dot_exact_lhs(tri, lw)
        cum_last = cum[CHUNK - 1:CHUNK]
        w_incl = jnp.exp(cum)
        w_inv = 1.0 / w_incl
        w_prev = jnp.exp(cum - lw)
        w_last = jnp.exp(cum_last)
        bt = b[rs] * w_inv
        kt = k_mod[rs] * w_inv
        full = dict(at=-kk[rs] * w_prev, rt=r[rs] * w_incl, bt=bt, kt=kt, be=bt * w_last, ke=kt * w_last, v2=v[rs])
        for p in range(RW_PAIRS):
            ls = slice(p * PAIR, (p + 1) * PAIR)
            item = {name: val[:, ls] for name, val in full.items()}
            item["w_last"] = w_last[:, ls]
            ops[(c, p)] = item

    keys = list(ops)
    for g0 in range(0, len(keys), RW1_GROUP):
        group = keys[g0:g0 + RW1_GROUP]
        n_ab, a_ak, a_rb, a_rk, t_inv, sv = {}, {}, {}, {}, {}, {}
        for key in group:
            o = ops[key]
            prod = _bdot_nt(jnp.concatenate([o["at"], o["rt"]], axis=0),
                            jnp.concatenate([stack(o["bt"]), stack(o["kt"])], axis=0))
            n_ab[key] = prod[:CHUNK, :PAIR] * strict
            a_ak[key] = prod[:CHUNK, PAIR:] * strict
            a_rb[key] = prod[CHUNK:, :PAIR] * incl
            a_rk[key] = prod[CHUNK:, PAIR:] * incl
            sv[key] = stack(o["v2"])
            t_inv[key] = eye + n_ab[key] * mask_ref[3]
        for lvl in range(1, INV_LEVELS):
            half = {key: _bdot(t_inv[key], stack(n_ab[key] * mask_ref[3 + lvl])) for key in group}
            t_inv = {key: t_inv[key] + _bdot(half[key], stack(t_inv[key])) for key in group}
        akv = {key: _bdot(a_ak[key], sv[key]) for key in group}
        pq = {key: _bdot(t_inv[key], jnp.concatenate([stack(ops[key]["at"]), stack(akv[key])], axis=1))
              for key in group}
        gy = {key: _bdot(a_rb[key], jnp.concatenate([stack(pq[key][:, :PAIR]), stack(pq[key][:, PAIR:])], axis=1))
              for key in group}
        ark_v = {key: _bdot(a_rk[key], sv[key]) for key in group}
        mc = {key: _bdot_tn(ops[key]["be"], pq[key]) for key in group}
        ke_v = {key: _bdot_tn(ops[key]["ke"], ops[key]["v2"]) for key in group}
        for key in group:
            c, p = key
            rs = slice(c * CHUNK, (c + 1) * CHUNK)
            ls = slice(p * PAIR, (p + 1) * PAIR)
            g_ref[rs, ls] = (ops[key]["rt"] + gy[key][:, :PAIR]).astype(BF16)
            yi_ref[rs, ls] = gy[key][:, PAIR:] + ark_v[key]
            cv = mc[key][:, PAIR:] + ke_v[key]
            m_ref[c, p] = eye * ops[key]["w_last"] + jnp.where(head0, mc[key][:CHUNK, :PAIR], mc[key][CHUNK:, :PAIR])
            c_ref[c, p] = jnp.where(head0, cv[:CHUNK], cv[CHUNK:])


def _rw1_call(zrw, w2p, a2p, g2, vec, gsum, tri, masks):
    t = zrw.shape[0]
    rows = RW1_CHUNKS * CHUNK
    nchunks = t // CHUNK
    tok = lambda i: (i, 0)
    blk = lambda i: (i, 0, 0, 0)
    tok_spec = pl.BlockSpec((rows, RW_WIDTH), tok)
    mc_spec = pl.BlockSpec((RW1_CHUNKS, RW_PAIRS, CHUNK, PAIR), blk)
    tok_shape = jax.ShapeDtypeStruct((t, RW_WIDTH), F32)
    mc_shape = jax.ShapeDtypeStruct((nchunks, RW_PAIRS, CHUNK, PAIR), F32)
    return pl.pallas_call(
        _rw1_kernel,
        grid=(t // rows,),
        in_specs=[pl.BlockSpec((rows, RW_COLS), tok),
                  _const_spec(w2p.shape), _const_spec(a2p.shape), _const_spec(g2.shape),
                  _const_spec(vec.shape), _const_spec(gsum.shape), _const_spec(tri.shape),
                  _const_spec(masks.shape)],
        out_specs=[tok_spec, tok_spec, tok_spec, tok_spec, mc_spec, mc_spec],
        out_shape=[jax.ShapeDtypeStruct((t, RW_WIDTH), BF16), tok_shape, tok_shape, tok_shape, mc_shape, mc_shape],
        compiler_params=_params(("parallel",)),
        name="rwkv_chunk_local",
    )(zrw, w2p, a2p, g2, vec, gsum, tri, masks)


RW2_CHUNKS = 4
RW2_BATCH = 4


def _rw2_kernel(g_ref, yi_ref, bonus_ref, gate_ref, m_ref, c_ref, gmean_ref, gn_ref, y_ref, h_ref):
    @pl.when(pl.program_id(1) == 0)
    def _():
        h_ref[...] = jnp.zeros(h_ref.shape, F32)

    lane = lax.broadcasted_iota(jnp.int32, (CHUNK, PAIR), 1)
    head0 = lane < RW_HEAD_DIM

    def block_diag(x):
        return jnp.concatenate([jnp.where(head0, x, 0.0), jnp.where(head0, 0.0, x)], axis=0)

    nb = g_ref.shape[0]
    chains = [(b, p) for b in range(nb) for p in range(RW_PAIRS)]
    lsl = [slice(p * PAIR, (p + 1) * PAIR) for p in range(RW_PAIRS)]
    state = {(b, p): h_ref[b, p] for b, p in chains}
    ys = {}
    for c in range(RW2_CHUNKS):
        rs = slice(c * CHUNK, (c + 1) * CHUNK)
        for b, p in chains:
            ys[b, c, p] = _bdot(g_ref[b, rs, lsl[p]], state[b, p]) + yi_ref[b, rs, lsl[p]]
        state = {(b, p): _dot_f32(block_diag(m_ref[b, c, p]), state[b, p]) + block_diag(c_ref[b, c, p])
                 for b, p in chains}
    for b, p in chains:
        h_ref[b, p] = state[b, p]
    gmean = gmean_ref[...]
    for b in range(nb):
        y = jnp.concatenate([jnp.concatenate([ys[b, c, p] for p in range(RW_PAIRS)], axis=1)
                             for c in range(RW2_CHUNKS)], axis=0)
        mu = _per_slab(_dot_split2_rhs, y, gmean)
        d = y - mu
        var = _per_slab(_bdot, d * d, gmean)
        y = d * lax.rsqrt(var + GN_EPS) * gn_ref[0:1, :] + gn_ref[1:2, :]
        y_ref[b] = ((y + bonus_ref[b]) * gate_ref[b]).astype(BF16)


def _rw2_call(g, yi, bonus, gate, m, c, gmean, gn, bsz, seq):
    t = g.shape[0]
    rows = RW2_CHUNKS * CHUNK
    nt = seq // rows
    nb = RW2_BATCH if bsz % RW2_BATCH == 0 else 1
    per_batch = lambda a: a.reshape((bsz, a.shape[0] // bsz) + a.shape[1:])
    tok_spec = pl.BlockSpec((nb, rows, RW_WIDTH), lambda b, i: (b, i, 0))
    mc_spec = pl.BlockSpec((nb, RW2_CHUNKS, RW_PAIRS, CHUNK, PAIR), lambda b, i: (b, i, 0, 0, 0))
    y = pl.pallas_call(
        _rw2_kernel,
        grid=(bsz // nb, nt),
        in_specs=[tok_spec, tok_spec, tok_spec, tok_spec, mc_spec, mc_spec,
                  _const_spec(gmean.shape), _const_spec(gn.shape)],
        out_specs=tok_spec,
        out_shape=jax.ShapeDtypeStruct((bsz, seq, RW_WIDTH), BF16),
        scratch_shapes=[pltpu.VMEM((nb, RW_PAIRS, PAIR, PAIR), F32)],
        compiler_params=_params(("parallel", "arbitrary")),
        name="rwkv_scan",
    )(per_batch(g), per_batch(yi), per_batch(bonus), per_batch(gate), per_batch(m), per_batch(c), gmean, gn)
    return y.reshape(t, RW_WIDTH)


ML_CHUNK = 128
ML_STEP_CHUNKS = 4
ML_BATCH = 1


def _log_sigmoid(x):
    return jnp.minimum(x, 0.0) - jnp.log(1.0 + jnp.exp(-jnp.abs(x)))


def _ml_kernel(z_ref, zif_ref, ifb_ref, norm_ref, tri_ref, y_ref, state_ref, m_ref):
    @pl.when(pl.program_id(1) == 0)
    def _():
        state_ref[...] = jnp.zeros(state_ref.shape, F32)
        m_ref[...] = jnp.zeros(m_ref.shape, F32)

    nb, rows = z_ref.shape[0], z_ref.shape[1]
    q_all = [z_ref[b, :, 0:ML_WIDTH] for b in range(nb)]
    k_all = [z_ref[b, :, ML_WIDTH:2 * ML_WIDTH] for b in range(nb)]
    v_all = [z_ref[b, :, 2 * ML_WIDTH:3 * ML_WIDTH] for b in range(nb)]
    ogate_all = [z_ref[b, :, 3 * ML_WIDTH:4 * ML_WIDTH] for b in range(nb)]

    pre = [zif_ref[b] + ifb_ref[...] for b in range(nb)]
    logf = [_log_sigmoid(pre[b]) for b in range(nb)]
    tri = tri_ref[...]
    lane = lax.broadcasted_iota(jnp.int32, (ML_CHUNK, LANES), 1)
    tq = lax.broadcasted_iota(jnp.int32, (ML_CHUNK, ML_CHUNK), 0)
    tk = lax.broadcasted_iota(jnp.int32, (ML_CHUNK, ML_CHUNK), 1)
    causal = tk <= tq
    ones = jnp.ones((ML_CHUNK, ML_HEAD_DIM), F32)
    dense = (ML_CHUNK, ML_CHUNK)

    chains = [(b, h) for b in range(nb) for h in range(ML_HEADS)]
    lsl = [slice(h * ML_HEAD_DIM, (h + 1) * ML_HEAD_DIM) for h in range(ML_HEADS)]
    m_run = {ch: m_ref[ch] for ch in chains}
    state_run = {ch: state_ref[ch] for ch in chains}
    ys = {b: [] for b in range(nb)}
    for c in range(rows // ML_CHUNK):
        rs = slice(c * ML_CHUNK, (c + 1) * ML_CHUNK)
        z, z_t = [], []
        for b in range(nb):
            bcum = _dot_exact_lhs(tri, logf[b][rs])
            z.append(jnp.where(lane < ML_HEADS, pre[b][rs], bcum))
            z_t.append(z[b].T)
        b_col = {(b, h): jnp.broadcast_to(z[b][:, ML_HEADS + h:ML_HEADS + h + 1], dense)
                 for b, h in chains}
        ib_col = {(b, h): jnp.broadcast_to(z[b][:, h:h + 1] - z[b][:, ML_HEADS + h:ML_HEADS + h + 1], dense)
                  for b, h in chains}
        ib_row = {(b, h): z_t[b][h:h + 1, :] - z_t[b][ML_HEADS + h:ML_HEADS + h + 1, :]
                  for b, h in chains}
        b_end = {ch: b_col[ch][ML_CHUNK - 1:ML_CHUNK] for ch in chains}
        m_prev = m_run
        state = state_run
        q = {(b, h): q_all[b][rs, lsl[h]] for b, h in chains}
        k = {(b, h): k_all[b][rs, lsl[h]] for b, h in chains}
        v = {(b, h): v_all[b][rs, lsl[h]] for b, h in chains}
        s_qk = {ch: _bdot_nt(q[ch], k[ch]) for ch in chains}
        inter = {ch: _bdot(q[ch], state[ch]) for ch in chains}

        log_d = {ch: jnp.where(causal, b_col[ch] + ib_row[ch], NEG_BIG) for ch in chains}
        log_inter = {ch: b_col[ch] + m_prev[ch] for ch in chains}
        m_row = {ch: jnp.maximum(log_inter[ch], jnp.max(log_d[ch], axis=-1, keepdims=True)) for ch in chains}
        p = {ch: jnp.exp(log_d[ch] - m_row[ch]) * s_qk[ch] for ch in chains}
        intra = {ch: _bdot(p[ch], jnp.concatenate([v[ch], ones], axis=1)) for ch in chains}

        log_w_end = {ch: ib_col[ch] + b_end[ch] for ch in chains}
        m_chunk = {ch: jnp.max(log_w_end[ch], axis=0, keepdims=True) for ch in chains}
        w_end = {ch: jnp.exp(log_w_end[ch] - m_chunk[ch]) for ch in chains}
        kv = {ch: _bdot_tn(k[ch], jnp.concatenate([w_end[ch] * v[ch], w_end[ch]], axis=1)) for ch in chains}
        m_run, state_run = {}, {}
        for ch in chains:
            m_new = jnp.maximum(b_end[ch] + m_prev[ch], m_chunk[ch])
            dec = jnp.exp(b_end[ch] + m_prev[ch] - m_new)
            sc = jnp.exp(m_chunk[ch] - m_new)
            state_run[ch] = (jnp.concatenate([dec, dec], axis=1) * state[ch]
                             + jnp.concatenate([sc, sc], axis=1) * kv[ch])
            m_run[ch] = m_new

        outs = {}
        for ch in chains:
            b, h = ch
            w_inter = jnp.exp(log_inter[ch] - m_row[ch])
            num = intra[ch][:, :ML_HEAD_DIM] + w_inter * inter[ch][:, :ML_HEAD_DIM]
            den = intra[ch][:, ML_HEAD_DIM:] + w_inter * inter[ch][:, ML_HEAD_DIM:]
            hh = num / jnp.maximum(jnp.abs(den), jnp.exp(-m_row[ch]))
            mu = jnp.mean(hh, axis=-1, keepdims=True)
            d = hh - mu
            var = jnp.mean(d * d, axis=-1, keepdims=True)
            hn = d * lax.rsqrt(var + ML_NORM_EPS)
            outs[ch] = hn * norm_ref[:, lsl[h]] * ogate_all[b][rs, lsl[h]]
        for b in range(nb):
            ys[b].append(jnp.concatenate([outs[b, h] for h in range(ML_HEADS)], axis=1).astype(BF16))
    for b in range(nb):
        y_ref[b] = jnp.concatenate(ys[b], axis=0)
    for ch in chains:
        state_ref[ch] = state_run[ch]
        m_ref[ch] = m_run[ch]


def _ml_call(zml, zif, ifb, norm, tri, bsz, seq):
    t = zml.shape[0]
    rows = ML_STEP_CHUNKS * ML_CHUNK
    nt = seq // rows
    nb = ML_BATCH if bsz % ML_BATCH == 0 else 1
    tok = lambda b, i: (b, i, 0)
    y = pl.pallas_call(
        _ml_kernel,
        grid=(bsz // nb, nt),
        in_specs=[pl.BlockSpec((nb, rows, ML_QKVO), tok), pl.BlockSpec((nb, rows, IF_PAD), tok),
                  _const_spec(ifb.shape), _const_spec(norm.shape), _const_spec(tri.shape)],
        out_specs=pl.BlockSpec((nb, rows, ML_WIDTH), tok),
        out_shape=jax.ShapeDtypeStruct((bsz, seq, ML_WIDTH), BF16),
        scratch_shapes=[pltpu.VMEM((nb, ML_HEADS, ML_HEAD_DIM, 2 * ML_HEAD_DIM), F32),
                        pltpu.VMEM((nb, ML_HEADS, 1, LANES), F32)],
        compiler_params=_params(("parallel", "arbitrary")),
        name="mlstm",
    )(zml.reshape(bsz, seq, ML_QKVO), zif.reshape(bsz, seq, IF_PAD), ifb, norm, tri)
    return y.reshape(t, ML_WIDTH)


MERGE_TM = 512


def _merge_kernel(x_ref, g_ref, wgt_ref, yrw_ref, yml_ref, wur_ref, wum_ref, wo_ref, o_ref):
    x = x_ref[...]
    d = x.shape[1]
    h = _rmsnorm(x, g_ref[...]).astype(BF16)
    gates = _sigmoid(jnp.dot(h, wgt_ref[...], preferred_element_type=F32))
    up_rw = _bdot(yrw_ref[...], wur_ref[...])
    up_ml = _bdot(yml_ref[...], wum_ref[...])
    merged = gates[:, 0:d] * up_rw + gates[:, d:] * up_ml
    o_ref[...] = x + _bdot(merged, wo_ref[...])


def _merge_call(x, g, wgt, yrw, yml, wur, wum, wo):
    t, d = x.shape
    tm = min(MERGE_TM, t)
    row = lambda i: (i, 0)
    return pl.pallas_call(
        _merge_kernel,
        grid=(t // tm,),
        in_specs=[pl.BlockSpec((tm, d), row), _const_spec(g.shape), _const_spec(wgt.shape),
                  pl.BlockSpec((tm, RW_WIDTH), row), pl.BlockSpec((tm, ML_WIDTH), row),
                  _const_spec(wur.shape), _const_spec(wum.shape), _const_spec(wo.shape)],
        out_specs=pl.BlockSpec((tm, d), row),
        out_shape=jax.ShapeDtypeStruct((t, d), F32),
        compiler_params=_params(("parallel",)),
        name="merge",
    )(x, g, wgt, yrw, yml, wur, wum, wo)


def _memkv_kernel(mem_ref, g_ref, wkv_ref, k_ref, v_ref):
    d = mem_ref.shape[1]
    hm = _rmsnorm(mem_ref[...], g_ref[...])
    kv = jnp.dot(hm, wkv_ref[...], preferred_element_type=F32)
    k_ref[...] = kv[:, 0:d].astype(BF16)
    v_ref[...] = kv[:, d:].astype(BF16)


def _memkv_call(mem2d, g, wkv):
    n, d = mem2d.shape
    row = lambda i: (i, 0)
    return pl.pallas_call(
        _memkv_kernel,
        grid=(n // MEM_LEN,),
        in_specs=[pl.BlockSpec((MEM_LEN, d), row), _const_spec(g.shape), _const_spec(wkv.shape)],
        out_specs=[pl.BlockSpec((MEM_LEN, d), row), pl.BlockSpec((MEM_LEN, d), row)],
        out_shape=[jax.ShapeDtypeStruct((n, d), BF16), jax.ShapeDtypeStruct((n, d), BF16)],
        compiler_params=_params(("parallel",)),
        name="mem_kv",
    )(mem2d, g, wkv)


XA_TM = 512


def _xattn_kernel(x_ref, g_ref, wq_ref, k_ref, v_ref, wo_ref, o_ref):
    x = x_ref[...]
    hq = _rmsnorm(x, g_ref[...])
    q = _bdot(hq, wq_ref[...]) * (XA_HEAD_DIM ** -0.5)
    lsl = [slice(h * XA_HEAD_DIM, (h + 1) * XA_HEAD_DIM) for h in range(XA_HEADS)]

    def softmax(s):
        e = jnp.exp(s - jnp.max(s, axis=-1, keepdims=True))
        return e / jnp.sum(e, axis=-1, keepdims=True)

    outs = []
    s_prev = _bdot_nt(q[:, lsl[0]], k_ref[:, lsl[0]])
    for h in range(XA_HEADS):
        s_next = _bdot_nt(q[:, lsl[h + 1]], k_ref[:, lsl[h + 1]]) if h + 1 < XA_HEADS else None
        outs.append(_bdot(softmax(s_prev), v_ref[:, lsl[h]]))
        s_prev = s_next
    o = jnp.concatenate(outs, axis=1)
    o_ref[...] = x + _bdot(o, wo_ref[...])


def _xattn_call(x, g, wq, kmem, vmem, wo, bsz, seq):
    t, d = x.shape
    tm = min(XA_TM, seq)
    nt = seq // tm
    row = lambda b, i: (b * nt + i, 0)
    mem = lambda b, i: (b, 0)
    return pl.pallas_call(
        _xattn_kernel,
        grid=(bsz, nt),
        in_specs=[pl.BlockSpec((tm, d), row), _const_spec(g.shape), _const_spec(wq.shape),
                  pl.BlockSpec((MEM_LEN, d), mem), pl.BlockSpec((MEM_LEN, d), mem),
                  _const_spec(wo.shape)],
        out_specs=pl.BlockSpec((tm, d), row),
        out_shape=jax.ShapeDtypeStruct((t, d), F32),
        compiler_params=_params(("parallel", "parallel")),
        name="mem_xattn",
    )(x, g, wq, kmem, vmem, wo)


def _row(v):
    return v.reshape(1, -1).astype(F32)


def _layer(x, mem2d, bsz, seq, p):
    d = D_MODEL
    bf = lambda w: w.astype(BF16)
    masks_np, tri_np = _rw_masks()
    masks = jnp.asarray(masks_np)
    tri = jnp.asarray(tri_np, dtype=BF16)
    gsum = jnp.asarray(_group_ones(PAIR, RW_HEAD_DIM), dtype=BF16)
    gmean = jnp.asarray(_group_ones(PAIR, RW_HEAD_DIM) / RW_HEAD_DIM, dtype=BF16)

    x = _ffn_call(x, _row(p["ffn1_norm"]), bf(p["ffn1_w_gate"]), bf(p["ffn1_w_up"]), bf(p["ffn1_w_down"]),
                  _row(p["ffn1_norm"]), final=False)

    w_in = p["w_in"]
    c0 = RW_COLS
    c1 = c0 + ML_QKVO
    c2 = c1 + 2 * ML_HEADS
    w_if = jnp.pad(w_in[:, c1:c2], ((0, 0), (0, IF_PAD - 2 * ML_HEADS)))
    convw = jnp.concatenate([p["ml_conv_w"].astype(F32), jnp.zeros((SUBLANES - ML_CONV, 2 * ML_WIDTH), F32)], axis=0)
    zrw, zml, zif = _inproj_call(x, _row(p["mix_norm"]), bf(w_in[:, :c0]), bf(w_in[:, c0:c1]), bf(w_if),
                                 _row(p["rw_mu"]), convw, _row(p["ml_conv_b"]), bsz, seq)

    zpad = jnp.zeros((DECAY_LORA, RW_WIDTH), F32)
    w2p = bf(jnp.concatenate([p["rw_w2"], zpad], axis=0))
    a2p = bf(jnp.concatenate([zpad, p["rw_a2"]], axis=0))
    vec = jnp.stack([p["rw_w0"], p["rw_a0"], p["rw_k_k"], p["rw_k_a"], p["rw_r_k"].reshape(-1),
                     jnp.zeros_like(p["rw_w0"]), jnp.zeros_like(p["rw_w0"]), jnp.zeros_like(p["rw_w0"])]).astype(F32)
    g, yi, bonus, gate, m, c = _rw1_call(zrw, w2p, a2p, bf(p["rw_g2"]), vec, gsum, tri, masks)
    gn = jnp.concatenate([jnp.stack([p["rw_gn_w"], p["rw_gn_b"]]).astype(F32),
                          jnp.zeros((SUBLANES - 2, RW_WIDTH), F32)], axis=0)
    y_rw = _rw2_call(g, yi, bonus, gate, m, c, gmean, gn, bsz, seq)

    ifb = jnp.pad(jnp.concatenate([p["ml_i_b"], p["ml_f_b"]]).astype(F32), (0, IF_PAD - 2 * ML_HEADS)).reshape(1, -1)
    tri_ml = jnp.asarray(np.tril(np.ones((ML_CHUNK, ML_CHUNK), np.float32)), dtype=BF16)
    y_ml = _ml_call(zml, zif, ifb, _row(p["ml_norm"]), tri_ml, bsz, seq)

    x = _merge_call(x, _row(p["mix_norm"]), bf(w_in[:, c2:]), y_rw, y_ml,
                    p["w_up_rw"], p["w_up_ml"], p["w_out"])

    kmem, vmem = _memkv_call(mem2d, _row(p["mem_norm"]), p["xa_wkv"])
    x = _xattn_call(x, _row(p["xa_norm"]), p["xa_wq"], kmem, vmem, p["xa_wo"], bsz, seq)
    return x


def kernel(x, mem, ffn1_norm, ffn1_w_gate, ffn1_w_up, ffn1_w_down, mix_norm, w_in, rw_mu, rw_w0, rw_w2, rw_a0,
           rw_a2, rw_g2, rw_k_k, rw_k_a, rw_r_k, rw_gn_w, rw_gn_b, ml_conv_w, ml_conv_b, ml_i_b, ml_f_b, ml_norm,
           w_up_rw, w_up_ml, w_out, xa_norm, mem_norm, xa_wq, xa_wkv, xa_wo, ffn2_norm, ffn2_w_gate, ffn2_w_up,
           ffn2_w_down, final_norm):
    bsz, seq, d = x.shape
    depth = ffn1_norm.shape[0]
    stacked = dict(ffn1_norm=ffn1_norm, ffn1_w_gate=ffn1_w_gate, ffn1_w_up=ffn1_w_up, ffn1_w_down=ffn1_w_down,
                   mix_norm=mix_norm, w_in=w_in, rw_mu=rw_mu, rw_w0=rw_w0, rw_w2=rw_w2, rw_a0=rw_a0, rw_a2=rw_a2,
                   rw_g2=rw_g2, rw_k_k=rw_k_k, rw_k_a=rw_k_a, rw_r_k=rw_r_k, rw_gn_w=rw_gn_w, rw_gn_b=rw_gn_b,
                   ml_conv_w=ml_conv_w, ml_conv_b=ml_conv_b, ml_i_b=ml_i_b, ml_f_b=ml_f_b, ml_norm=ml_norm,
                   w_up_rw=w_up_rw, w_up_ml=w_up_ml, w_out=w_out, xa_norm=xa_norm, mem_norm=mem_norm,
                   xa_wq=xa_wq, xa_wkv=xa_wkv, xa_wo=xa_wo, ffn2_norm=ffn2_norm, ffn2_w_gate=ffn2_w_gate,
                   ffn2_w_up=ffn2_w_up, ffn2_w_down=ffn2_w_down)
    h = x.reshape(bsz * seq, d)
    mem2d = mem.reshape(bsz * mem.shape[1], d)
    bf = lambda w: w.astype(BF16)
    for l in range(depth):
        p = {name: w[l] for name, w in stacked.items()}
        h = _layer(h, mem2d, bsz, seq, p)
        last = l == depth - 1
        h = _ffn_call(h, _row(p["ffn2_norm"]), bf(p["ffn2_w_gate"]), bf(p["ffn2_w_up"]), bf(p["ffn2_w_down"]),
                      _row(final_norm), final=last)
    return h.reshape(bsz, seq, d)
```

```python
import functools
import math

import jax
import jax.numpy as jnp
import numpy as np
from jax import lax
from jax.experimental import pallas as pl
from jax.experimental.pallas import tpu as pltpu

F32 = jnp.float32
BF16 = jnp.bfloat16

D_MODEL = 1024
D_FF = 2816
FFN_HALF = 0.5
NORM_EPS = 1e-5
MEM_LEN = 256

RW_HEADS = 8
RW_HEAD_DIM = 64
RW_WIDTH = 512
DECAY_LORA = 64
AAA_LORA = 64
GATE_LORA = 128
RW_COLS = 3 * RW_WIDTH + DECAY_LORA + AAA_LORA + GATE_LORA
GN_EPS = 64e-5
RW_PAIRS = RW_HEADS // 2
PAIR = 2 * RW_HEAD_DIM

ML_HEADS = 4
ML_HEAD_DIM = 128
ML_WIDTH = 512
ML_CONV = 4
ML_NORM_EPS = 1e-6
ML_QKVO = 4 * ML_WIDTH
IF_PAD = 128

XA_HEADS = 4
XA_HEAD_DIM = 256

CHUNK = 64
LANES = 128
SUBLANES = 8
NEG_BIG = -1e30

VMEM_LIMIT = 56 * 1024 * 1024


def _bdot(a, b):
    return jnp.dot(a.astype(BF16), b.astype(BF16), preferred_element_type=F32)


def _bdot_nt(a, b):
    return lax.dot_general(a.astype(BF16), b.astype(BF16), (((1,), (1,)), ((), ())),
                           preferred_element_type=F32)


def _bdot_tn(a, b):
    return lax.dot_general(a.astype(BF16), b.astype(BF16), (((0,), (0,)), ((), ())),
                           preferred_element_type=F32)


def _split3(x):
    hi = x.astype(BF16)
    r1 = x - hi.astype(F32)
    mid = r1.astype(BF16)
    lo = (r1 - mid.astype(F32)).astype(BF16)
    return hi, mid, lo


def _dot_exact_lhs(c_bf16, x):
    hi, mid, lo = _split3(x)
    f = lambda p: jnp.dot(c_bf16, p, preferred_element_type=F32)
    return f(hi) + f(mid) + f(lo)


def _dot_split2_rhs(x, c_bf16):
    hi = x.astype(BF16)
    lo = (x - hi.astype(F32)).astype(BF16)
    f = lambda p: jnp.dot(p, c_bf16, preferred_element_type=F32)
    return f(hi) + f(lo)


def _per_slab(fn, x, c_bf16):
    n = x.shape[1] // LANES
    return jnp.concatenate([fn(x[:, i * LANES:(i + 1) * LANES], c_bf16) for i in range(n)], axis=1)


def _dot_f32(a, b):
    a_hi = a.astype(BF16)
    a_lo = (a - a_hi.astype(F32)).astype(BF16)
    b_hi = b.astype(BF16)
    b_lo = (b - b_hi.astype(F32)).astype(BF16)
    f = lambda p, q: jnp.dot(p, q, preferred_element_type=F32)
    return f(a_hi, b_hi) + f(a_hi, b_lo) + f(a_lo, b_hi)


def _sigmoid(x):
    return 1.0 / (1.0 + jnp.exp(-x))


def _silu(x):
    return x * _sigmoid(x)


def _rmsnorm(x, g):
    ms = jnp.mean(x * x, axis=-1, keepdims=True)
    return x * lax.rsqrt(ms + NORM_EPS) * g


def _shift_rows(u, carry8, j):
    rolled = pltpu.roll(u, j, axis=0)
    rolled_c = pltpu.roll(carry8, j, axis=0)
    row = lax.broadcasted_iota(jnp.int32, (SUBLANES, u.shape[1]), 0)
    top = jnp.where(row < j, rolled_c, rolled[:SUBLANES])
    return jnp.concatenate([top, rolled[SUBLANES:]], axis=0)


def _shift_rows_via(buf_ref, u, carry8, j):
    rows = u.shape[0]
    buf_ref[0:SUBLANES, :] = carry8
    buf_ref[SUBLANES:, :] = u
    return buf_ref[SUBLANES - j:SUBLANES - j + rows, :]


def _const_spec(shape):
    nd = len(shape)
    return pl.BlockSpec(shape, lambda *_: (0,) * nd, pipeline_mode=pl.Buffered(1))


def _params(sem):
    return pltpu.CompilerParams(dimension_semantics=sem, vmem_limit_bytes=VMEM_LIMIT)


FF_CHUNK = 256
FFN_TM = 1024


def _ffn_kernel(x_ref, g_ref, wg_ref, wu_ref, wd_ref, fg_ref, o_ref, *, final):
    x = x_ref[...]
    h = _rmsnorm(x, g_ref[...]).astype(BF16)
    acc = jnp.zeros(x.shape, F32)
    for c in range(D_FF // FF_CHUNK):
        sl = slice(c * FF_CHUNK, (c + 1) * FF_CHUNK)
        gate = jnp.dot(h, wg_ref[:, sl], preferred_element_type=F32)
        up = jnp.dot(h, wu_ref[:, sl], preferred_element_type=F32)
        act = (_silu(gate) * up).astype(BF16)
        acc = acc + jnp.dot(act, wd_ref[sl, :], preferred_element_type=F32)
    y = x + FFN_HALF * acc
    if final:
        y = _rmsnorm(y, fg_ref[...])
    o_ref[...] = y


def _ffn_call(x, g, wg, wu, wd, fg, final):
    t, d = x.shape
    tm = min(FFN_TM, t)
    return pl.pallas_call(
        functools.partial(_ffn_kernel, final=final),
        grid=(t // tm,),
        in_specs=[pl.BlockSpec((tm, d), lambda i: (i, 0)),
                  _const_spec((1, d)), _const_spec((d, D_FF)), _const_spec((d, D_FF)),
                  _const_spec((D_FF, d)), _const_spec((1, d))],
        out_specs=pl.BlockSpec((tm, d), lambda i: (i, 0)),
        out_shape=jax.ShapeDtypeStruct((t, d), F32),
        compiler_params=_params(("parallel",)),
        name="ffn_final" if final else "ffn",
    )(x, g, wg, wu, wd, fg)


INPROJ_TM = 1024
INPROJ_COLS = 256


def _inproj_kernel(x_ref, g_ref, wrw_ref, wml_ref, wif_ref, mu_ref, convw_ref, convb_ref,
                   zrw_ref, zml_ref, zif_ref, carry_rw_ref, carry_ml_ref, buf_a_ref, buf_b_ref):
    @pl.when(pl.program_id(1) == 0)
    def _():
        carry_rw_ref[...] = jnp.zeros(carry_rw_ref.shape, F32)
        carry_ml_ref[...] = jnp.zeros(carry_ml_ref.shape, F32)

    rows = x_ref.shape[0]
    h = _rmsnorm(x_ref[...], g_ref[...]).astype(BF16)

    def conv_stage(cols, scale):
        def finish(u):
            carry = carry_ml_ref[:, cols]
            w0, w1, w2, w3 = (convw_ref[j:j + 1, cols] for j in range(ML_CONV))
            u1 = _shift_rows_via(buf_a_ref, u, carry, 1)
            pair = w2 * u + w3 * u1
            pair_carry = w2 * carry + w3 * pltpu.roll(carry, 1, axis=0)
            conv = convb_ref[:, cols] + w0 * u + w1 * u1 + _shift_rows_via(buf_b_ref, pair, pair_carry, 2)
            carry_ml_ref[:, cols] = u[rows - SUBLANES:]
            zml_ref[:, cols] = _silu(conv) * scale if scale != 1.0 else _silu(conv)
        return wml_ref, cols, finish

    def plain_stage(cols, fn):
        def finish(u):
            zml_ref[:, cols] = fn(u)
        return wml_ref, cols, finish

    def lerp_stage(cols):
        def finish(z):
            prev = _shift_rows_via(buf_a_ref, z, carry_rw_ref[:, cols], 1)
            carry_rw_ref[:, cols] = z[rows - SUBLANES:]
            zrw_ref[:, cols] = z + (prev - z) * mu_ref[:, cols]
        return wrw_ref, cols, finish

    blk = lambda c: slice(c * INPROJ_COLS, (c + 1) * INPROJ_COLS)
    n_head_blocks = ML_WIDTH // INPROJ_COLS
    heavy, light = [], []
    for c in range(ML_QKVO // INPROJ_COLS):
        group = c // n_head_blocks
        if group < 2:
            heavy.append(conv_stage(blk(c), ML_HEAD_DIM ** -0.5 if group == 0 else 1.0))
        else:
            light.append(plain_stage(blk(c), _sigmoid if group == 3 else (lambda u: u)))
    light += [lerp_stage(blk(c)) for c in range(RW_COLS // INPROJ_COLS)]

    per_heavy = -(-len(light) // len(heavy))
    stages = []
    for i, stage in enumerate(heavy):
        stages.append(stage)
        stages += light[i * per_heavy:(i + 1) * per_heavy]
    pending = None
    for w_ref, cols, finish in stages:
        acc = jnp.dot(h, w_ref[:, cols], preferred_element_type=F32)
        if pending is not None:
            pending[0](pending[1])
        pending = (finish, acc)
    zif_ref[...] = jnp.dot(h, wif_ref[...], preferred_element_type=F32)
    pending[0](pending[1])


def _inproj_call(x, g, wrw, wml, wif, mu, convw, convb, bsz, seq):
    t, d = x.shape
    tm = min(INPROJ_TM, seq)
    nt = seq // tm
    row = lambda b, i: (b * nt + i, 0)
    return pl.pallas_call(
        _inproj_kernel,
        grid=(bsz, nt),
        in_specs=[pl.BlockSpec((tm, d), row),
                  _const_spec((1, d)), _const_spec((d, RW_COLS)), _const_spec((d, ML_QKVO)),
                  _const_spec((d, IF_PAD)), _const_spec((1, RW_COLS)),
                  _const_spec(convw.shape), _const_spec(convb.shape)],
        out_specs=[pl.BlockSpec((tm, RW_COLS), row), pl.BlockSpec((tm, ML_QKVO), row),
                   pl.BlockSpec((tm, IF_PAD), row)],
        out_shape=[jax.ShapeDtypeStruct((t, RW_COLS), F32), jax.ShapeDtypeStruct((t, ML_QKVO), F32),
                   jax.ShapeDtypeStruct((t, IF_PAD), F32)],
        scratch_shapes=[pltpu.VMEM((SUBLANES, RW_COLS), F32), pltpu.VMEM((SUBLANES, 2 * ML_WIDTH), F32),
                        pltpu.VMEM((SUBLANES + tm, INPROJ_COLS), F32),
                        pltpu.VMEM((SUBLANES + tm, INPROJ_COLS), F32)],
        compiler_params=_params(("parallel", "arbitrary")),
        name="inproj",
    )(x, g, wrw, wml, wif, mu, convw, convb)


RW1_CHUNKS = 8
RW1_GROUP = 16
INV_LEVELS = 6


def _rw_masks():
    t = np.arange(CHUNK)[:, None]
    s = np.tile(np.arange(CHUNK), 2)[None, :]
    levels = []
    for k in range(INV_LEVELS):
        size = 1 << k
        levels.append(((t // (2 * size)) == (s // (2 * size))) & ((t // size) % 2 == 1) & ((s // size) % 2 == 0))
    m = np.stack([s < t, s <= t, s == t] + levels).astype(np.float32)
    tri = (np.arange(CHUNK)[:, None] >= np.arange(CHUNK)[None, :]).astype(np.float32)
    return m, tri


def _group_ones(width, group):
    i = np.arange(width)
    return (i[:, None] // group == i[None, :] // group).astype(np.float32)


def _rw1_kernel(z_ref, w2_ref, a2_ref, g2_ref, vec_ref, gsum_ref, tri_ref, mask_ref,
                g_ref, yi_ref, bonus_ref, gate_ref, m_ref, c_ref):
    z = z_ref[...]
    rows = z.shape[0]
    r = z[:, 0:RW_WIDTH]
    k = z[:, RW_WIDTH:2 * RW_WIDTH]
    v = z[:, 2 * RW_WIDTH:3 * RW_WIDTH]
    lora_in = z[:, 3 * RW_WIDTH:3 * RW_WIDTH + DECAY_LORA + AAA_LORA]
    g_d = z[:, 3 * RW_WIDTH + DECAY_LORA + AAA_LORA:]
    w0 = vec_ref[0:1, :]
    a0 = vec_ref[1:2, :]
    k_k = vec_ref[2:3, :]
    k_a = vec_ref[3:4, :]
    r_k = vec_ref[4:5, :]

    w_raw = w0 + _bdot(jnp.tanh(lora_in), w2_ref[...])
    logw = -math.exp(-0.5) * _sigmoid(w_raw)
    a = _sigmoid(a0 + _bdot(lora_in, a2_ref[...]))
    gate_ref[...] = _bdot(_sigmoid(g_d), g2_ref[...])

    gsum = gsum_ref[...]
    kk = k * k_k
    kk = kk / jnp.maximum(jnp.sqrt(_per_slab(_bdot, kk * kk, gsum)), 1e-12)
    k_mod = k * (1.0 + (a - 1.0) * k_a)
    b = kk * a
    bonus_ref[...] = _per_slab(_dot_split2_rhs, r * k_mod * r_k, gsum) * v

    lane = lax.broadcasted_iota(jnp.int32, (CHUNK, PAIR), 1)
    head0 = lane < RW_HEAD_DIM
    strict = mask_ref[0]
    incl = mask_ref[1]
    eye = mask_ref[2]
    tri = tri_ref[...]

    def stack(x):
        return jnp.concatenate([jnp.where(head0, x, 0.0), jnp.where(head0, 0.0, x)], axis=0).astype(BF16)

    ops = {}
    for c in range(rows // CHUNK):
        rs = slice(c * CHUNK, (c + 1) * CHUNK)
        lw = logw[rs]
        cum = _dot_exact_lhs(tri, lw)
        cum_last = cum[CHUNK - 1:CHUNK]
        w_incl = jnp.exp(cum)
        w_inv = 1.0 / w_incl
        w_prev = jnp.exp(cum - lw)
        w_last = jnp.exp(cum_last)
        bt = b[rs] * w_inv
        kt = k_mod[rs] * w_inv
        full = dict(at=-kk[rs] * w_prev, rt=r[rs] * w_incl, bt=bt, kt=kt, be=bt * w_last, ke=kt * w_last, v2=v[rs])
        for p in range(RW_PAIRS):
            ls = slice(p * PAIR, (p + 1) * PAIR)
            item = {name: val[:, ls] for name, val in full.items()}
            item["w_last"] = w_last[:, ls]
            ops[(c, p)] = item

    keys = list(ops)
    for g0 in range(0, len(keys), RW1_GROUP):
        group = keys[g0:g0 + RW1_GROUP]
        n_ab, a_ak, a_rb, a_rk, t_inv, sv = {}, {}, {}, {}, {}, {}
        for key in group:
            o = ops[key]
            prod = _bdot_nt(jnp.concatenate([o["at"], o["rt"]], axis=0),
                            jnp.concatenate([stack(o["bt"]), stack(o["kt"])], axis=0))
            n_ab[key] = prod[:CHUNK, :PAIR] * strict
            a_ak[key] = prod[:CHUNK, PAIR:] * strict
            a_rb[key] = prod[CHUNK:, :PAIR] * incl
            a_rk[key] = prod[CHUNK:, PAIR:] * incl
            sv[key] = stack(o["v2"])
            t_inv[key] = eye + n_ab[key] * mask_ref[3]
        for lvl in range(1, INV_LEVELS):
            half = {key: _bdot(t_inv[key], stack(n_ab[key] * mask_ref[3 + lvl])) for key in group}
            t_inv = {key: t_inv[key] + _bdot(half[key], stack(t_inv[key])) for key in group}
        akv = {key: _bdot(a_ak[key], sv[key]) for key in group}
        pq = {key: _bdot(t_inv[key], jnp.concatenate([stack(ops[key]["at"]), stack(akv[key])], axis=1))
              for key in group}
        gy = {key: _bdot(a_rb[key], jnp.concatenate([stack(pq[key][:, :PAIR]), stack(pq[key][:, PAIR:])], axis=1))
              for key in group}
        ark_v = {key: _bdot(a_rk[key], sv[key]) for key in group}
        mc = {key: _bdot_tn(ops[key]["be"], pq[key]) for key in group}
        ke_v = {key: _bdot_tn(ops[key]["ke"], ops[key]["v2"]) for key in group}
        for key in group:
            c, p = key
            rs = slice(c * CHUNK, (c + 1) * CHUNK)
            ls = slice(p * PAIR, (p + 1) * PAIR)
            g_ref[rs, ls] = (ops[key]["rt"] + gy[key][:, :PAIR]).astype(BF16)
            yi_ref[rs, ls] = gy[key][:, PAIR:] + ark_v[key]
            cv = mc[key][:, PAIR:] + ke_v[key]
            m_ref[c, p] = eye * ops[key]["w_last"] + jnp.where(head0, mc[key][:CHUNK, :PAIR], mc[key][CHUNK:, :PAIR])
            c_ref[c, p] = jnp.where(head0, cv[:CHUNK], cv[CHUNK:])


def _rw1_call(zrw, w2p, a2p, g2, vec, gsum, tri, masks):
    t = zrw.shape[0]
    rows = RW1_CHUNKS * CHUNK
    nchunks = t // CHUNK
    tok = lambda i: (i, 0)
    blk = lambda i: (i, 0, 0, 0)
    tok_spec = pl.BlockSpec((rows, RW_WIDTH), tok)
    mc_spec = pl.BlockSpec((RW1_CHUNKS, RW_PAIRS, CHUNK, PAIR), blk)
    tok_shape = jax.ShapeDtypeStruct((t, RW_WIDTH), F32)
    mc_shape = jax.ShapeDtypeStruct((nchunks, RW_PAIRS, CHUNK, PAIR), F32)
    return pl.pallas_call(
        _rw1_kernel,
        grid=(t // rows,),
        in_specs=[pl.BlockSpec((rows, RW_COLS), tok),
                  _const_spec(w2p.shape), _const_spec(a2p.shape), _const_spec(g2.shape),
                  _const_spec(vec.shape), _const_spec(gsum.shape), _const_spec(tri.shape),
                  _const_spec(masks.shape)],
        out_specs=[tok_spec, tok_spec, tok_spec, tok_spec, mc_spec, mc_spec],
        out_shape=[jax.ShapeDtypeStruct((t, RW_WIDTH), BF16), tok_shape, tok_shape, tok_shape, mc_shape, mc_shape],
        compiler_params=_params(("parallel",)),
        name="rwkv_chunk_local",
    )(zrw, w2p, a2p, g2, vec, gsum, tri, masks)


RW2_CHUNKS = 4
RW2_BATCH = 4


def _rw2_kernel(g_ref, yi_ref, bonus_ref, gate_ref, m_ref, c_ref, gmean_ref, gn_ref, y_ref, h_ref):
    @pl.when(pl.program_id(1) == 0)
    def _():
        h_ref[...] = jnp.zeros(h_ref.shape, F32)

    lane = lax.broadcasted_iota(jnp.int32, (CHUNK, PAIR), 1)
    head0 = lane < RW_HEAD_DIM

    def block_diag(x):
        return jnp.concatenate([jnp.where(head0, x, 0.0), jnp.where(head0, 0.0, x)], axis=0)

    nb = g_ref.shape[0]
    chains = [(b, p) for b in range(nb) for p in range(RW_PAIRS)]
    lsl = [slice(p * PAIR, (p + 1) * PAIR) for p in range(RW_PAIRS)]
    state = {(b, p): h_ref[b, p] for b, p in chains}
    ys = {}
    for c in range(RW2_CHUNKS):
        rs = slice(c * CHUNK, (c + 1) * CHUNK)
        for b, p in chains:
            ys[b, c, p] = _bdot(g_ref[b, rs, lsl[p]], state[b, p]) + yi_ref[b, rs, lsl[p]]
        state = {(b, p): _dot_f32(block_diag(m_ref[b, c, p]), state[b, p]) + block_diag(c_ref[b, c, p])
                 for b, p in chains}
    for b, p in chains:
        h_ref[b, p] = state[b, p]
    gmean = gmean_ref[...]
    for b in range(nb):
        y = jnp.concatenate([jnp.concatenate([ys[b, c, p] for p in range(RW_PAIRS)], axis=1)
                             for c in range(RW2_CHUNKS)], axis=0)
        mu = _per_slab(_dot_split2_rhs, y, gmean)
        d = y - mu
        var = _per_slab(_bdot, d * d, gmean)
        y = d * lax.rsqrt(var + GN_EPS) * gn_ref[0:1, :] + gn_ref[1:2, :]
        y_ref[b] = ((y + bonus_ref[b]) * gate_ref[b]).astype(BF16)


def _rw2_call(g, yi, bonus, gate, m, c, gmean, gn, bsz, seq):
    t = g.shape[0]
    rows = RW2_CHUNKS * CHUNK
    nt = seq // rows
    nb = RW2_BATCH if bsz % RW2_BATCH == 0 else 1
    per_batch = lambda a: a.reshape((bsz, a.shape[0] // bsz) + a.shape[1:])
    tok_spec = pl.BlockSpec((nb, rows, RW_WIDTH), lambda b, i: (b, i, 0))
    mc_spec = pl.BlockSpec((nb, RW2_CHUNKS, RW_PAIRS, CHUNK, PAIR), lambda b, i: (b, i, 0, 0, 0))
    y = pl.pallas_call(
        _rw2_kernel,
        grid=(bsz // nb, nt),
        in_specs=[tok_spec, tok_spec, tok_spec, tok_spec, mc_spec, mc_spec,
                  _const_spec(gmean.shape), _const_spec(gn.shape)],
        out_specs=tok_spec,
        out_shape=jax.ShapeDtypeStruct((bsz, seq, RW_WIDTH), BF16),
        scratch_shapes=[pltpu.VMEM((nb, RW_PAIRS, PAIR, PAIR), F32)],
        compiler_params=_params(("parallel", "arbitrary")),
        name="rwkv_scan",
    )(per_batch(g), per_batch(yi), per_batch(bonus), per_batch(gate), per_batch(m), per_batch(c), gmean, gn)
    return y.reshape(t, RW_WIDTH)


ML_CHUNK = 128
ML_STEP_CHUNKS = 4
ML_BATCH = 1


def _log_sigmoid(x):
    return jnp.minimum(x, 0.0) - jnp.log(1.0 + jnp.exp(-jnp.abs(x)))


def _ml_kernel(z_ref, zif_ref, ifb_ref, norm_ref, tri_ref, y_ref, state_ref, m_ref):
    @pl.when(pl.program_id(1) == 0)
    def _():
        state_ref[...] = jnp.zeros(state_ref.shape, F32)
        m_ref[...] = jnp.zeros(m_ref.shape, F32)

    nb, rows = z_ref.shape[0], z_ref.shape[1]
    q_all = [z_ref[b, :, 0:ML_WIDTH] for b in range(nb)]
    k_all = [z_ref[b, :, ML_WIDTH:2 * ML_WIDTH] for b in range(nb)]
    v_all = [z_ref[b, :, 2 * ML_WIDTH:3 * ML_WIDTH] for b in range(nb)]
    ogate_all = [z_ref[b, :, 3 * ML_WIDTH:4 * ML_WIDTH] for b in range(nb)]

    pre = [zif_ref[b] + ifb_ref[...] for b in range(nb)]
    logf = [_log_sigmoid(pre[b]) for b in range(nb)]
    tri = tri_ref[...]
    lane = lax.broadcasted_iota(jnp.int32, (ML_CHUNK, LANES), 1)
    tq = lax.broadcasted_iota(jnp.int32, (ML_CHUNK, ML_CHUNK), 0)
    tk = lax.broadcasted_iota(jnp.int32, (ML_CHUNK, ML_CHUNK), 1)
    causal = tk <= tq
    ones = jnp.ones((ML_CHUNK, ML_HEAD_DIM), F32)
    dense = (ML_CHUNK, ML_CHUNK)

    chains = [(b, h) for b in range(nb) for h in range(ML_HEADS)]
    lsl = [slice(h * ML_HEAD_DIM, (h + 1) * ML_HEAD_DIM) for h in range(ML_HEADS)]
    m_run = {ch: m_ref[ch] for ch in chains}
    state_run = {ch: state_ref[ch] for ch in chains}
    ys = {b: [] for b in range(nb)}
    for c in range(rows // ML_CHUNK):
        rs = slice(c * ML_CHUNK, (c + 1) * ML_CHUNK)
        z, z_t = [], []
        for b in range(nb):
            bcum = _dot_exact_lhs(tri, logf[b][rs])
            z.append(jnp.where(lane < ML_HEADS, pre[b][rs], bcum))
            z_t.append(z[b].T)
        b_col = {(b, h): jnp.broadcast_to(z[b][:, ML_HEADS + h:ML_HEADS + h + 1], dense)
                 for b, h in chains}
        ib_col = {(b, h): jnp.broadcast_to(z[b][:, h:h + 1] - z[b][:, ML_HEADS + h:ML_HEADS + h + 1], dense)
                  for b, h in chains}
        ib_row = {(b, h): z_t[b][h:h + 1, :] - z_t[b][ML_HEADS + h:ML_HEADS + h + 1, :]
                  for b, h in chains}
        b_end = {ch: b_col[ch][ML_CHUNK - 1:ML_CHUNK] for ch in chains}
        m_prev = m_run
        state = state_run
        q = {(b, h): q_all[b][rs, lsl[h]] for b, h in chains}
        k = {(b, h): k_all[b][rs, lsl[h]] for b, h in chains}
        v = {(b, h): v_all[b][rs, lsl[h]] for b, h in chains}
        s_qk = {ch: _bdot_nt(q[ch], k[ch]) for ch in chains}
        inter = {ch: _bdot(q[ch], state[ch]) for ch in chains}

        log_d = {ch: jnp.where(causal, b_col[ch] + ib_row[ch], NEG_BIG) for ch in chains}
        log_inter = {ch: b_col[ch] + m_prev[ch] for ch in chains}
        m_row = {ch: jnp.maximum(log_inter[ch], jnp.max(log_d[ch], axis=-1, keepdims=True)) for ch in chains}
        p = {ch: jnp.exp(log_d[ch] - m_row[ch]) * s_qk[ch] for ch in chains}
        intra = {ch: _bdot(p[ch], jnp.concatenate([v[ch], ones], axis=1)) for ch in chains}

        log_w_end = {ch: ib_col[ch] + b_end[ch] for ch in chains}
        m_chunk = {ch: jnp.max(log_w_end[ch], axis=0, keepdims=True) for ch in chains}
        w_end = {ch: jnp.exp(log_w_end[ch] - m_chunk[ch]) for ch in chains}
        kv = {ch: _bdot_tn(k[ch], jnp.concatenate([w_end[ch] * v[ch], w_end[ch]], axis=1)) for ch in chains}
        m_run, state_run = {}, {}
        for ch in chains:
            m_new = jnp.maximum(b_end[ch] + m_prev[ch], m_chunk[ch])
            dec = jnp.exp(b_end[ch] + m_prev[ch] - m_new)
            sc = jnp.exp(m_chunk[ch] - m_new)
            state_run[ch] = (jnp.concatenate([dec, dec], axis=1) * state[ch]
                             + jnp.concatenate([sc, sc], axis=1) * kv[ch])
            m_run[ch] = m_new

        outs = {}
        for ch in chains:
            b, h = ch
            w_inter = jnp.exp(log_inter[ch] - m_row[ch])
            num = intra[ch][:, :ML_HEAD_DIM] + w_inter * inter[ch][:, :ML_HEAD_DIM]
            den = intra[ch][:, ML_HEAD_DIM:] + w_inter * inter[ch][:, ML_HEAD_DIM:]
            hh = num / jnp.maximum(jnp.abs(den), jnp.exp(-m_row[ch]))
            mu = jnp.mean(hh, axis=-1, keepdims=True)
            d = hh - mu
            var = jnp.mean(d * d, axis=-1, keepdims=True)
            hn = d * lax.rsqrt(var + ML_NORM_EPS)
            outs[ch] = hn * norm_ref[:, lsl[h]] * ogate_all[b][rs, lsl[h]]
        for b in range(nb):
            ys[b].append(jnp.concatenate([outs[b, h] for h in range(ML_HEADS)], axis=1).astype(BF16))
    for b in range(nb):
        y_ref[b] = jnp.concatenate(ys[b], axis=0)
    for ch in chains:
        state_ref[ch] = state_run[ch]
        m_ref[ch] = m_run[ch]


def _ml_call(zml, zif, ifb, norm, tri, bsz, seq):
    t = zml.shape[0]
    rows = ML_STEP_CHUNKS * ML_CHUNK
    nt = seq // rows
    nb = ML_BATCH if bsz % ML_BATCH == 0 else 1
    tok = lambda b, i: (b, i, 0)
    y = pl.pallas_call(
        _ml_kernel,
        grid=(bsz // nb, nt),
        in_specs=[pl.BlockSpec((nb, rows, ML_QKVO), tok), pl.BlockSpec((nb, rows, IF_PAD), tok),
                  _const_spec(ifb.shape), _const_spec(norm.shape), _const_spec(tri.shape)],
        out_specs=pl.BlockSpec((nb, rows, ML_WIDTH), tok),
        out_shape=jax.ShapeDtypeStruct((bsz, seq, ML_WIDTH), BF16),
        scratch_shapes=[pltpu.VMEM((nb, ML_HEADS, ML_HEAD_DIM, 2 * ML_HEAD_DIM), F32),
                        pltpu.VMEM((nb, ML_HEADS, 1, LANES), F32)],
        compiler_params=_params(("parallel", "arbitrary")),
        name="mlstm",
    )(zml.reshape(bsz, seq, ML_QKVO), zif.reshape(bsz, seq, IF_PAD), ifb, norm, tri)
    return y.reshape(t, ML_WIDTH)


MERGE_TM = 1024


def _merge_kernel(x_ref, g_ref, wgt_ref, yrw_ref, yml_ref, wur_ref, wum_ref, wo_ref, o_ref):
    x = x_ref[...]
    d = x.shape[1]
    h = _rmsnorm(x, g_ref[...]).astype(BF16)
    gates = _sigmoid(jnp.dot(h, wgt_ref[...], preferred_element_type=F32))
    up_rw = _bdot(yrw_ref[...], wur_ref[...])
    up_ml = _bdot(yml_ref[...], wum_ref[...])
    merged = gates[:, 0:d] * up_rw + gates[:, d:] * up_ml
    o_ref[...] = x + _bdot(merged, wo_ref[...])


def _merge_call(x, g, wgt, yrw, yml, wur, wum, wo):
    t, d = x.shape
    tm = min(MERGE_TM, t)
    row = lambda i: (i, 0)
    return pl.pallas_call(
        _merge_kernel,
        grid=(t // tm,),
        in_specs=[pl.BlockSpec((tm, d), row), _const_spec(g.shape), _const_spec(wgt.shape),
                  pl.BlockSpec((tm, RW_WIDTH), row), pl.BlockSpec((tm, ML_WIDTH), row),
                  _const_spec(wur.shape), _const_spec(wum.shape), _const_spec(wo.shape)],
        out_specs=pl.BlockSpec((tm, d), row),
        out_shape=jax.ShapeDtypeStruct((t, d), F32),
        compiler_params=_params(("parallel",)),
        name="merge",
    )(x, g, wgt, yrw, yml, wur, wum, wo)


def _memkv_kernel(mem_ref, g_ref, wkv_ref, k_ref, v_ref):
    d = mem_ref.shape[1]
    hm = _rmsnorm(mem_ref[...], g_ref[...])
    kv = jnp.dot(hm, wkv_ref[...], preferred_element_type=F32)
    k_ref[...] = kv[:, 0:d].astype(BF16)
    v_ref[...] = kv[:, d:].astype(BF16)


def _memkv_call(mem2d, g, wkv):
    n, d = mem2d.shape
    row = lambda i: (i, 0)
    return pl.pallas_call(
        _memkv_kernel,
        grid=(n // MEM_LEN,),
        in_specs=[pl.BlockSpec((MEM_LEN, d), row), _const_spec(g.shape), _const_spec(wkv.shape)],
        out_specs=[pl.BlockSpec((MEM_LEN, d), row), pl.BlockSpec((MEM_LEN, d), row)],
        out_shape=[jax.ShapeDtypeStruct((n, d), BF16), jax.ShapeDtypeStruct((n, d), BF16)],
        compiler_params=_params(("parallel",)),
        name="mem_kv",
    )(mem2d, g, wkv)


XA_TM = 1024


def _xattn_kernel(x_ref, g_ref, wq_ref, k_ref, v_ref, wo_ref, o_ref):
    x = x_ref[...]
    hq = _rmsnorm(x, g_ref[...])
    q = _bdot(hq, wq_ref[...]) * (XA_HEAD_DIM ** -0.5)
    lsl = [slice(h * XA_HEAD_DIM, (h + 1) * XA_HEAD_DIM) for h in range(XA_HEADS)]

    def softmax(s):
        e = jnp.exp(s - jnp.max(s, axis=-1, keepdims=True))
        return e / jnp.sum(e, axis=-1, keepdims=True)

    outs = []
    s_prev = _bdot_nt(q[:, lsl[0]], k_ref[:, lsl[0]])
    for h in range(XA_HEADS):
        s_next = _bdot_nt(q[:, lsl[h + 1]], k_ref[:, lsl[h + 1]]) if h + 1 < XA_HEADS else None
        outs.append(_bdot(softmax(s_prev), v_ref[:, lsl[h]]))
        s_prev = s_next
    o = jnp.concatenate(outs, axis=1)
    o_ref[...] = x + _bdot(o, wo_ref[...])


def _xattn_call(x, g, wq, kmem, vmem, wo, bsz, seq):
    t, d = x.shape
    tm = min(XA_TM, seq)
    nt = seq // tm
    row = lambda b, i: (b * nt + i, 0)
    mem = lambda b, i: (b, 0)
    return pl.pallas_call(
        _xattn_kernel,
        grid=(bsz, nt),
        in_specs=[pl.BlockSpec((tm, d), row), _const_spec(g.shape), _const_spec(wq.shape),
                  pl.BlockSpec((MEM_LEN, d), mem), pl.BlockSpec((MEM_LEN, d), mem),
                  _const_spec(wo.shape)],
        out_specs=pl.BlockSpec((tm, d), row),
        out_shape=jax.ShapeDtypeStruct((t, d), F32),
        compiler_params=_params(("parallel", "parallel")),
        name="mem_xattn",
    )(x, g, wq, kmem, vmem, wo)


def _row(v):
    return v.reshape(1, -1).astype(F32)


def _layer(x, mem2d, bsz, seq, p):
    d = D_MODEL
    bf = lambda w: w.astype(BF16)
    masks_np, tri_np = _rw_masks()
    masks = jnp.asarray(masks_np)
    tri = jnp.asarray(tri_np, dtype=BF16)
    gsum = jnp.asarray(_group_ones(PAIR, RW_HEAD_DIM), dtype=BF16)
    gmean = jnp.asarray(_group_ones(PAIR, RW_HEAD_DIM) / RW_HEAD_DIM, dtype=BF16)

    x = _ffn_call(x, _row(p["ffn1_norm"]), bf(p["ffn1_w_gate"]), bf(p["ffn1_w_up"]), bf(p["ffn1_w_down"]),
                  _row(p["ffn1_norm"]), final=False)

    w_in = p["w_in"]
    c0 = RW_COLS
    c1 = c0 + ML_QKVO
    c2 = c1 + 2 * ML_HEADS
    w_if = jnp.pad(w_in[:, c1:c2], ((0, 0), (0, IF_PAD - 2 * ML_HEADS)))
    convw = jnp.concatenate([p["ml_conv_w"].astype(F32), jnp.zeros((SUBLANES - ML_CONV, 2 * ML_WIDTH), F32)], axis=0)
    zrw, zml, zif = _inproj_call(x, _row(p["mix_norm"]), bf(w_in[:, :c0]), bf(w_in[:, c0:c1]), bf(w_if),
                                 _row(p["rw_mu"]), convw, _row(p["ml_conv_b"]), bsz, seq)

    zpad = jnp.zeros((DECAY_LORA, RW_WIDTH), F32)
    w2p = bf(jnp.concatenate([p["rw_w2"], zpad], axis=0))
    a2p = bf(jnp.concatenate([zpad, p["rw_a2"]], axis=0))
    vec = jnp.stack([p["rw_w0"], p["rw_a0"], p["rw_k_k"], p["rw_k_a"], p["rw_r_k"].reshape(-1),
                     jnp.zeros_like(p["rw_w0"]), jnp.zeros_like(p["rw_w0"]), jnp.zeros_like(p["rw_w0"])]).astype(F32)
    g, yi, bonus, gate, m, c = _rw1_call(zrw, w2p, a2p, bf(p["rw_g2"]), vec, gsum, tri, masks)
    gn = jnp.concatenate([jnp.stack([p["rw_gn_w"], p["rw_gn_b"]]).astype(F32),
                          jnp.zeros((SUBLANES - 2, RW_WIDTH), F32)], axis=0)
    y_rw = _rw2_call(g, yi, bonus, gate, m, c, gmean, gn, bsz, seq)

    ifb = jnp.pad(jnp.concatenate([p["ml_i_b"], p["ml_f_b"]]).astype(F32), (0, IF_PAD - 2 * ML_HEADS)).reshape(1, -1)
    tri_ml = jnp.asarray(np.tril(np.ones((ML_CHUNK, ML_CHUNK), np.float32)), dtype=BF16)
    y_ml = _ml_call(zml, zif, ifb, _row(p["ml_norm"]), tri_ml, bsz, seq)

    x = _merge_call(x, _row(p["mix_norm"]), bf(w_in[:, c2:]), y_rw, y_ml,
                    p["w_up_rw"], p["w_up_ml"], p["w_out"])

    kmem, vmem = _memkv_call(mem2d, _row(p["mem_norm"]), p["xa_wkv"])
    x = _xattn_call(x, _row(p["xa_norm"]), p["xa_wq"], kmem, vmem, p["xa_wo"], bsz, seq)
    return x


def kernel(x, mem, ffn1_norm, ffn1_w_gate, ffn1_w_up, ffn1_w_down, mix_norm, w_in, rw_mu, rw_w0, rw_w2, rw_a0,
           rw_a2, rw_g2, rw_k_k, rw_k_a, rw_r_k, rw_gn_w, rw_gn_b, ml_conv_w, ml_conv_b, ml_i_b, ml_f_b, ml_norm,
           w_up_rw, w_up_ml, w_out, xa_norm, mem_norm, xa_wq, xa_wkv, xa_wo, ffn2_norm, ffn2_w_gate, ffn2_w_up,
           ffn2_w_down, final_norm):
    bsz, seq, d = x.shape
    depth = ffn1_norm.shape[0]
    stacked = dict(ffn1_norm=ffn1_norm, ffn1_w_gate=ffn1_w_gate, ffn1_w_up=ffn1_w_up, ffn1_w_down=ffn1_w_down,
                   mix_norm=mix_norm, w_in=w_in, rw_mu=rw_mu, rw_w0=rw_w0, rw_w2=rw_w2, rw_a0=rw_a0, rw_a2=rw_a2,
                   rw_g2=rw_g2, rw_k_k=rw_k_k, rw_k_a=rw_k_a, rw_r_k=rw_r_k, rw_gn_w=rw_gn_w, rw_gn_b=rw_gn_b,
                   ml_conv_w=ml_conv_w, ml_conv_b=ml_conv_b, ml_i_b=ml_i_b, ml_f_b=ml_f_b, ml_norm=ml_norm,
                   w_up_rw=w_up_rw, w_up_ml=w_up_ml, w_out=w_out, xa_norm=xa_norm, mem_norm=mem_norm,
                   xa_wq=xa_wq, xa_wkv=xa_wkv, xa_wo=xa_wo, ffn2_norm=ffn2_norm, ffn2_w_gate=ffn2_w_gate,
                   ffn2_w_up=ffn2_w_up, ffn2_w_down=ffn2_w_down)
    h = x.reshape(bsz * seq, d)
    mem2d = mem.reshape(bsz * mem.shape[1], d)
    bf = lambda w: w.astype(BF16)
    for l in range(depth):
        p = {name: w[l] for name, w in stacked.items()}
        h = _layer(h, mem2d, bsz, seq, p)
        last = l == depth - 1
        h = _ffn_call(h, _row(p["ffn2_norm"]), bf(p["ffn2_w_gate"]), bf(p["ffn2_w_up"]), bf(p["ffn2_w_down"]),
                      _row(final_norm), final=last)
    return h.reshape(bsz, seq, d)
```

```python
import functools
import math

import jax
import jax.numpy as jnp
import numpy as np
from jax import lax
from jax.experimental import pallas as pl
from jax.experimental.pallas import tpu as pltpu

F32 = jnp.float32
BF16 = jnp.bfloat16

D_MODEL = 1024
D_FF = 2816
FFN_HALF = 0.5
NORM_EPS = 1e-5
MEM_LEN = 256

RW_HEADS = 8
RW_HEAD_DIM = 64
RW_WIDTH = 512
DECAY_LORA = 64
AAA_LORA = 64
GATE_LORA = 128
RW_COLS = 3 * RW_WIDTH + DECAY_LORA + AAA_LORA + GATE_LORA
GN_EPS = 64e-5
RW_PAIRS = RW_HEADS // 2
PAIR = 2 * RW_HEAD_DIM

ML_HEADS = 4
ML_HEAD_DIM = 128
ML_WIDTH = 512
ML_CONV = 4
ML_NORM_EPS = 1e-6
ML_QKVO = 4 * ML_WIDTH
IF_PAD = 128

XA_HEADS = 4
XA_HEAD_DIM = 256

CHUNK = 64
LANES = 128
SUBLANES = 8
NEG_BIG = -1e30

VMEM_LIMIT = 56 * 1024 * 1024


def _bdot(a, b):
    return jnp.dot(a.astype(BF16), b.astype(BF16), preferred_element_type=F32)


def _bdot_nt(a, b):
    return lax.dot_general(a.astype(BF16), b.astype(BF16), (((1,), (1,)), ((), ())),
                           preferred_element_type=F32)


def _bdot_tn(a, b):
    return lax.dot_general(a.astype(BF16), b.astype(BF16), (((0,), (0,)), ((), ())),
                           preferred_element_type=F32)


def _split3(x):
    hi = x.astype(BF16)
    r1 = x - hi.astype(F32)
    mid = r1.astype(BF16)
    lo = (r1 - mid.astype(F32)).astype(BF16)
    return hi, mid, lo


def _dot_exact_lhs(c_bf16, x):
    hi, mid, lo = _split3(x)
    f = lambda p: jnp.dot(c_bf16, p, preferred_element_type=F32)
    return f(hi) + f(mid) + f(lo)


def _dot_split2_rhs(x, c_bf16):
    hi = x.astype(BF16)
    lo = (x - hi.astype(F32)).astype(BF16)
    f = lambda p: jnp.dot(p, c_bf16, preferred_element_type=F32)
    return f(hi) + f(lo)


def _per_slab(fn, x, c_bf16):
    n = x.shape[1] // LANES
    return jnp.concatenate([fn(x[:, i * LANES:(i + 1) * LANES], c_bf16) for i in range(n)], axis=1)


def _dot_f32(a, b):
    a_hi = a.astype(BF16)
    a_lo = (a - a_hi.astype(F32)).astype(BF16)
    b_hi = b.astype(BF16)
    b_lo = (b - b_hi.astype(F32)).astype(BF16)
    f = lambda p, q: jnp.dot(p, q, preferred_element_type=F32)
    return f(a_hi, b_hi) + f(a_hi, b_lo) + f(a_lo, b_hi)


def _sigmoid(x):
    return 1.0 / (1.0 + jnp.exp(-x))


def _silu(x):
    return x * _sigmoid(x)


def _rmsnorm(x, g):
    ms = jnp.mean(x * x, axis=-1, keepdims=True)
    return x * lax.rsqrt(ms + NORM_EPS) * g


def _shift_rows(u, carry8, j):
    rolled = pltpu.roll(u, j, axis=0)
    rolled_c = pltpu.roll(carry8, j, axis=0)
    row = lax.broadcasted_iota(jnp.int32, (SUBLANES, u.shape[1]), 0)
    top = jnp.where(row < j, rolled_c, rolled[:SUBLANES])
    return jnp.concatenate([top, rolled[SUBLANES:]], axis=0)


def _shift_rows_via(buf_ref, u, carry8, j):
    rows = u.shape[0]
    buf_ref[0:SUBLANES, :] = carry8
    buf_ref[SUBLANES:, :] = u
    return buf_ref[SUBLANES - j:SUBLANES - j + rows, :]


def _const_spec(shape):
    nd = len(shape)
    return pl.BlockSpec(shape, lambda *_: (0,) * nd, pipeline_mode=pl.Buffered(1))


def _params(sem):
    return pltpu.CompilerParams(dimension_semantics=sem, vmem_limit_bytes=VMEM_LIMIT)


FF_CHUNK = 256
FFN_TM = 1024


def _ffn_kernel(x_ref, g_ref, wg_ref, wu_ref, wd_ref, fg_ref, o_ref, *, final):
    x = x_ref[...]
    h = _rmsnorm(x, g_ref[...]).astype(BF16)
    acc = jnp.zeros(x.shape, F32)
    for c in range(D_FF // FF_CHUNK):
        sl = slice(c * FF_CHUNK, (c + 1) * FF_CHUNK)
        gate = jnp.dot(h, wg_ref[:, sl], preferred_element_type=F32)
        up = jnp.dot(h, wu_ref[:, sl], preferred_element_type=F32)
        act = (_silu(gate) * up).astype(BF16)
        acc = acc + jnp.dot(act, wd_ref[sl, :], preferred_element_type=F32)
    y = x + FFN_HALF * acc
    if final:
        y = _rmsnorm(y, fg_ref[...])
    o_ref[...] = y


def _ffn_call(x, g, wg, wu, wd, fg, final):
    t, d = x.shape
    tm = min(FFN_TM, t)
    return pl.pallas_call(
        functools.partial(_ffn_kernel, final=final),
        grid=(t // tm,),
        in_specs=[pl.BlockSpec((tm, d), lambda i: (i, 0)),
                  _const_spec((1, d)), _const_spec((d, D_FF)), _const_spec((d, D_FF)),
                  _const_spec((D_FF, d)), _const_spec((1, d))],
        out_specs=pl.BlockSpec((tm, d), lambda i: (i, 0)),
        out_shape=jax.ShapeDtypeStruct((t, d), F32),
        compiler_params=_params(("parallel",)),
        name="ffn_final" if final else "ffn",
    )(x, g, wg, wu, wd, fg)


INPROJ_TM = 1024
INPROJ_COLS = 256


def _inproj_kernel(x_ref, g_ref, wrw_ref, wml_ref, wif_ref, mu_ref, convw_ref, convb_ref,
                   zrw_ref, zml_ref, zif_ref, carry_rw_ref, carry_ml_ref, buf_a_ref, buf_b_ref):
    @pl.when(pl.program_id(1) == 0)
    def _():
        carry_rw_ref[...] = jnp.zeros(carry_rw_ref.shape, F32)
        carry_ml_ref[...] = jnp.zeros(carry_ml_ref.shape, F32)

    rows = x_ref.shape[0]
    h = _rmsnorm(x_ref[...], g_ref[...]).astype(BF16)

    def conv_stage(cols, scale):
        def finish(u):
            carry = carry_ml_ref[:, cols]
            w0, w1, w2, w3 = (convw_ref[j:j + 1, cols] for j in range(ML_CONV))
            u1 = _shift_rows_via(buf_a_ref, u, carry, 1)
            pair = w2 * u + w3 * u1
            pair_carry = w2 * carry + w3 * pltpu.roll(carry, 1, axis=0)
            conv = convb_ref[:, cols] + w0 * u + w1 * u1 + _shift_rows_via(buf_b_ref, pair, pair_carry, 2)
            carry_ml_ref[:, cols] = u[rows - SUBLANES:]
            zml_ref[:, cols] = _silu(conv) * scale if scale != 1.0 else _silu(conv)
        return wml_ref, cols, finish

    def plain_stage(cols, fn):
        def finish(u):
            zml_ref[:, cols] = fn(u)
        return wml_ref, cols, finish

    def lerp_stage(cols):
        def finish(z):
            prev = _shift_rows_via(buf_a_ref, z, carry_rw_ref[:, cols], 1)
            carry_rw_ref[:, cols] = z[rows - SUBLANES:]
            zrw_ref[:, cols] = z + (prev - z) * mu_ref[:, cols]
        return wrw_ref, cols, finish

    blk = lambda c: slice(c * INPROJ_COLS, (c + 1) * INPROJ_COLS)
    n_head_blocks = ML_WIDTH // INPROJ_COLS
    heavy, light = [], []
    for c in range(ML_QKVO // INPROJ_COLS):
        group = c // n_head_blocks
        if group < 2:
            heavy.append(conv_stage(blk(c), ML_HEAD_DIM ** -0.5 if group == 0 else 1.0))
        else:
            light.append(plain_stage(blk(c), _sigmoid if group == 3 else (lambda u: u)))
    light += [lerp_stage(blk(c)) for c in range(RW_COLS // INPROJ_COLS)]

    per_heavy = -(-len(light) // len(heavy))
    stages = []
    for i, stage in enumerate(heavy):
        stages.append(stage)
        stages += light[i * per_heavy:(i + 1) * per_heavy]
    pending = None
    for w_ref, cols, finish in stages:
        acc = jnp.dot(h, w_ref[:, cols], preferred_element_type=F32)
        if pending is not None:
            pending[0](pending[1])
        pending = (finish, acc)
    zif_ref[...] = jnp.dot(h, wif_ref[...], preferred_element_type=F32)
    pending[0](pending[1])


def _inproj_call(x, g, wrw, wml, wif, mu, convw, convb, bsz, seq):
    t, d = x.shape
    tm = min(INPROJ_TM, seq)
    nt = seq // tm
    row = lambda b, i: (b * nt + i, 0)
    return pl.pallas_call(
        _inproj_kernel,
        grid=(bsz, nt),
        in_specs=[pl.BlockSpec((tm, d), row),
                  _const_spec((1, d)), _const_spec((d, RW_COLS)), _const_spec((d, ML_QKVO)),
                  _const_spec((d, IF_PAD)), _const_spec((1, RW_COLS)),
                  _const_spec(convw.shape), _const_spec(convb.shape)],
        out_specs=[pl.BlockSpec((tm, RW_COLS), row), pl.BlockSpec((tm, ML_QKVO), row),
                   pl.BlockSpec((tm, IF_PAD), row)],
        out_shape=[jax.ShapeDtypeStruct((t, RW_COLS), F32), jax.ShapeDtypeStruct((t, ML_QKVO), F32),
                   jax.ShapeDtypeStruct((t, IF_PAD), F32)],
        scratch_shapes=[pltpu.VMEM((SUBLANES, RW_COLS), F32), pltpu.VMEM((SUBLANES, 2 * ML_WIDTH), F32),
                        pltpu.VMEM((SUBLANES + tm, INPROJ_COLS), F32),
                        pltpu.VMEM((SUBLANES + tm, INPROJ_COLS), F32)],
        compiler_params=_params(("parallel", "arbitrary")),
        name="inproj",
    )(x, g, wrw, wml, wif, mu, convw, convb)


RW1_CHUNKS = 16
RW1_GROUP = 16
INV_LEVELS = 6


def _rw_masks():
    t = np.arange(CHUNK)[:, None]
    s = np.tile(np.arange(CHUNK), 2)[None, :]
    levels = []
    for k in range(INV_LEVELS):
        size = 1 << k
        levels.append(((t // (2 * size)) == (s // (2 * size))) & ((t // size) % 2 == 1) & ((s // size) % 2 == 0))
    m = np.stack([s < t, s <= t, s == t] + levels).astype(np.float32)
    tri = (np.arange(CHUNK)[:, None] >= np.arange(CHUNK)[None, :]).astype(np.float32)
    return m, tri


def _group_ones(width, group):
    i = np.arange(width)
    return (i[:, None] // group == i[None, :] // group).astype(np.float32)


def _rw1_kernel(z_ref, w2_ref, a2_ref, g2_ref, vec_ref, gsum_ref, tri_ref, mask_ref,
                g_ref, yi_ref, bonus_ref, gate_ref, m_ref, c_ref):
    z = z_ref[...]
    rows = z.shape[0]
    r = z[:, 0:RW_WIDTH]
    k = z[:, RW_WIDTH:2 * RW_WIDTH]
    v = z[:, 2 * RW_WIDTH:3 * RW_WIDTH]
    lora_in = z[:, 3 * RW_WIDTH:3 * RW_WIDTH + DECAY_LORA + AAA_LORA]
    g_d = z[:, 3 * RW_WIDTH + DECAY_LORA + AAA_LORA:]
    w0 = vec_ref[0:1, :]
    a0 = vec_ref[1:2, :]
    k_k = vec_ref[2:3, :]
    k_a = vec_ref[3:4, :]
    r_k = vec_ref[4:5, :]

    w_raw = w0 + _bdot(jnp.tanh(lora_in), w2_ref[...])
    logw = -math.exp(-0.5) * _sigmoid(w_raw)
    a = _sigmoid(a0 + _bdot(lora_in, a2_ref[...]))
    gate_ref[...] = _bdot(_sigmoid(g_d), g2_ref[...])

    gsum = gsum_ref[...]
    kk = k * k_k
    kk = kk / jnp.maximum(jnp.sqrt(_per_slab(_bdot, kk * kk, gsum)), 1e-12)
    k_mod = k * (1.0 + (a - 1.0) * k_a)
    b = kk * a
    bonus_ref[...] = _per_slab(_dot_split2_rhs, r * k_mod * r_k, gsum) * v

    lane = lax.broadcasted_iota(jnp.int32, (CHUNK, PAIR), 1)
    head0 = lane < RW_HEAD_DIM
    strict = mask_ref[0]
    incl = mask_ref[1]
    eye = mask_ref[2]
    tri = tri_ref[...]

    def stack(x):
        return jnp.concatenate([jnp.where(head0, x, 0.0), jnp.where(head0, 0.0, x)], axis=0).astype(BF16)

    ops = {}
    for c in range(rows // CHUNK):
        rs = slice(c * CHUNK, (c + 1) * CHUNK)
        lw = logw[rs]
        cum = _dot_exact_lhs(tri, lw)
        cum_last = cum[CHUNK - 1:CHUNK]
        w_incl = jnp.exp(cum)
        w_inv = 1.0 / w_incl
        w_prev = jnp.exp(cum - lw)
        w_last = jnp.exp(cum_last)
        bt = b[rs] * w_inv
        kt = k_mod[rs] * w_inv
        full = dict(at=-kk[rs] * w_prev, rt=r[rs] * w_incl, bt=bt, kt=kt, be=bt * w_last, ke=kt * w_last, v2=v[rs])
        for p in range(RW_PAIRS):
            ls = slice(p * PAIR, (p + 1) * PAIR)
            item = {name: val[:, ls] for name, val in full.items()}
            item["w_last"] = w_last[:, ls]
            ops[(c, p)] = item

    keys = list(ops)
    for g0 in range(0, len(keys), RW1_GROUP):
        group = keys[g0:g0 + RW1_GROUP]
        n_ab, a_ak, a_rb, a_rk, t_inv, sv = {}, {}, {}, {}, {}, {}
        for key in group:
            o = ops[key]
            prod = _bdot_nt(jnp.concatenate([o["at"], o["rt"]], axis=0),
                            jnp.concatenate([stack(o["bt"]), stack(o["kt"])], axis=0))
            n_ab[key] = prod[:CHUNK, :PAIR] * strict
            a_ak[key] = prod[:CHUNK, PAIR:] * strict
            a_rb[key] = prod[CHUNK:, :PAIR] * incl
            a_rk[key] = prod[CHUNK:, PAIR:] * incl
            sv[key] = stack(o["v2"])
            t_inv[key] = eye + n_ab[key] * mask_ref[3]
        for lvl in range(1, INV_LEVELS):
            half = {key: _bdot(t_inv[key], stack(n_ab[key] * mask_ref[3 + lvl])) for key in group}
            t_inv = {key: t_inv[key] + _bdot(half[key], stack(t_inv[key])) for key in group}
        akv = {key: _bdot(a_ak[key], sv[key]) for key in group}
        pq = {key: _bdot(t_inv[key], jnp.concatenate([stack(ops[key]["at"]), stack(akv[key])], axis=1))
              for key in group}
        gy = {key: _bdot(a_rb[key], jnp.concatenate([stack(pq[key][:, :PAIR]), stack(pq[key][:, PAIR:])], axis=1))
              for key in group}
        ark_v = {key: _bdot(a_rk[key], sv[key]) for key in group}
        mc = {key: _bdot_tn(ops[key]["be"], pq[key]) for key in group}
        ke_v = {key: _bdot_tn(ops[key]["ke"], ops[key]["v2"]) for key in group}
        for key in group:
            c, p = key
            rs = slice(c * CHUNK, (c + 1) * CHUNK)
            ls = slice(p * PAIR, (p + 1) * PAIR)
            g_ref[rs, ls] = (ops[key]["rt"] + gy[key][:, :PAIR]).astype(BF16)
            yi_ref[rs, ls] = gy[key][:, PAIR:] + ark_v[key]
            cv = mc[key][:, PAIR:] + ke_v[key]
            m_ref[c, p] = eye * ops[key]["w_last"] + jnp.where(head0, mc[key][:CHUNK, :PAIR], mc[key][CHUNK:, :PAIR])
            c_ref[c, p] = jnp.where(head0, cv[:CHUNK], cv[CHUNK:])


def _rw1_call(zrw, w2p, a2p, g2, vec, gsum, tri, masks):
    t = zrw.shape[0]
    rows = RW1_CHUNKS * CHUNK
    nchunks = t // CHUNK
    tok = lambda i: (i, 0)
    blk = lambda i: (i, 0, 0, 0)
    tok_spec = pl.BlockSpec((rows, RW_WIDTH), tok)
    mc_spec = pl.BlockSpec((RW1_CHUNKS, RW_PAIRS, CHUNK, PAIR), blk)
    tok_shape = jax.ShapeDtypeStruct((t, RW_WIDTH), F32)
    mc_shape = jax.ShapeDtypeStruct((nchunks, RW_PAIRS, CHUNK, PAIR), F32)
    return pl.pallas_call(
        _rw1_kernel,
        grid=(t // rows,),
        in_specs=[pl.BlockSpec((rows, RW_COLS), tok),
                  _const_spec(w2p.shape), _const_spec(a2p.shape), _const_spec(g2.shape),
                  _const_spec(vec.shape), _const_spec(gsum.shape), _const_spec(tri.shape),
                  _const_spec(masks.shape)],
        out_specs=[tok_spec, tok_spec, tok_spec, tok_spec, mc_spec, mc_spec],
        out_shape=[jax.ShapeDtypeStruct((t, RW_WIDTH), BF16), tok_shape, tok_shape, tok_shape, mc_shape, mc_shape],
        compiler_params=_params(("parallel",)),
        name="rwkv_chunk_local",
    )(zrw, w2p, a2p, g2, vec, gsum, tri, masks)


RW2_CHUNKS = 4
RW2_BATCH = 4


def _rw2_kernel(g_ref, yi_ref, bonus_ref, gate_ref, m_ref, c_ref, gmean_ref, gn_ref, y_ref, h_ref):
    @pl.when(pl.program_id(1) == 0)
    def _():
        h_ref[...] = jnp.zeros(h_ref.shape, F32)

    lane = lax.broadcasted_iota(jnp.int32, (CHUNK, PAIR), 1)
    head0 = lane < RW_HEAD_DIM

    def block_diag(x):
        return jnp.concatenate([jnp.where(head0, x, 0.0), jnp.where(head0, 0.0, x)], axis=0)

    nb = g_ref.shape[0]
    chains = [(b, p) for b in range(nb) for p in range(RW_PAIRS)]
    lsl = [slice(p * PAIR, (p + 1) * PAIR) for p in range(RW_PAIRS)]
    state = {(b, p): h_ref[b, p] for b, p in chains}
    ys = {}
    for c in range(RW2_CHUNKS):
        rs = slice(c * CHUNK, (c + 1) * CHUNK)
        for b, p in chains:
            ys[b, c, p] = _bdot(g_ref[b, rs, lsl[p]], state[b, p]) + yi_ref[b, rs, lsl[p]]
        state = {(b, p): _dot_f32(block_diag(m_ref[b, c, p]), state[b, p]) + block_diag(c_ref[b, c, p])
                 for b, p in chains}
    for b, p in chains:
        h_ref[b, p] = state[b, p]
    gmean = gmean_ref[...]
    for b in range(nb):
        y = jnp.concatenate([jnp.concatenate([ys[b, c, p] for p in range(RW_PAIRS)], axis=1)
                             for c in range(RW2_CHUNKS)], axis=0)
        mu = _per_slab(_dot_split2_rhs, y, gmean)
        d = y - mu
        var = _per_slab(_bdot, d * d, gmean)
        y = d * lax.rsqrt(var + GN_EPS) * gn_ref[0:1, :] + gn_ref[1:2, :]
        y_ref[b] = ((y + bonus_ref[b]) * gate_ref[b]).astype(BF16)


def _rw2_call(g, yi, bonus, gate, m, c, gmean, gn, bsz, seq):
    t = g.shape[0]
    rows = RW2_CHUNKS * CHUNK
    nt = seq // rows
    nb = RW2_BATCH if bsz % RW2_BATCH == 0 else 1
    per_batch = lambda a: a.reshape((bsz, a.shape[0] // bsz) + a.shape[1:])
    tok_spec = pl.BlockSpec((nb, rows, RW_WIDTH), lambda b, i: (b, i, 0))
    mc_spec = pl.BlockSpec((nb, RW2_CHUNKS, RW_PAIRS, CHUNK, PAIR), lambda b, i: (b, i, 0, 0, 0))
    y = pl.pallas_call(
        _rw2_kernel,
        grid=(bsz // nb, nt),
        in_specs=[tok_spec, tok_spec, tok_spec, tok_spec, mc_spec, mc_spec,
                  _const_spec(gmean.shape), _const_spec(gn.shape)],
        out_specs=tok_spec,
        out_shape=jax.ShapeDtypeStruct((bsz, seq, RW_WIDTH), BF16),
        scratch_shapes=[pltpu.VMEM((nb, RW_PAIRS, PAIR, PAIR), F32)],
        compiler_params=_params(("parallel", "arbitrary")),
        name="rwkv_scan",
    )(per_batch(g), per_batch(yi), per_batch(bonus), per_batch(gate), per_batch(m), per_batch(c), gmean, gn)
    return y.reshape(t, RW_WIDTH)


ML_CHUNK = 128
ML_STEP_CHUNKS = 4
ML_BATCH = 1


def _log_sigmoid(x):
    return jnp.minimum(x, 0.0) - jnp.log(1.0 + jnp.exp(-jnp.abs(x)))


def _ml_kernel(z_ref, zif_ref, ifb_ref, norm_ref, tri_ref, y_ref, state_ref, m_ref):
    @pl.when(pl.program_id(1) == 0)
    def _():
        state_ref[...] = jnp.zeros(state_ref.shape, F32)
        m_ref[...] = jnp.zeros(m_ref.shape, F32)

    nb, rows = z_ref.shape[0], z_ref.shape[1]
    q_all = [z_ref[b, :, 0:ML_WIDTH] for b in range(nb)]
    k_all = [z_ref[b, :, ML_WIDTH:2 * ML_WIDTH] for b in range(nb)]
    v_all = [z_ref[b, :, 2 * ML_WIDTH:3 * ML_WIDTH] for b in range(nb)]
    ogate_all = [z_ref[b, :, 3 * ML_WIDTH:4 * ML_WIDTH] for b in range(nb)]

    pre = [zif_ref[b] + ifb_ref[...] for b in range(nb)]
    logf = [_log_sigmoid(pre[b]) for b in range(nb)]
    tri = tri_ref[...]
    lane = lax.broadcasted_iota(jnp.int32, (ML_CHUNK, LANES), 1)
    tq = lax.broadcasted_iota(jnp.int32, (ML_CHUNK, ML_CHUNK), 0)
    tk = lax.broadcasted_iota(jnp.int32, (ML_CHUNK, ML_CHUNK), 1)
    causal = tk <= tq
    ones = jnp.ones((ML_CHUNK, ML_HEAD_DIM), F32)
    dense = (ML_CHUNK, ML_CHUNK)

    chains = [(b, h) for b in range(nb) for h in range(ML_HEADS)]
    lsl = [slice(h * ML_HEAD_DIM, (h + 1) * ML_HEAD_DIM) for h in range(ML_HEADS)]
    m_run = {ch: m_ref[ch] for ch in chains}
    state_run = {ch: state_ref[ch] for ch in chains}
    ys = {b: [] for b in range(nb)}
    for c in range(rows // ML_CHUNK):
        rs = slice(c * ML_CHUNK, (c + 1) * ML_CHUNK)
        z, z_t = [], []
        for b in range(nb):
            bcum = _dot_exact_lhs(tri, logf[b][rs])
            z.append(jnp.where(lane < ML_HEADS, pre[b][rs], bcum))
            z_t.append(z[b].T)
        b_col = {(b, h): jnp.broadcast_to(z[b][:, ML_HEADS + h:ML_HEADS + h + 1], dense)
                 for b, h in chains}
        ib_col = {(b, h): jnp.broadcast_to(z[b][:, h:h + 1] - z[b][:, ML_HEADS + h:ML_HEADS + h + 1], dense)
                  for b, h in chains}
        ib_row = {(b, h): z_t[b][h:h + 1, :] - z_t[b][ML_HEADS + h:ML_HEADS + h + 1, :]
                  for b, h in chains}
        b_end = {ch: b_col[ch][ML_CHUNK - 1:ML_CHUNK] for ch in chains}
        m_prev = m_run
        state = state_run
        q = {(b, h): q_all[b][rs, lsl[h]] for b, h in chains}
        k = {(b, h): k_all[b][rs, lsl[h]] for b, h in chains}
        v = {(b, h): v_all[b][rs, lsl[h]] for b, h in chains}
        s_qk = {ch: _bdot_nt(q[ch], k[ch]) for ch in chains}
        inter = {ch: _bdot(q[ch], state[ch]) for ch in chains}

        log_d = {ch: jnp.where(causal, b_col[ch] + ib_row[ch], NEG_BIG) for ch in chains}
        log_inter = {ch: b_col[ch] + m_prev[ch] for ch in chains}
        m_row = {ch: jnp.maximum(log_inter[ch], jnp.max(log_d[ch], axis=-1, keepdims=True)) for ch in chains}
        p = {ch: jnp.exp(log_d[ch] - m_row[ch]) * s_qk[ch] for ch in chains}
        intra = {ch: _bdot(p[ch], jnp.concatenate([v[ch], ones], axis=1)) for ch in chains}

        log_w_end = {ch: ib_col[ch] + b_end[ch] for ch in chains}
        m_chunk = {ch: jnp.max(log_w_end[ch], axis=0, keepdims=True) for ch in chains}
        w_end = {ch: jnp.exp(log_w_end[ch] - m_chunk[ch]) for ch in chains}
        kv = {ch: _bdot_tn(k[ch], jnp.concatenate([w_end[ch] * v[ch], w_end[ch]], axis=1)) for ch in chains}
        m_run, state_run = {}, {}
        for ch in chains:
            m_new = jnp.maximum(b_end[ch] + m_prev[ch], m_chunk[ch])
            dec = jnp.exp(b_end[ch] + m_prev[ch] - m_new)
            sc = jnp.exp(m_chunk[ch] - m_new)
            state_run[ch] = (jnp.concatenate([dec, dec], axis=1) * state[ch]
                             + jnp.concatenate([sc, sc], axis=1) * kv[ch])
            m_run[ch] = m_new

        outs = {}
        for ch in chains:
            b, h = ch
            w_inter = jnp.exp(log_inter[ch] - m_row[ch])
            num = intra[ch][:, :ML_HEAD_DIM] + w_inter * inter[ch][:, :ML_HEAD_DIM]
            den = intra[ch][:, ML_HEAD_DIM:] + w_inter * inter[ch][:, ML_HEAD_DIM:]
            hh = num / jnp.maximum(jnp.abs(den), jnp.exp(-m_row[ch]))
            mu = jnp.mean(hh, axis=-1, keepdims=True)
            d = hh - mu
            var = jnp.mean(d * d, axis=-1, keepdims=True)
            hn = d * lax.rsqrt(var + ML_NORM_EPS)
            outs[ch] = hn * norm_ref[:, lsl[h]] * ogate_all[b][rs, lsl[h]]
        for b in range(nb):
            ys[b].append(jnp.concatenate([outs[b, h] for h in range(ML_HEADS)], axis=1).astype(BF16))
    for b in range(nb):
        y_ref[b] = jnp.concatenate(ys[b], axis=0)
    for ch in chains:
        state_ref[ch] = state_run[ch]
        m_ref[ch] = m_run[ch]


def _ml_call(zml, zif, ifb, norm, tri, bsz, seq):
    t = zml.shape[0]
    rows = ML_STEP_CHUNKS * ML_CHUNK
    nt = seq // rows
    nb = ML_BATCH if bsz % ML_BATCH == 0 else 1
    tok = lambda b, i: (b, i, 0)
    y = pl.pallas_call(
        _ml_kernel,
        grid=(bsz // nb, nt),
        in_specs=[pl.BlockSpec((nb, rows, ML_QKVO), tok), pl.BlockSpec((nb, rows, IF_PAD), tok),
                  _const_spec(ifb.shape), _const_spec(norm.shape), _const_spec(tri.shape)],
        out_specs=pl.BlockSpec((nb, rows, ML_WIDTH), tok),
        out_shape=jax.ShapeDtypeStruct((bsz, seq, ML_WIDTH), BF16),
        scratch_shapes=[pltpu.VMEM((nb, ML_HEADS, ML_HEAD_DIM, 2 * ML_HEAD_DIM), F32),
                        pltpu.VMEM((nb, ML_HEADS, 1, LANES), F32)],
        compiler_params=_params(("parallel", "arbitrary")),
        name="mlstm",
    )(zml.reshape(bsz, seq, ML_QKVO), zif.reshape(bsz, seq, IF_PAD), ifb, norm, tri)
    return y.reshape(t, ML_WIDTH)


MERGE_TM = 1024


def _merge_kernel(x_ref, g_ref, wgt_ref, yrw_ref, yml_ref, wur_ref, wum_ref, wo_ref, o_ref):
    x = x_ref[...]
    d = x.shape[1]
    h = _rmsnorm(x, g_ref[...]).astype(BF16)
    gates = _sigmoid(jnp.dot(h, wgt_ref[...], preferred_element_type=F32))
    up_rw = _bdot(yrw_ref[...], wur_ref[...])
    up_ml = _bdot(yml_ref[...], wum_ref[...])
    merged = gates[:, 0:d] * up_rw + gates[:, d:] * up_ml
    o_ref[...] = x + _bdot(merged, wo_ref[...])


def _merge_call(x, g, wgt, yrw, yml, wur, wum, wo):
    t, d = x.shape
    tm = min(MERGE_TM, t)
    row = lambda i: (i, 0)
    return pl.pallas_call(
        _merge_kernel,
        grid=(t // tm,),
        in_specs=[pl.BlockSpec((tm, d), row), _const_spec(g.shape), _const_spec(wgt.shape),
                  pl.BlockSpec((tm, RW_WIDTH), row), pl.BlockSpec((tm, ML_WIDTH), row),
                  _const_spec(wur.shape), _const_spec(wum.shape), _const_spec(wo.shape)],
        out_specs=pl.BlockSpec((tm, d), row),
        out_shape=jax.ShapeDtypeStruct((t, d), F32),
        compiler_params=_params(("parallel",)),
        name="merge",
    )(x, g, wgt, yrw, yml, wur, wum, wo)


def _memkv_kernel(mem_ref, g_ref, wkv_ref, k_ref, v_ref):
    d = mem_ref.shape[1]
    hm = _rmsnorm(mem_ref[...], g_ref[...])
    kv = jnp.dot(hm, wkv_ref[...], preferred_element_type=F32)
    k_ref[...] = kv[:, 0:d].astype(BF16)
    v_ref[...] = kv[:, d:].astype(BF16)


def _memkv_call(mem2d, g, wkv):
    n, d = mem2d.shape
    row = lambda i: (i, 0)
    return pl.pallas_call(
        _memkv_kernel,
        grid=(n // MEM_LEN,),
        in_specs=[pl.BlockSpec((MEM_LEN, d), row), _const_spec(g.shape), _const_spec(wkv.shape)],
        out_specs=[pl.BlockSpec((MEM_LEN, d), row), pl.BlockSpec((MEM_LEN, d), row)],
        out_shape=[jax.ShapeDtypeStruct((n, d), BF16), jax.ShapeDtypeStruct((n, d), BF16)],
        compiler_params=_params(("parallel",)),
        name="mem_kv",
    )(mem2d, g, wkv)


XA_TM = 1024


def _xattn_kernel(x_ref, g_ref, wq_ref, k_ref, v_ref, wo_ref, o_ref):
    x = x_ref[...]
    hq = _rmsnorm(x, g_ref[...])
    q = _bdot(hq, wq_ref[...]) * (XA_HEAD_DIM ** -0.5)
    lsl = [slice(h * XA_HEAD_DIM, (h + 1) * XA_HEAD_DIM) for h in range(XA_HEADS)]

    def softmax(s):
        e = jnp.exp(s - jnp.max(s, axis=-1, keepdims=True))
        return e / jnp.sum(e, axis=-1, keepdims=True)

    outs = []
    s_prev = _bdot_nt(q[:, lsl[0]], k_ref[:, lsl[0]])
    for h in range(XA_HEADS):
        s_next = _bdot_nt(q[:, lsl[h + 1]], k_ref[:, lsl[h + 1]]) if h + 1 < XA_HEADS else None
        outs.append(_bdot(softmax(s_prev), v_ref[:, lsl[h]]))
        s_prev = s_next
    o = jnp.concatenate(outs, axis=1)
    o_ref[...] = x + _bdot(o, wo_ref[...])


def _xattn_call(x, g, wq, kmem, vmem, wo, bsz, seq):
    t, d = x.shape
    tm = min(XA_TM, seq)
    nt = seq // tm
    row = lambda b, i: (b * nt + i, 0)
    mem = lambda b, i: (b, 0)
    return pl.pallas_call(
        _xattn_kernel,
        grid=(bsz, nt),
        in_specs=[pl.BlockSpec((tm, d), row), _const_spec(g.shape), _const_spec(wq.shape),
                  pl.BlockSpec((MEM_LEN, d), mem), pl.BlockSpec((MEM_LEN, d), mem),
                  _const_spec(wo.shape)],
        out_specs=pl.BlockSpec((tm, d), row),
        out_shape=jax.ShapeDtypeStruct((t, d), F32),
        compiler_params=_params(("parallel", "parallel")),
        name="mem_xattn",
    )(x, g, wq, kmem, vmem, wo)


def _row(v):
    return v.reshape(1, -1).astype(F32)


def _layer(x, mem2d, bsz, seq, p):
    d = D_MODEL
    bf = lambda w: w.astype(BF16)
    masks_np, tri_np = _rw_masks()
    masks = jnp.asarray(masks_np)
    tri = jnp.asarray(tri_np, dtype=BF16)
    gsum = jnp.asarray(_group_ones(PAIR, RW_HEAD_DIM), dtype=BF16)
    gmean = jnp.asarray(_group_ones(PAIR, RW_HEAD_DIM) / RW_HEAD_DIM, dtype=BF16)

    x = _ffn_call(x, _row(p["ffn1_norm"]), bf(p["ffn1_w_gate"]), bf(p["ffn1_w_up"]), bf(p["ffn1_w_down"]),
                  _row(p["ffn1_norm"]), final=False)

    w_in = p["w_in"]
    c0 = RW_COLS
    c1 = c0 + ML_QKVO
    c2 = c1 + 2 * ML_HEADS
    w_if = jnp.pad(w_in[:, c1:c2], ((0, 0), (0, IF_PAD - 2 * ML_HEADS)))
    convw = jnp.concatenate([p["ml_conv_w"].astype(F32), jnp.zeros((SUBLANES - ML_CONV, 2 * ML_WIDTH), F32)], axis=0)
    zrw, zml, zif = _inproj_call(x, _row(p["mix_norm"]), bf(w_in[:, :c0]), bf(w_in[:, c0:c1]), bf(w_if),
                                 _row(p["rw_mu"]), convw, _row(p["ml_conv_b"]), bsz, seq)

    zpad = jnp.zeros((DECAY_LORA, RW_WIDTH), F32)
    w2p = bf(jnp.concatenate([p["rw_w2"], zpad], axis=0))
    a2p = bf(jnp.concatenate([zpad, p["rw_a2"]], axis=0))
    vec = jnp.stack([p["rw_w0"], p["rw_a0"], p["rw_k_k"], p["rw_k_a"], p["rw_r_k"].reshape(-1),
                     jnp.zeros_like(p["rw_w0"]), jnp.zeros_like(p["rw_w0"]), jnp.zeros_like(p["rw_w0"])]).astype(F32)
    g, yi, bonus, gate, m, c = _rw1_call(zrw, w2p, a2p, bf(p["rw_g2"]), vec, gsum, tri, masks)
    gn = jnp.concatenate([jnp.stack([p["rw_gn_w"], p["rw_gn_b"]]).astype(F32),
                          jnp.zeros((SUBLANES - 2, RW_WIDTH), F32)], axis=0)
    y_rw = _rw2_call(g, yi, bonus, gate, m, c, gmean, gn, bsz, seq)

    ifb = jnp.pad(jnp.concatenate([p["ml_i_b"], p["ml_f_b"]]).astype(F32), (0, IF_PAD - 2 * ML_HEADS)).reshape(1, -1)
    tri_ml = jnp.asarray(np.tril(np.ones((ML_CHUNK, ML_CHUNK), np.float32)), dtype=BF16)
    y_ml = _ml_call(zml, zif, ifb, _row(p["ml_norm"]), tri_ml, bsz, seq)

    x = _merge_call(x, _row(p["mix_norm"]), bf(w_in[:, c2:]), y_rw, y_ml,
                    p["w_up_rw"], p["w_up_ml"], p["w_out"])

    kmem, vmem = _memkv_call(mem2d, _row(p["mem_norm"]), p["xa_wkv"])
    x = _xattn_call(x, _row(p["xa_norm"]), p["xa_wq"], kmem, vmem, p["xa_wo"], bsz, seq)
    return x


def kernel(x, mem, ffn1_norm, ffn1_w_gate, ffn1_w_up, ffn1_w_down, mix_norm, w_in, rw_mu, rw_w0, rw_w2, rw_a0,
           rw_a2, rw_g2, rw_k_k, rw_k_a, rw_r_k, rw_gn_w, rw_gn_b, ml_conv_w, ml_conv_b, ml_i_b, ml_f_b, ml_norm,
           w_up_rw, w_up_ml, w_out, xa_norm, mem_norm, xa_wq, xa_wkv, xa_wo, ffn2_norm, ffn2_w_gate, ffn2_w_up,
           ffn2_w_down, final_norm):
    bsz, seq, d = x.shape
    depth = ffn1_norm.shape[0]
    stacked = dict(ffn1_norm=ffn1_norm, ffn1_w_gate=ffn1_w_gate, ffn1_w_up=ffn1_w_up, ffn1_w_down=ffn1_w_down,
                   mix_norm=mix_norm, w_in=w_in, rw_mu=rw_mu, rw_w0=rw_w0, rw_w2=rw_w2, rw_a0=rw_a0, rw_a2=rw_a2,
                   rw_g2=rw_g2, rw_k_k=rw_k_k, rw_k_a=rw_k_a, rw_r_k=rw_r_k, rw_gn_w=rw_gn_w, rw_gn_b=rw_gn_b,
                   ml_conv_w=ml_conv_w, ml_conv_b=ml_conv_b, ml_i_b=ml_i_b, ml_f_b=ml_f_b, ml_norm=ml_norm,
                   w_up_rw=w_up_rw, w_up_ml=w_up_ml, w_out=w_out, xa_norm=xa_norm, mem_norm=mem_norm,
                   xa_wq=xa_wq, xa_wkv=xa_wkv, xa_wo=xa_wo, ffn2_norm=ffn2_norm, ffn2_w_gate=ffn2_w_gate,
                   ffn2_w_up=ffn2_w_up, ffn2_w_down=ffn2_w_down)
    h = x.reshape(bsz * seq, d)
    mem2d = mem.reshape(bsz * mem.shape[1], d)
    bf = lambda w: w.astype(BF16)
    for l in range(depth):
        p = {name: w[l] for name, w in stacked.items()}
        h = _layer(h, mem2d, bsz, seq, p)
        last = l == depth - 1
        h = _ffn_call(h, _row(p["ffn2_norm"]), bf(p["ffn2_w_gate"]), bf(p["ffn2_w_up"]), bf(p["ffn2_w_down"]),
                      _row(final_norm), final=last)
    return h.reshape(bsz, seq, d)
```

```python
import functools
import math

import jax
import jax.numpy as jnp
import numpy as np
from jax import lax
from jax.experimental import pallas as pl
from jax.experimental.pallas import tpu as pltpu

F32 = jnp.float32
BF16 = jnp.bfloat16

D_MODEL = 1024
D_FF = 2816
FFN_HALF = 0.5
NORM_EPS = 1e-5
MEM_LEN = 256

RW_HEADS = 8
RW_HEAD_DIM = 64
RW_WIDTH = 512
DECAY_LORA = 64
AAA_LORA = 64
GATE_LORA = 128
RW_COLS = 3 * RW_WIDTH + DECAY_LORA + AAA_LORA + GATE_LORA
GN_EPS = 64e-5
RW_PAIRS = RW_HEADS // 2
PAIR = 2 * RW_HEAD_DIM

ML_HEADS = 4
ML_HEAD_DIM = 128
ML_WIDTH = 512
ML_CONV = 4
ML_NORM_EPS = 1e-6
ML_QKVO = 4 * ML_WIDTH
IF_PAD = 128

XA_HEADS = 4
XA_HEAD_DIM = 256

CHUNK = 64
LANES = 128
SUBLANES = 8
NEG_BIG = -1e30

VMEM_LIMIT = 56 * 1024 * 1024


def _bdot(a, b):
    return jnp.dot(a.astype(BF16), b.astype(BF16), preferred_element_type=F32)


def _bdot_nt(a, b):
    return lax.dot_general(a.astype(BF16), b.astype(BF16), (((1,), (1,)), ((), ())),
                           preferred_element_type=F32)


def _bdot_tn(a, b):
    return lax.dot_general(a.astype(BF16), b.astype(BF16), (((0,), (0,)), ((), ())),
                           preferred_element_type=F32)


def _split3(x):
    hi = x.astype(BF16)
    r1 = x - hi.astype(F32)
    mid = r1.astype(BF16)
    lo = (r1 - mid.astype(F32)).astype(BF16)
    return hi, mid, lo


def _dot_exact_lhs(c_bf16, x):
    hi, mid, lo = _split3(x)
    f = lambda p: jnp.dot(c_bf16, p, preferred_element_type=F32)
    return f(hi) + f(mid) + f(lo)


def _dot_split2_rhs(x, c_bf16):
    hi = x.astype(BF16)
    lo = (x - hi.astype(F32)).astype(BF16)
    f = lambda p: jnp.dot(p, c_bf16, preferred_element_type=F32)
    return f(hi) + f(lo)


def _per_slab(fn, x, c_bf16):
    n = x.shape[1] // LANES
    return jnp.concatenate([fn(x[:, i * LANES:(i + 1) * LANES], c_bf16) for i in range(n)], axis=1)


def _dot_f32(a, b):
    a_hi = a.astype(BF16)
    a_lo = (a - a_hi.astype(F32)).astype(BF16)
    b_hi = b.astype(BF16)
    b_lo = (b - b_hi.astype(F32)).astype(BF16)
    f = lambda p, q: jnp.dot(p, q, preferred_element_type=F32)
    return f(a_hi, b_hi) + f(a_hi, b_lo) + f(a_lo, b_hi)


def _sigmoid(x):
    return 1.0 / (1.0 + jnp.exp(-x))


def _silu(x):
    return x * _sigmoid(x)


def _rmsnorm(x, g):
    ms = jnp.mean(x * x, axis=-1, keepdims=True)
    return x * lax.rsqrt(ms + NORM_EPS) * g


def _shift_rows(u, carry8, j):
    rolled = pltpu.roll(u, j, axis=0)
    rolled_c = pltpu.roll(carry8, j, axis=0)
    row = lax.broadcasted_iota(jnp.int32, (SUBLANES, u.shape[1]), 0)
    top = jnp.where(row < j, rolled_c, rolled[:SUBLANES])
    return jnp.concatenate([top, rolled[SUBLANES:]], axis=0)


def _shift_rows_via(buf_ref, u, carry8, j):
    rows = u.shape[0]
    buf_ref[0:SUBLANES, :] = carry8
    buf_ref[SUBLANES:, :] = u
    return buf_ref[SUBLANES - j:SUBLANES - j + rows, :]


def _const_spec(shape):
    nd = len(shape)
    return pl.BlockSpec(shape, lambda *_: (0,) * nd, pipeline_mode=pl.Buffered(1))


def _params(sem, vmem_limit=VMEM_LIMIT):
    return pltpu.CompilerParams(dimension_semantics=sem, vmem_limit_bytes=vmem_limit)


FF_CHUNK = 256
FFN_TM = 1024
FFN_VMEM_LIMIT = 62 * 1024 * 1024


def _ffn_kernel(x_ref, g_ref, wg_ref, wu_ref, wd_ref, fg_ref, o_ref, *, final):
    x = x_ref[...]
    h = _rmsnorm(x, g_ref[...]).astype(BF16)
    acc = jnp.zeros(x.shape, F32)
    for c in range(D_FF // FF_CHUNK):
        sl = slice(c * FF_CHUNK, (c + 1) * FF_CHUNK)
        gate = jnp.dot(h, wg_ref[:, sl].astype(BF16), preferred_element_type=F32)
        up = jnp.dot(h, wu_ref[:, sl].astype(BF16), preferred_element_type=F32)
        act = (_silu(gate) * up).astype(BF16)
        acc = acc + jnp.dot(act, wd_ref[sl, :].astype(BF16), preferred_element_type=F32)
    y = x + FFN_HALF * acc
    if final:
        y = _rmsnorm(y, fg_ref[...])
    o_ref[...] = y


def _ffn_call(x, g, wg, wu, wd, fg, final):
    t, d = x.shape
    tm = min(FFN_TM, t)
    return pl.pallas_call(
        functools.partial(_ffn_kernel, final=final),
        grid=(t // tm,),
        in_specs=[pl.BlockSpec((tm, d), lambda i: (i, 0)),
                  _const_spec((1, d)), _const_spec((d, D_FF)), _const_spec((d, D_FF)),
                  _const_spec((D_FF, d)), _const_spec((1, d))],
        out_specs=pl.BlockSpec((tm, d), lambda i: (i, 0)),
        out_shape=jax.ShapeDtypeStruct((t, d), F32),
        compiler_params=_params(("parallel",), FFN_VMEM_LIMIT),
        name="ffn_final" if final else "ffn",
    )(x, g, wg, wu, wd, fg)


INPROJ_TM = 1024
INPROJ_COLS = 256


def _inproj_kernel(x_ref, g_ref, wrw_ref, wml_ref, wif_ref, mu_ref, convw_ref, convb_ref,
                   zrw_ref, zml_ref, zif_ref, carry_rw_ref, carry_ml_ref, buf_a_ref, buf_b_ref):
    @pl.when(pl.program_id(1) == 0)
    def _():
        carry_rw_ref[...] = jnp.zeros(carry_rw_ref.shape, F32)
        carry_ml_ref[...] = jnp.zeros(carry_ml_ref.shape, F32)

    rows = x_ref.shape[0]
    h = _rmsnorm(x_ref[...], g_ref[...]).astype(BF16)

    def conv_stage(cols, scale):
        def finish(u):
            carry = carry_ml_ref[:, cols]
            w0, w1, w2, w3 = (convw_ref[j:j + 1, cols] for j in range(ML_CONV))
            u1 = _shift_rows_via(buf_a_ref, u, carry, 1)
            pair = w2 * u + w3 * u1
            pair_carry = w2 * carry + w3 * pltpu.roll(carry, 1, axis=0)
            conv = convb_ref[:, cols] + w0 * u + w1 * u1 + _shift_rows_via(buf_b_ref, pair, pair_carry, 2)
            carry_ml_ref[:, cols] = u[rows - SUBLANES:]
            zml_ref[:, cols] = _silu(conv) * scale if scale != 1.0 else _silu(conv)
        return wml_ref, cols, finish

    def plain_stage(cols, fn):
        def finish(u):
            zml_ref[:, cols] = fn(u)
        return wml_ref, cols, finish

    def lerp_stage(cols):
        def finish(z):
            prev = _shift_rows_via(buf_a_ref, z, carry_rw_ref[:, cols], 1)
            carry_rw_ref[:, cols] = z[rows - SUBLANES:]
            zrw_ref[:, cols] = z + (prev - z) * mu_ref[:, cols]
        return wrw_ref, cols, finish

    blk = lambda c: slice(c * INPROJ_COLS, (c + 1) * INPROJ_COLS)
    n_head_blocks = ML_WIDTH // INPROJ_COLS
    heavy, light = [], []
    for c in range(ML_QKVO // INPROJ_COLS):
        group = c // n_head_blocks
        if group < 2:
            heavy.append(conv_stage(blk(c), ML_HEAD_DIM ** -0.5 if group == 0 else 1.0))
        else:
            light.append(plain_stage(blk(c), _sigmoid if group == 3 else (lambda u: u)))
    light += [lerp_stage(blk(c)) for c in range(RW_COLS // INPROJ_COLS)]

    per_heavy = -(-len(light) // len(heavy))
    stages = []
    for i, stage in enumerate(heavy):
        stages.append(stage)
        stages += light[i * per_heavy:(i + 1) * per_heavy]
    pending = None
    for w_ref, cols, finish in stages:
        acc = jnp.dot(h, w_ref[:, cols], preferred_element_type=F32)
        if pending is not None:
            pending[0](pending[1])
        pending = (finish, acc)
    zif_ref[...] = jnp.dot(h, wif_ref[...], preferred_element_type=F32)
    pending[0](pending[1])


def _inproj_call(x, g, wrw, wml, wif, mu, convw, convb, bsz, seq):
    t, d = x.shape
    tm = min(INPROJ_TM, seq)
    nt = seq // tm
    row = lambda b, i: (b * nt + i, 0)
    return pl.pallas_call(
        _inproj_kernel,
        grid=(bsz, nt),
        in_specs=[pl.BlockSpec((tm, d), row),
                  _const_spec((1, d)), _const_spec((d, RW_COLS)), _const_spec((d, ML_QKVO)),
                  _const_spec((d, IF_PAD)), _const_spec((1, RW_COLS)),
                  _const_spec(convw.shape), _const_spec(convb.shape)],
        out_specs=[pl.BlockSpec((tm, RW_COLS), row), pl.BlockSpec((tm, ML_QKVO), row),
                   pl.BlockSpec((tm, IF_PAD), row)],
        out_shape=[jax.ShapeDtypeStruct((t, RW_COLS), F32), jax.ShapeDtypeStruct((t, ML_QKVO), F32),
                   jax.ShapeDtypeStruct((t, IF_PAD), F32)],
        scratch_shapes=[pltpu.VMEM((SUBLANES, RW_COLS), F32), pltpu.VMEM((SUBLANES, 2 * ML_WIDTH), F32),
                        pltpu.VMEM((SUBLANES + tm, INPROJ_COLS), F32),
                        pltpu.VMEM((SUBLANES + tm, INPROJ_COLS), F32)],
        compiler_params=_params(("parallel", "arbitrary")),
        name="inproj",
    )(x, g, wrw, wml, wif, mu, convw, convb)


RW1_CHUNKS = 16
RW1_GROUP = 16
INV_LEVELS = 6


def _rw_masks():
    t = np.arange(CHUNK)[:, None]
    s = np.tile(np.arange(CHUNK), 2)[None, :]
    levels = []
    for k in range(INV_LEVELS):
        size = 1 << k
        levels.append(((t // (2 * size)) == (s // (2 * size))) & ((t // size) % 2 == 1) & ((s // size) % 2 == 0))
    m = np.stack([s < t, s <= t, s == t] + levels).astype(np.float32)
    tri = (np.arange(CHUNK)[:, None] >= np.arange(CHUNK)[None, :]).astype(np.float32)
    return m, tri


def _group_ones(width, group):
    i = np.arange(width)
    return (i[:, None] // group == i[None, :] // group).astype(np.float32)


def _rw1_kernel(z_ref, w2_ref, a2_ref, g2_ref, vec_ref, gsum_ref, tri_ref, mask_ref,
                g_ref, yi_ref, bonus_ref, gate_ref, m_ref, c_ref):
    z = z_ref[...]
    rows = z.shape[0]
    r = z[:, 0:RW_WIDTH]
    k = z[:, RW_WIDTH:2 * RW_WIDTH]
    v = z[:, 2 * RW_WIDTH:3 * RW_WIDTH]
    lora_in = z[:, 3 * RW_WIDTH:3 * RW_WIDTH + DECAY_LORA + AAA_LORA]
    g_d = z[:, 3 * RW_WIDTH + DECAY_LORA + AAA_LORA:]
    w0 = vec_ref[0:1, :]
    a0 = vec_ref[1:2, :]
    k_k = vec_ref[2:3, :]
    k_a = vec_ref[3:4, :]
    r_k = vec_ref[4:5, :]

    w_raw = w0 + _bdot(jnp.tanh(lora_in), w2_ref[...])
    logw = -math.exp(-0.5) * _sigmoid(w_raw)
    a = _sigmoid(a0 + _bdot(lora_in, a2_ref[...]))
    gate_ref[...] = _bdot(_sigmoid(g_d), g2_ref[...])

    gsum = gsum_ref[...]
    kk = k * k_k
    kk = kk / jnp.maximum(jnp.sqrt(_per_slab(_bdot, kk * kk, gsum)), 1e-12)
    k_mod = k * (1.0 + (a - 1.0) * k_a)
    b = kk * a
    bonus_ref[...] = _per_slab(_dot_split2_rhs, r * k_mod * r_k, gsum) * v

    lane = lax.broadcasted_iota(jnp.int32, (CHUNK, PAIR), 1)
    head0 = lane < RW_HEAD_DIM
    strict = mask_ref[0]
    incl = mask_ref[1]
    eye = mask_ref[2]
    tri = tri_ref[...]

    def stack(x):
        return jnp.concatenate([jnp.where(head0, x, 0.0), jnp.where(head0, 0.0, x)], axis=0).astype(BF16)

    ops = {}
    for c in range(rows // CHUNK):
        rs = slice(c * CHUNK, (c + 1) * CHUNK)
        lw = logw[rs]
        cum = _dot_exact_lhs(tri, lw)
        cum_last = cum[CHUNK - 1:CHUNK]
        w_incl = jnp.exp(cum)
        w_inv = 1.0 / w_incl
        w_prev = jnp.exp(cum - lw)
        w_last = jnp.exp(cum_last)
        bt = b[rs] * w_inv
        kt = k_mod[rs] * w_inv
        full = dict(at=-kk[rs] * w_prev, rt=r[rs] * w_incl, bt=bt, kt=kt, be=bt * w_last, ke=kt * w_last, v2=v[rs])
        for p in range(RW_PAIRS):
            ls = slice(p * PAIR, (p + 1) * PAIR)
            item = {name: val[:, ls] for name, val in full.items()}
            item["w_last"] = w_last[:, ls]
            ops[(c, p)] = item

    keys = list(ops)
    for g0 in range(0, len(keys), RW1_GROUP):
        group = keys[g0:g0 + RW1_GROUP]
        n_ab, a_ak, a_rb, a_rk, t_inv, sv = {}, {}, {}, {}, {}, {}
        for key in group:
            o = ops[key]
            prod = _bdot_nt(jnp.concatenate([o["at"], o["rt"]], axis=0),
                            jnp.concatenate([stack(o["bt"]), stack(o["kt"])], axis=0))
            n_ab[key] = prod[:CHUNK, :PAIR] * strict
            a_ak[key] = prod[:CHUNK, PAIR:] * strict
            a_rb[key] = prod[CHUNK:, :PAIR] * incl
            a_rk[key] = prod[CHUNK:, PAIR:] * incl
            sv[key] = stack(o["v2"])
            t_inv[key] = eye + n_ab[key] * mask_ref[3]
        for lvl in range(1, INV_LEVELS):
            half = {key: _bdot(t_inv[key], stack(n_ab[key] * mask_ref[3 + lvl])) for key in group}
            t_inv = {key: t_inv[key] + _bdot(half[key], stack(t_inv[key])) for key in group}
        akv = {key: _bdot(a_ak[key], sv[key]) for key in group}
        pq = {key: _bdot(t_inv[key], jnp.concatenate([stack(ops[key]["at"]), stack(akv[key])], axis=1))
              for key in group}
        gy = {key: _bdot(a_rb[key], jnp.concatenate([stack(pq[key][:, :PAIR]), stack(pq[key][:, PAIR:])], axis=1))
              for key in group}
        ark_v = {key: _bdot(a_rk[key], sv[key]) for key in group}
        mc = {key: _bdot_tn(ops[key]["be"], pq[key]) for key in group}
        ke_v = {key: _bdot_tn(ops[key]["ke"], ops[key]["v2"]) for key in group}
        for key in group:
            c, p = key
            rs = slice(c * CHUNK, (c + 1) * CHUNK)
            ls = slice(p * PAIR, (p + 1) * PAIR)
            g_ref[rs, ls] = (ops[key]["rt"] + gy[key][:, :PAIR]).astype(BF16)
            yi_ref[rs, ls] = gy[key][:, PAIR:] + ark_v[key]
            cv = mc[key][:, PAIR:] + ke_v[key]
            m_ref[c, p] = eye * ops[key]["w_last"] + jnp.where(head0, mc[key][:CHUNK, :PAIR], mc[key][CHUNK:, :PAIR])
            c_ref[c, p] = jnp.where(head0, cv[:CHUNK], cv[CHUNK:])


def _rw1_call(zrw, w2p, a2p, g2, vec, gsum, tri, masks):
    t = zrw.shape[0]
    rows = RW1_CHUNKS * CHUNK
    nchunks = t // CHUNK
    tok = lambda i: (i, 0)
    blk = lambda i: (i, 0, 0, 0)
    tok_spec = pl.BlockSpec((rows, RW_WIDTH), tok)
    mc_spec = pl.BlockSpec((RW1_CHUNKS, RW_PAIRS, CHUNK, PAIR), blk)
    tok_shape = jax.ShapeDtypeStruct((t, RW_WIDTH), F32)
    mc_shape = jax.ShapeDtypeStruct((nchunks, RW_PAIRS, CHUNK, PAIR), F32)
    return pl.pallas_call(
        _rw1_kernel,
        grid=(t // rows,),
        in_specs=[pl.BlockSpec((rows, RW_COLS), tok),
                  _const_spec(w2p.shape), _const_spec(a2p.shape), _const_spec(g2.shape),
                  _const_spec(vec.shape), _const_spec(gsum.shape), _const_spec(tri.shape),
                  _const_spec(masks.shape)],
        out_specs=[tok_spec, tok_spec, tok_spec, tok_spec, mc_spec, mc_spec],
        out_shape=[jax.ShapeDtypeStruct((t, RW_WIDTH), BF16), tok_shape, tok_shape, tok_shape, mc_shape, mc_shape],
        compiler_params=_params(("parallel",)),
        name="rwkv_chunk_local",
    )(zrw, w2p, a2p, g2, vec, gsum, tri, masks)


RW2_CHUNKS = 4
RW2_BATCH = 4


def _rw2_kernel(g_ref, yi_ref, bonus_ref, gate_ref, m_ref, c_ref, gmean_ref, gn_ref, y_ref, h_ref):
    @pl.when(pl.program_id(1) == 0)
    def _():
        h_ref[...] = jnp.zeros(h_ref.shape, F32)

    lane = lax.broadcasted_iota(jnp.int32, (CHUNK, PAIR), 1)
    head0 = lane < RW_HEAD_DIM

    def block_diag(x):
        return jnp.concatenate([jnp.where(head0, x, 0.0), jnp.where(head0, 0.0, x)], axis=0)

    nb = g_ref.shape[0]
    chains = [(b, p) for b in range(nb) for p in range(RW_PAIRS)]
    lsl = [slice(p * PAIR, (p + 1) * PAIR) for p in range(RW_PAIRS)]
    state = {(b, p): h_ref[b, p] for b, p in chains}
    ys = {}
    for c in range(RW2_CHUNKS):
        rs = slice(c * CHUNK, (c + 1) * CHUNK)
        for b, p in chains:
            ys[b, c, p] = _bdot(g_ref[b, rs, lsl[p]], state[b, p]) + yi_ref[b, rs, lsl[p]]
        state = {(b, p): _dot_f32(block_diag(m_ref[b, c, p]), state[b, p]) + block_diag(c_ref[b, c, p])
                 for b, p in chains}
    for b, p in chains:
        h_ref[b, p] = state[b, p]
    gmean = gmean_ref[...]
    for b in range(nb):
        y = jnp.concatenate([jnp.concatenate([ys[b, c, p] for p in range(RW_PAIRS)], axis=1)
                             for c in range(RW2_CHUNKS)], axis=0)
        mu = _per_slab(_dot_split2_rhs, y, gmean)
        d = y - mu
        var = _per_slab(_bdot, d * d, gmean)
        y = d * lax.rsqrt(var + GN_EPS) * gn_ref[0:1, :] + gn_ref[1:2, :]
        y_ref[b] = ((y + bonus_ref[b]) * gate_ref[b]).astype(BF16)


def _rw2_call(g, yi, bonus, gate, m, c, gmean, gn, bsz, seq):
    t = g.shape[0]
    rows = RW2_CHUNKS * CHUNK
    nt = seq // rows
    nb = RW2_BATCH if bsz % RW2_BATCH == 0 else 1
    per_batch = lambda a: a.reshape((bsz, a.shape[0] // bsz) + a.shape[1:])
    tok_spec = pl.BlockSpec((nb, rows, RW_WIDTH), lambda b, i: (b, i, 0))
    mc_spec = pl.BlockSpec((nb, RW2_CHUNKS, RW_PAIRS, CHUNK, PAIR), lambda b, i: (b, i, 0, 0, 0))
    y = pl.pallas_call(
        _rw2_kernel,
        grid=(bsz // nb, nt),
        in_specs=[tok_spec, tok_spec, tok_spec, tok_spec, mc_spec, mc_spec,
                  _const_spec(gmean.shape), _const_spec(gn.shape)],
        out_specs=tok_spec,
        out_shape=jax.ShapeDtypeStruct((bsz, seq, RW_WIDTH), BF16),
        scratch_shapes=[pltpu.VMEM((nb, RW_PAIRS, PAIR, PAIR), F32)],
        compiler_params=_params(("parallel", "arbitrary")),
        name="rwkv_scan",
    )(per_batch(g), per_batch(yi), per_batch(bonus), per_batch(gate), per_batch(m), per_batch(c), gmean, gn)
    return y.reshape(t, RW_WIDTH)


ML_CHUNK = 128
ML_STEP_CHUNKS = 4
ML_BATCH = 1


def _log_sigmoid(x):
    return jnp.minimum(x, 0.0) - jnp.log(1.0 + jnp.exp(-jnp.abs(x)))


def _ml_kernel(z_ref, zif_ref, ifb_ref, norm_ref, tri_ref, y_ref, state_ref, m_ref):
    @pl.when(pl.program_id(1) == 0)
    def _():
        state_ref[...] = jnp.zeros(state_ref.shape, F32)
        m_ref[...] = jnp.zeros(m_ref.shape, F32)

    nb, rows = z_ref.shape[0], z_ref.shape[1]
    q_all = [z_ref[b, :, 0:ML_WIDTH] for b in range(nb)]
    k_all = [z_ref[b, :, ML_WIDTH:2 * ML_WIDTH] for b in range(nb)]
    v_all = [z_ref[b, :, 2 * ML_WIDTH:3 * ML_WIDTH] for b in range(nb)]
    ogate_all = [z_ref[b, :, 3 * ML_WIDTH:4 * ML_WIDTH] for b in range(nb)]

    pre = [zif_ref[b] + ifb_ref[...] for b in range(nb)]
    logf = [_log_sigmoid(pre[b]) for b in range(nb)]
    tri = tri_ref[...]
    lane = lax.broadcasted_iota(jnp.int32, (ML_CHUNK, LANES), 1)
    tq = lax.broadcasted_iota(jnp.int32, (ML_CHUNK, ML_CHUNK), 0)
    tk = lax.broadcasted_iota(jnp.int32, (ML_CHUNK, ML_CHUNK), 1)
    causal = tk <= tq
    ones = jnp.ones((ML_CHUNK, ML_HEAD_DIM), F32)
    dense = (ML_CHUNK, ML_CHUNK)

    chains = [(b, h) for b in range(nb) for h in range(ML_HEADS)]
    lsl = [slice(h * ML_HEAD_DIM, (h + 1) * ML_HEAD_DIM) for h in range(ML_HEADS)]
    m_run = {ch: m_ref[ch] for ch in chains}
    state_run = {ch: state_ref[ch] for ch in chains}
    ys = {b: [] for b in range(nb)}
    for c in range(rows // ML_CHUNK):
        rs = slice(c * ML_CHUNK, (c + 1) * ML_CHUNK)
        z, z_t = [], []
        for b in range(nb):
            bcum = _dot_exact_lhs(tri, logf[b][rs])
            z.append(jnp.where(lane < ML_HEADS, pre[b][rs], bcum))
            z_t.append(z[b].T)
        b_col = {(b, h): jnp.broadcast_to(z[b][:, ML_HEADS + h:ML_HEADS + h + 1], dense)
                 for b, h in chains}
        ib_col = {(b, h): jnp.broadcast_to(z[b][:, h:h + 1] - z[b][:, ML_HEADS + h:ML_HEADS + h + 1], dense)
                  for b, h in chains}
        ib_row = {(b, h): z_t[b][h:h + 1, :] - z_t[b][ML_HEADS + h:ML_HEADS + h + 1, :]
                  for b, h in chains}
        b_end = {ch: b_col[ch][ML_CHUNK - 1:ML_CHUNK] for ch in chains}
        m_prev = m_run
        state = state_run
        q = {(b, h): q_all[b][rs, lsl[h]] for b, h in chains}
        k = {(b, h): k_all[b][rs, lsl[h]] for b, h in chains}
        v = {(b, h): v_all[b][rs, lsl[h]] for b, h in chains}
        s_qk = {ch: _bdot_nt(q[ch], k[ch]) for ch in chains}
        inter = {ch: _bdot(q[ch], state[ch]) for ch in chains}

        log_d = {ch: jnp.where(causal, b_col[ch] + ib_row[ch], NEG_BIG) for ch in chains}
        log_inter = {ch: b_col[ch] + m_prev[ch] for ch in chains}
        m_row = {ch: jnp.maximum(log_inter[ch], jnp.max(log_d[ch], axis=-1, keepdims=True)) for ch in chains}
        p = {ch: jnp.exp(log_d[ch] - m_row[ch]) * s_qk[ch] for ch in chains}
        intra = {ch: _bdot(p[ch], jnp.concatenate([v[ch], ones], axis=1)) for ch in chains}

        log_w_end = {ch: ib_col[ch] + b_end[ch] for ch in chains}
        m_chunk = {ch: jnp.max(log_w_end[ch], axis=0, keepdims=True) for ch in chains}
        w_end = {ch: jnp.exp(log_w_end[ch] - m_chunk[ch]) for ch in chains}
        kv = {ch: _bdot_tn(k[ch], jnp.concatenate([w_end[ch] * v[ch], w_end[ch]], axis=1)) for ch in chains}
        m_run, state_run = {}, {}
        for ch in chains:
            m_new = jnp.maximum(b_end[ch] + m_prev[ch], m_chunk[ch])
            dec = jnp.exp(b_end[ch] + m_prev[ch] - m_new)
            sc = jnp.exp(m_chunk[ch] - m_new)
            state_run[ch] = (jnp.concatenate([dec, dec], axis=1) * state[ch]
                             + jnp.concatenate([sc, sc], axis=1) * kv[ch])
            m_run[ch] = m_new

        outs = {}
        for ch in chains:
            b, h = ch
            w_inter = jnp.exp(log_inter[ch] - m_row[ch])
            num = intra[ch][:, :ML_HEAD_DIM] + w_inter * inter[ch][:, :ML_HEAD_DIM]
            den = intra[ch][:, ML_HEAD_DIM:] + w_inter * inter[ch][:, ML_HEAD_DIM:]
            hh = num / jnp.maximum(jnp.abs(den), jnp.exp(-m_row[ch]))
            mu = jnp.mean(hh, axis=-1, keepdims=True)
            d = hh - mu
            var = jnp.mean(d * d, axis=-1, keepdims=True)
            hn = d * lax.rsqrt(var + ML_NORM_EPS)
            outs[ch] = hn * norm_ref[:, lsl[h]] * ogate_all[b][rs, lsl[h]]
        for b in range(nb):
            ys[b].append(jnp.concatenate([outs[b, h] for h in range(ML_HEADS)], axis=1).astype(BF16))
    for b in range(nb):
        y_ref[b] = jnp.concatenate(ys[b], axis=0)
    for ch in chains:
        state_ref[ch] = state_run[ch]
        m_ref[ch] = m_run[ch]


def _ml_call(zml, zif, ifb, norm, tri, bsz, seq):
    t = zml.shape[0]
    rows = ML_STEP_CHUNKS * ML_CHUNK
    nt = seq // rows
    nb = ML_BATCH if bsz % ML_BATCH == 0 else 1
    tok = lambda b, i: (b, i, 0)
    y = pl.pallas_call(
        _ml_kernel,
        grid=(bsz // nb, nt),
        in_specs=[pl.BlockSpec((nb, rows, ML_QKVO), tok), pl.BlockSpec((nb, rows, IF_PAD), tok),
                  _const_spec(ifb.shape), _const_spec(norm.shape), _const_spec(tri.shape)],
        out_specs=pl.BlockSpec((nb, rows, ML_WIDTH), tok),
        out_shape=jax.ShapeDtypeStruct((bsz, seq, ML_WIDTH), BF16),
        scratch_shapes=[pltpu.VMEM((nb, ML_HEADS, ML_HEAD_DIM, 2 * ML_HEAD_DIM), F32),
                        pltpu.VMEM((nb, ML_HEADS, 1, LANES), F32)],
        compiler_params=_params(("parallel", "arbitrary")),
        name="mlstm",
    )(zml.reshape(bsz, seq, ML_QKVO), zif.reshape(bsz, seq, IF_PAD), ifb, norm, tri)
    return y.reshape(t, ML_WIDTH)


MERGE_TM = 1024


def _merge_kernel(x_ref, g_ref, wgt_ref, yrw_ref, yml_ref, wur_ref, wum_ref, wo_ref, o_ref):
    x = x_ref[...]
    d = x.shape[1]
    h = _rmsnorm(x, g_ref[...]).astype(BF16)
    gates = _sigmoid(jnp.dot(h, wgt_ref[...], preferred_element_type=F32))
    up_rw = _bdot(yrw_ref[...], wur_ref[...])
    up_ml = _bdot(yml_ref[...], wum_ref[...])
    merged = gates[:, 0:d] * up_rw + gates[:, d:] * up_ml
    o_ref[...] = x + _bdot(merged, wo_ref[...])


def _merge_call(x, g, wgt, yrw, yml, wur, wum, wo):
    t, d = x.shape
    tm = min(MERGE_TM, t)
    row = lambda i: (i, 0)
    return pl.pallas_call(
        _merge_kernel,
        grid=(t // tm,),
        in_specs=[pl.BlockSpec((tm, d), row), _const_spec(g.shape), _const_spec(wgt.shape),
                  pl.BlockSpec((tm, RW_WIDTH), row), pl.BlockSpec((tm, ML_WIDTH), row),
                  _const_spec(wur.shape), _const_spec(wum.shape), _const_spec(wo.shape)],
        out_specs=pl.BlockSpec((tm, d), row),
        out_shape=jax.ShapeDtypeStruct((t, d), F32),
        compiler_params=_params(("parallel",)),
        name="merge",
    )(x, g, wgt, yrw, yml, wur, wum, wo)


def _memkv_kernel(mem_ref, g_ref, wkv_ref, k_ref, v_ref):
    d = mem_ref.shape[1]
    hm = _rmsnorm(mem_ref[...], g_ref[...])
    kv = jnp.dot(hm, wkv_ref[...], preferred_element_type=F32)
    k_ref[...] = kv[:, 0:d].astype(BF16)
    v_ref[...] = kv[:, d:].astype(BF16)


def _memkv_call(mem2d, g, wkv):
    n, d = mem2d.shape
    row = lambda i: (i, 0)
    return pl.pallas_call(
        _memkv_kernel,
        grid=(n // MEM_LEN,),
        in_specs=[pl.BlockSpec((MEM_LEN, d), row), _const_spec(g.shape), _const_spec(wkv.shape)],
        out_specs=[pl.BlockSpec((MEM_LEN, d), row), pl.BlockSpec((MEM_LEN, d), row)],
        out_shape=[jax.ShapeDtypeStruct((n, d), BF16), jax.ShapeDtypeStruct((n, d), BF16)],
        compiler_params=_params(("parallel",)),
        name="mem_kv",
    )(mem2d, g, wkv)


XA_TM = 1024


def _xattn_kernel(x_ref, g_ref, wq_ref, k_ref, v_ref, wo_ref, o_ref):
    x = x_ref[...]
    hq = _rmsnorm(x, g_ref[...])
    q = _bdot(hq, wq_ref[...]) * (XA_HEAD_DIM ** -0.5)
    lsl = [slice(h * XA_HEAD_DIM, (h + 1) * XA_HEAD_DIM) for h in range(XA_HEADS)]

    def softmax(s):
        e = jnp.exp(s - jnp.max(s, axis=-1, keepdims=True))
        return e / jnp.sum(e, axis=-1, keepdims=True)

    outs = []
    s_prev = _bdot_nt(q[:, lsl[0]], k_ref[:, lsl[0]])
    for h in range(XA_HEADS):
        s_next = _bdot_nt(q[:, lsl[h + 1]], k_ref[:, lsl[h + 1]]) if h + 1 < XA_HEADS else None
        outs.append(_bdot(softmax(s_prev), v_ref[:, lsl[h]]))
        s_prev = s_next
    o = jnp.concatenate(outs, axis=1)
    o_ref[...] = x + _bdot(o, wo_ref[...])


def _xattn_call(x, g, wq, kmem, vmem, wo, bsz, seq):
    t, d = x.shape
    tm = min(XA_TM, seq)
    nt = seq // tm
    row = lambda b, i: (b * nt + i, 0)
    mem = lambda b, i: (b, 0)
    return pl.pallas_call(
        _xattn_kernel,
        grid=(bsz, nt),
        in_specs=[pl.BlockSpec((tm, d), row), _const_spec(g.shape), _const_spec(wq.shape),
                  pl.BlockSpec((MEM_LEN, d), mem), pl.BlockSpec((MEM_LEN, d), mem),
                  _const_spec(wo.shape)],
        out_specs=pl.BlockSpec((tm, d), row),
        out_shape=jax.ShapeDtypeStruct((t, d), F32),
        compiler_params=_params(("parallel", "parallel")),
        name="mem_xattn",
    )(x, g, wq, kmem, vmem, wo)


def _row(v):
    return v.reshape(1, -1).astype(F32)


def _layer(x, mem2d, bsz, seq, p):
    d = D_MODEL
    bf = lambda w: w.astype(BF16)
    masks_np, tri_np = _rw_masks()
    masks = jnp.asarray(masks_np)
    tri = jnp.asarray(tri_np, dtype=BF16)
    gsum = jnp.asarray(_group_ones(PAIR, RW_HEAD_DIM), dtype=BF16)
    gmean = jnp.asarray(_group_ones(PAIR, RW_HEAD_DIM) / RW_HEAD_DIM, dtype=BF16)

    x = _ffn_call(x, _row(p["ffn1_norm"]), p["ffn1_w_gate"], p["ffn1_w_up"], p["ffn1_w_down"],
                  _row(p["ffn1_norm"]), final=False)

    w_in = p["w_in"]
    c0 = RW_COLS
    c1 = c0 + ML_QKVO
    c2 = c1 + 2 * ML_HEADS
    w_if = jnp.pad(w_in[:, c1:c2], ((0, 0), (0, IF_PAD - 2 * ML_HEADS)))
    convw = jnp.concatenate([p["ml_conv_w"].astype(F32), jnp.zeros((SUBLANES - ML_CONV, 2 * ML_WIDTH), F32)], axis=0)
    zrw, zml, zif = _inproj_call(x, _row(p["mix_norm"]), bf(w_in[:, :c0]), bf(w_in[:, c0:c1]), bf(w_if),
                                 _row(p["rw_mu"]), convw, _row(p["ml_conv_b"]), bsz, seq)

    zpad = jnp.zeros((DECAY_LORA, RW_WIDTH), F32)
    w2p = bf(jnp.concatenate([p["rw_w2"], zpad], axis=0))
    a2p = bf(jnp.concatenate([zpad, p["rw_a2"]], axis=0))
    vec = jnp.stack([p["rw_w0"], p["rw_a0"], p["rw_k_k"], p["rw_k_a"], p["rw_r_k"].reshape(-1),
                     jnp.zeros_like(p["rw_w0"]), jnp.zeros_like(p["rw_w0"]), jnp.zeros_like(p["rw_w0"])]).astype(F32)
    g, yi, bonus, gate, m, c = _rw1_call(zrw, w2p, a2p, bf(p["rw_g2"]), vec, gsum, tri, masks)
    gn = jnp.concatenate([jnp.stack([p["rw_gn_w"], p["rw_gn_b"]]).astype(F32),
                          jnp.zeros((SUBLANES - 2, RW_WIDTH), F32)], axis=0)
    y_rw = _rw2_call(g, yi, bonus, gate, m, c, gmean, gn, bsz, seq)

    ifb = jnp.pad(jnp.concatenate([p["ml_i_b"], p["ml_f_b"]]).astype(F32), (0, IF_PAD - 2 * ML_HEADS)).reshape(1, -1)
    tri_ml = jnp.asarray(np.tril(np.ones((ML_CHUNK, ML_CHUNK), np.float32)), dtype=BF16)
    y_ml = _ml_call(zml, zif, ifb, _row(p["ml_norm"]), tri_ml, bsz, seq)

    x = _merge_call(x, _row(p["mix_norm"]), bf(w_in[:, c2:]), y_rw, y_ml,
                    p["w_up_rw"], p["w_up_ml"], p["w_out"])

    kmem, vmem = _memkv_call(mem2d, _row(p["mem_norm"]), p["xa_wkv"])
    x = _xattn_call(x, _row(p["xa_norm"]), p["xa_wq"], kmem, vmem, p["xa_wo"], bsz, seq)
    return x


def kernel(x, mem, ffn1_norm, ffn1_w_gate, ffn1_w_up, ffn1_w_down, mix_norm, w_in, rw_mu, rw_w0, rw_w2, rw_a0,
           rw_a2, rw_g2, rw_k_k, rw_k_a, rw_r_k, rw_gn_w, rw_gn_b, ml_conv_w, ml_conv_b, ml_i_b, ml_f_b, ml_norm,
           w_up_rw, w_up_ml, w_out, xa_norm, mem_norm, xa_wq, xa_wkv, xa_wo, ffn2_norm, ffn2_w_gate, ffn2_w_up,
           ffn2_w_down, final_norm):
    bsz, seq, d = x.shape
    depth = ffn1_norm.shape[0]
    stacked = dict(ffn1_norm=ffn1_norm, ffn1_w_gate=ffn1_w_gate, ffn1_w_up=ffn1_w_up, ffn1_w_down=ffn1_w_down,
                   mix_norm=mix_norm, w_in=w_in, rw_mu=rw_mu, rw_w0=rw_w0, rw_w2=rw_w2, rw_a0=rw_a0, rw_a2=rw_a2,
                   rw_g2=rw_g2, rw_k_k=rw_k_k, rw_k_a=rw_k_a, rw_r_k=rw_r_k, rw_gn_w=rw_gn_w, rw_gn_b=rw_gn_b,
                   ml_conv_w=ml_conv_w, ml_conv_b=ml_conv_b, ml_i_b=ml_i_b, ml_f_b=ml_f_b, ml_norm=ml_norm,
                   w_up_rw=w_up_rw, w_up_ml=w_up_ml, w_out=w_out, xa_norm=xa_norm, mem_norm=mem_norm,
                   xa_wq=xa_wq, xa_wkv=xa_wkv, xa_wo=xa_wo, ffn2_norm=ffn2_norm, ffn2_w_gate=ffn2_w_gate,
                   ffn2_w_up=ffn2_w_up, ffn2_w_down=ffn2_w_down)
    h = x.reshape(bsz * seq, d)
    mem2d = mem.reshape(bsz * mem.shape[1], d)
    bf = lambda w: w.astype(BF16)
    for l in range(depth):
        p = {name: w[l] for name, w in stacked.items()}
        h = _layer(h, mem2d, bsz, seq, p)
        last = l == depth - 1
        h = _ffn_call(h, _row(p["ffn2_norm"]), p["ffn2_w_gate"], p["ffn2_w_up"], p["ffn2_w_down"],
                      _row(final_norm), final=last)
    return h.reshape(bsz, seq, d)
```
